```python
import math
import jax, jax.numpy as jnp
from jax import lax
import numpy as np

D_MODEL = 1024
BATCH = 4
SEQ = 4096
DEPTH = 2

GRID_W = 64
CTX_LEN = 256
HEAD_DIM = 64
D_FF = 2816
N_MOD = 9
RMS_EPS = 1e-6
RWKV_WIDTH = D_MODEL // 2
RWKV_HEADS = RWKV_WIDTH // HEAD_DIM
RWKV_DECAY_LORA = 64
RWKV_ICLR_LORA = 64
RWKV_GATE_LORA = 128
RWKV_PROJ = 3 * RWKV_WIDTH + RWKV_DECAY_LORA + RWKV_ICLR_LORA + RWKV_GATE_LORA
GN_EPS = 64e-5
S5_WIDTH = D_MODEL - RWKV_WIDTH
S5_GROUP_CH = 16
S5_GROUPS = S5_WIDTH // S5_GROUP_CH
S5_STATE = 64
AB_PROJ = RWKV_PROJ + S5_WIDTH
ATTN_HEADS = D_MODEL // HEAD_DIM
ATTN_KV_HEADS = 4
ATTN_GROUP = ATTN_HEADS // ATTN_KV_HEADS
ATTN_WINDOW = 128
ATTN_BLOCK = 128
ATTN_Q_W = ATTN_HEADS * HEAD_DIM
ATTN_KV_W = ATTN_KV_HEADS * HEAD_DIM
C_PROJ = ATTN_Q_W + 2 * ATTN_KV_W
ROPE_AXIS_DIM = HEAD_DIM // 2
ROPE_BASE = 10000.0
NEG_INF = -1e30

kernel_name = "hybrid_rwkv7_s5_swa_macaron_dit"


def _rmsnorm(h, g):
    h32 = h.astype(jnp.float32)
    h32 = h32 * lax.rsqrt(jnp.mean(h32 * h32, axis=-1, keepdims=True) + RMS_EPS)
    return (h32 * g).astype(h.dtype)


def _modnorm(h, g, shift, scale):
    return _rmsnorm(h, g) * (1.0 + scale) + shift


def _swiglu(h, w1, w2):
    gate, up = jnp.split(h @ w1, 2, axis=-1)
    return (jax.nn.silu(gate) * up) @ w2


def _shift_mix(p, mu_prev, mu_next):
    prev = jnp.pad(p[:, :-1], ((0, 0), (1, 0), (0, 0)))
    nxt = jnp.pad(p[:, 1:], ((0, 0), (0, 1), (0, 0)))
    return p + mu_prev * (prev - p) + mu_next * (nxt - p)


def _rwkv_scan(seq, reverse):
    bsz = seq[0].shape[1]

    def step(state, inp):
        r_t, w_t, k_t, v_t, a_t, b_t = inp
        sa = jnp.einsum('bhvk,bhk->bhv', state, a_t)
        state = state * w_t[:, :, None, :] + sa[..., None] * b_t[:, :, None, :] + v_t[..., None] * k_t[:, :, None, :]
        return state, jnp.einsum('bhvk,bhk->bhv', state, r_t)

    s0 = jnp.zeros((bsz, RWKV_HEADS, HEAD_DIM, HEAD_DIM), jnp.float32)
    _, y = lax.scan(step, s0, seq, reverse=reverse)
    return y


def _rwkv7(p, lc, w0, w2, a0, a2, g2, k_k, k_a, r_k, lnx_g, lnx_b):
    f32 = jnp.float32
    p = p.astype(f32)
    bsz, L, _ = p.shape
    W = RWKV_WIDTH
    r, k, v, wl, al, gl = jnp.split(
        p, [W, 2 * W, 3 * W, 3 * W + RWKV_DECAY_LORA, 3 * W + RWKV_DECAY_LORA + RWKV_ICLR_LORA], axis=-1)
    heads = lambda t: t.reshape(bsz, L, RWKV_HEADS, HEAD_DIM)
    tm = lambda t: jnp.swapaxes(t, 0, 1)
    kk = heads(k * k_k)
    kk = kk * lax.rsqrt(jnp.maximum(jnp.sum(kk * kk, axis=-1, keepdims=True), 1e-12))
    tanh_wl = jnp.tanh(wl)
    r_h, v_h = heads(r), heads(v)
    y = 0.0
    k_sum = 0.0
    for d in range(2):
        w = -jax.nn.softplus(-(w0[d] + tanh_wl @ w2[d])) - 0.5
        decay = jnp.exp(-jnp.exp(w))
        a = jax.nn.sigmoid(a0[d] + al @ a2[d])
        kd = heads(k * (1.0 + (a - 1.0) * k_a))
        seq = tuple(tm(t) for t in (r_h, heads(decay), kd, v_h, -kk, kk * heads(a)))
        if d == 1:
            seq = tuple(jnp.roll(t, -lc, axis=0) for t in seq)
        yd = _rwkv_scan(seq, reverse=(d == 1))
        if d == 1:
            yd = jnp.roll(yd, lc, axis=0)
        y = y + yd
        k_sum = k_sum + kd
    y = jnp.swapaxes(y, 0, 1)
    mu = jnp.mean(y, axis=-1, keepdims=True)
    var = jnp.mean(jnp.square(y - mu), axis=-1, keepdims=True)
    y = (y - mu) * lax.rsqrt(var + GN_EPS) * lnx_g.reshape(RWKV_HEADS, HEAD_DIM) + lnx_b.reshape(RWKV_HEADS, HEAD_DIM)
    bonus = jnp.sum(r_h * (0.5 * k_sum) * r_k, axis=-1, keepdims=True) * v_h
    g = jax.nn.sigmoid(gl) @ g2
    return (y + bonus).reshape(bsz, L, W) * g


def _complex_affine_combine(e1, e2):
    a1r, a1i, b1r, b1i = e1
    a2r, a2i, b2r, b2i = e2
    return (a2r * a1r - a2i * a1i, a2r * a1i + a2i * a1r,
            a2r * b1r - a2i * b1i + b2r, a2r * b1i + a2i * b1r + b2i)


def _s5_scan(u_tm, a_re, a_im, log_step, b_re, b_im, c_re, c_im, reverse):
    f32 = jnp.float32
    lam_re = jnp.minimum(a_re.astype(f32), -1e-4)
    lam_im = a_im.astype(f32)
    dt = jnp.exp(log_step.astype(f32))[:, None]
    mag = jnp.exp(lam_re * dt)
    ab_re, ab_im = mag * jnp.cos(lam_im * dt), mag * jnp.sin(lam_im * dt)
    den = lam_re * lam_re + lam_im * lam_im
    f_re = ((ab_re - 1.0) * lam_re + ab_im * lam_im) / den
    f_im = (ab_im * lam_re - (ab_re - 1.0) * lam_im) / den
    b_re, b_im = b_re.astype(f32), b_im.astype(f32)
    bb_re = f_re[..., None] * b_re - f_im[..., None] * b_im
    bb_im = f_re[..., None] * b_im + f_im[..., None] * b_re
    bu_re = jnp.einsum('tbgi,gpi->tbgp', u_tm, bb_re)
    bu_im = jnp.einsum('tbgi,gpi->tbgp', u_tm, bb_im)
    L = u_tm.shape[0]
    a_re_t = jnp.broadcast_to(ab_re, (L, 1) + ab_re.shape)
    a_im_t = jnp.broadcast_to(ab_im, (L, 1) + ab_im.shape)
    _, _, x_re, x_im = lax.associative_scan(
        _complex_affine_combine, (a_re_t, a_im_t, bu_re, bu_im), reverse=reverse, axis=0)
    return (jnp.einsum('tbgp,gip->tbgi', x_re, c_re.astype(f32))
            - jnp.einsum('tbgp,gip->tbgi', x_im, c_im.astype(f32)))


def _s5(u, lc, a_re, a_im, log_step, b_re, b_im, c_re, c_im, d_skip, glu_w, glu_b):
    u = u.astype(jnp.float32)
    bsz, L, _ = u.shape
    u_tm = jnp.swapaxes(u.reshape(bsz, L, S5_GROUPS, S5_GROUP_CH), 0, 1)
    y_f = _s5_scan(u_tm, a_re[0], a_im[0], log_step[0], b_re[0], b_im[0], c_re[0], c_im[0], reverse=False)
    y_b = jnp.roll(_s5_scan(jnp.roll(u_tm, -lc, axis=0), a_re[1], a_im[1], log_step[1],
                            b_re[1], b_im[1], c_re[1], c_im[1], reverse=True), lc, axis=0)
    y = jnp.swapaxes(y_f + y_b, 0, 1).reshape(bsz, L, S5_WIDTH) + d_skip * u
    z = jax.nn.gelu(y)
    return z * jax.nn.sigmoid(z @ glu_w + glu_b)


def _mixer_ab(h_lat, h_ctx, in_w, out_w, mu, w0, w2, a0, a2, g2, k_k, k_a, r_k, lnx_g, lnx_b,
              s_are, s_aim, s_step, s_bre, s_bim, s_cre, s_cim, s_d, glu_w, glu_b):
    lc = h_ctx.shape[1]
    p = jnp.concatenate([h_ctx, h_lat], axis=1) @ in_w
    pa, pb = p[..., :RWKV_PROJ], p[..., RWKV_PROJ:]
    pa = jnp.concatenate([_shift_mix(pa[:, :lc], mu[0], mu[1]), _shift_mix(pa[:, lc:], mu[0], mu[1])], axis=1)
    ya = _rwkv7(pa, lc, w0, w2, a0, a2, g2, k_k, k_a, r_k, lnx_g, lnx_b)
    yb = _s5(pb, lc, s_are, s_aim, s_step, s_bre, s_bim, s_cre, s_cim, s_d, glu_w, glu_b)
    y = jnp.concatenate([ya, yb], axis=-1) @ out_w
    return y[:, lc:], y[:, :lc]


def _rope(t, cos, sin):
    half = HEAD_DIM // 2
    t1, t2 = t[..., :half], t[..., half:]
    cs, sn = cos[None, :, None, :], sin[None, :, None, :]
    return jnp.concatenate([t1 * cs - t2 * sn, t2 * cs + t1 * sn], axis=-1)


def _window_attention(q, k, v, kc, vc, sink):
    f32 = jnp.float32
    bsz, n, _, _ = q.shape
    lc = kc.shape[1]
    nb = n // ATTN_BLOCK
    qb = q.reshape(bsz, nb, ATTN_BLOCK, ATTN_KV_HEADS, ATTN_GROUP, HEAD_DIM)
    pad = ((0, 0), (ATTN_BLOCK, ATTN_BLOCK), (0, 0), (0, 0))
    kp = jnp.pad(k, pad).reshape(bsz, nb + 2, ATTN_BLOCK, ATTN_KV_HEADS, HEAD_DIM)
    vp = jnp.pad(v, pad).reshape(bsz, nb + 2, ATTN_BLOCK, ATTN_KV_HEADS, HEAD_DIM)
    kw = jnp.concatenate([kp[:, :-2], kp[:, 1:-1], kp[:, 2:]], axis=2)
    vw = jnp.concatenate([vp[:, :-2], vp[:, 1:-1], vp[:, 2:]], axis=2)
    qi = jnp.arange(ATTN_BLOCK)[:, None]
    mj = jnp.arange(3 * ATTN_BLOCK)[None, :] - ATTN_BLOCK
    kj = jnp.arange(nb)[:, None, None] * ATTN_BLOCK + mj[None]
    valid = (jnp.abs(mj - qi)[None] <= ATTN_WINDOW) & (kj >= 0) & (kj < n)
    sink_l = sink.astype(f32).reshape(ATTN_KV_HEADS, ATTN_GROUP)[None, :, :, None, None]
    scale = HEAD_DIM ** -0.5
    m_win = 3 * ATTN_BLOCK

    def one_block(args):
        qblk, kblk, vblk, vmask = args
        s_win = jnp.einsum('bqkgd,bmkd->bkgqm', qblk, kblk).astype(f32) * scale
        s_win = jnp.where(vmask, s_win, NEG_INF)
        s_ctx = jnp.einsum('bqkgd,bckd->bkgqc', qblk, kc).astype(f32) * scale
        s_sink = jnp.broadcast_to(sink_l, s_win.shape[:-1] + (1,))
        prob = jax.nn.softmax(jnp.concatenate([s_win, s_ctx, s_sink], axis=-1), axis=-1)
        return (jnp.einsum('bkgqm,bmkd->bqkgd', prob[..., :m_win], vblk)
                + jnp.einsum('bkgqc,bckd->bqkgd', prob[..., m_win:m_win + lc], vc))

    out = lax.map(one_block, (jnp.moveaxis(qb, 1, 0), jnp.moveaxis(kw, 1, 0), jnp.moveaxis(vw, 1, 0), valid))
    return jnp.moveaxis(out, 0, 1).reshape(bsz, n, ATTN_Q_W)


def _context_attention(qc, kc, vc, sink):
    f32 = jnp.float32
    bsz, lc = qc.shape[:2]
    qg = qc.reshape(bsz, lc, ATTN_KV_HEADS, ATTN_GROUP, HEAD_DIM)
    s = jnp.einsum('bckgd,bjkd->bkgcj', qg, kc).astype(f32) * HEAD_DIM ** -0.5
    s_sink = jnp.broadcast_to(sink.astype(f32).reshape(ATTN_KV_HEADS, ATTN_GROUP)[None, :, :, None, None],
                              s.shape[:-1] + (1,))
    prob = jax.nn.softmax(jnp.concatenate([s, s_sink], axis=-1), axis=-1)
    o = jnp.einsum('bkgcj,bjkd->bckgd', prob[..., :lc], vc)
    return o.reshape(bsz, lc, ATTN_Q_W)


def _mixer_c(h_lat, h_ctx, in_w, out_w, sink, cos, sin, need_ctx):
    bsz, n, _ = h_lat.shape
    lc = h_ctx.shape[1]
    p = h_lat @ in_w
    q = _rope(p[..., :ATTN_Q_W].reshape(bsz, n, ATTN_HEADS, HEAD_DIM), cos, sin)
    k = _rope(p[..., ATTN_Q_W:ATTN_Q_W + ATTN_KV_W].reshape(bsz, n, ATTN_KV_HEADS, HEAD_DIM), cos, sin)
    v = p[..., ATTN_Q_W + ATTN_KV_W:].reshape(bsz, n, ATTN_KV_HEADS, HEAD_DIM)
    pc = h_ctx @ in_w[:, ATTN_Q_W:]
    kc = pc[..., :ATTN_KV_W].reshape(bsz, lc, ATTN_KV_HEADS, HEAD_DIM)
    vc = pc[..., ATTN_KV_W:].reshape(bsz, lc, ATTN_KV_HEADS, HEAD_DIM)
    y_lat = _window_attention(q, k, v, kc, vc, sink) @ out_w
    if not need_ctx:
        return y_lat, None
    qc = (h_ctx @ in_w[:, :ATTN_Q_W]).reshape(bsz, lc, ATTN_HEADS, HEAD_DIM)
    return y_lat, _context_attention(qc, kc, vc, sink) @ out_w


def setup_inputs(seed: int = 0) -> dict:
    key = jax.random.key(seed)
    ks = iter(jax.random.split(key, 48))
    f32 = jnp.float32
    D = D_MODEL
    ne, no = (DEPTH + 1) // 2, DEPTH // 2

    def nrm(shape, scale=1.0):
        return scale * jax.random.normal(next(ks), shape, f32)

    def uni(shape, lo, hi):
        return jax.random.uniform(next(ks), shape, f32, lo, hi)

    ramp = jnp.arange(RWKV_WIDTH, dtype=f32) / (RWKV_WIDTH - 1)
    return {
        "x": nrm((BATCH, SEQ, D)),
        "c": nrm((BATCH, D)),
        "ctx": nrm((BATCH, CTX_LEN, D)),
        "c_ctx": nrm((D,)),
        "norm_g": 1.0 + nrm((DEPTH, 3, D), 0.05),
        "mod_w": nrm((DEPTH, D, N_MOD * D), 0.5 * D ** -0.5),
        "mod_b": nrm((DEPTH, N_MOD * D), 0.02),
        "ffn_w1": nrm((DEPTH, 2, D, 2 * D_FF), D ** -0.5),
        "ffn_w2": nrm((DEPTH, 2, D_FF, D), D_FF ** -0.5),
        "ab_in_w": nrm((ne, D, AB_PROJ), D ** -0.5),
        "ab_out_w": nrm((ne, D, D), D ** -0.5),
        "rwkv_mu": uni((ne, 2, RWKV_PROJ), 0.05, 0.45),
        "rwkv_w0": (-5.5 + 5.0 * ramp ** 0.85) + nrm((ne, 2, RWKV_WIDTH), 0.1),
        "rwkv_w2": nrm((ne, 2, RWKV_DECAY_LORA, RWKV_WIDTH), 0.5 * RWKV_DECAY_LORA ** -0.5),
        "rwkv_a0": nrm((ne, 2, RWKV_WIDTH), 0.1),
        "rwkv_a2": nrm((ne, 2, RWKV_ICLR_LORA, RWKV_WIDTH), 0.5 * RWKV_ICLR_LORA ** -0.5),
        "rwkv_g2": nrm((ne, RWKV_GATE_LORA, RWKV_WIDTH), RWKV_GATE_LORA ** -0.5),
        "rwkv_k_k": 0.85 + nrm((ne, RWKV_WIDTH), 0.05),
        "rwkv_k_a": 1.0 + nrm((ne, RWKV_WIDTH), 0.05),
        "rwkv_r_k": nrm((ne, RWKV_HEADS, HEAD_DIM), 0.1),
        "rwkv_lnx_g": 1.0 + nrm((ne, RWKV_WIDTH), 0.05),
        "rwkv_lnx_b": nrm((ne, RWKV_WIDTH), 0.02),
        "s5_a_re": -0.5 + nrm((ne, 2, S5_GROUPS, S5_STATE), 0.01),
        "s5_a_im": math.pi * jnp.arange(S5_STATE, dtype=f32) + nrm((ne, 2, S5_GROUPS, S5_STATE), 0.01),
        "s5_log_step": uni((ne, 2, S5_GROUPS), math.log(1e-3), math.log(1e-1)),
        "s5_b_re": nrm((ne, 2, S5_GROUPS, S5_STATE, S5_GROUP_CH), (2 * S5_GROUP_CH) ** -0.5),
        "s5_b_im": nrm((ne, 2, S5_GROUPS, S5_STATE, S5_GROUP_CH), (2 * S5_GROUP_CH) ** -0.5),
        "s5_c_re": nrm((ne, 2, S5_GROUPS, S5_GROUP_CH, S5_STATE), S5_STATE ** -0.5),
        "s5_c_im": nrm((ne, 2, S5_GROUPS, S5_GROUP_CH, S5_STATE), S5_STATE ** -0.5),
        "s5_d": nrm((ne, S5_WIDTH), 0.5),
        "s5_glu_w": nrm((ne, S5_WIDTH, S5_WIDTH), S5_WIDTH ** -0.5),
        "s5_glu_b": nrm((ne, S5_WIDTH), 0.02),
        "attn_in_w": nrm((no, D, C_PROJ), D ** -0.5),
        "attn_out_w": nrm((no, D, D), D ** -0.5),
        "attn_sink": nrm((no, ATTN_HEADS), 0.5),
        "final_g": 1.0 + nrm((D,), 0.05),
    }


def reference(x, c, ctx, c_ctx, norm_g, mod_w, mod_b, ffn_w1, ffn_w2, ab_in_w, ab_out_w, rwkv_mu, rwkv_w0,
              rwkv_w2, rwkv_a0, rwkv_a2, rwkv_g2, rwkv_k_k, rwkv_k_a, rwkv_r_k, rwkv_lnx_g, rwkv_lnx_b,
              s5_a_re, s5_a_im, s5_log_step, s5_b_re, s5_b_im, s5_c_re, s5_c_im, s5_d, s5_glu_w, s5_glu_b,
              attn_in_w, attn_out_w, attn_sink, final_g):
    f32 = jnp.float32
    n_lat = x.shape[1]
    rows = n_lat // GRID_W
    row_id = jnp.repeat(jnp.arange(rows, dtype=f32), GRID_W)
    col_id = jnp.tile(jnp.arange(GRID_W, dtype=f32), rows)
    inv_freq = ROPE_BASE ** (-jnp.arange(0, ROPE_AXIS_DIM, 2, dtype=f32) / ROPE_AXIS_DIM)
    ang = jnp.concatenate([row_id[:, None] * inv_freq, col_id[:, None] * inv_freq], axis=-1)
    cos, sin = jnp.cos(ang), jnp.sin(ang)

    xc = ctx
    for l in range(DEPTH):
        last = l == DEPTH - 1
        ml = [m[:, None, :] for m in jnp.split(jax.nn.silu(c) @ mod_w[l] + mod_b[l], N_MOD, axis=-1)]
        mc = jnp.split(jax.nn.silu(c_ctx) @ mod_w[l] + mod_b[l], N_MOD, axis=-1)
        x = x + 0.5 * ml[2] * _swiglu(_modnorm(x, norm_g[l, 0], ml[0], ml[1]), ffn_w1[l, 0], ffn_w2[l, 0])
        xc = xc + 0.5 * mc[2] * _swiglu(_modnorm(xc, norm_g[l, 0], mc[0], mc[1]), ffn_w1[l, 0], ffn_w2[l, 0])
        hl = _modnorm(x, norm_g[l, 1], ml[3], ml[4])
        hc = _modnorm(xc, norm_g[l, 1], mc[3], mc[4])
        if l % 2 == 0:
            e = l // 2
            yl, yc = _mixer_ab(hl, hc, ab_in_w[e], ab_out_w[e], rwkv_mu[e], rwkv_w0[e], rwkv_w2[e], rwkv_a0[e],
                               rwkv_a2[e], rwkv_g2[e], rwkv_k_k[e], rwkv_k_a[e], rwkv_r_k[e], rwkv_lnx_g[e],
                               rwkv_lnx_b[e], s5_a_re[e], s5_a_im[e], s5_log_step[e], s5_b_re[e], s5_b_im[e],
                               s5_c_re[e], s5_c_im[e], s5_d[e], s5_glu_w[e], s5_glu_b[e])
        else:
            o = l // 2
            yl, yc = _mixer_c(hl, hc, attn_in_w[o], attn_out_w[o], attn_sink[o], cos, sin, not last)
        x = x + ml[5] * yl
        x = x + 0.5 * ml[8] * _swiglu(_modnorm(x, norm_g[l, 2], ml[6], ml[7]), ffn_w1[l, 1], ffn_w2[l, 1])
        if not last:
            xc = xc + mc[5] * yc
            xc = xc + 0.5 * mc[8] * _swiglu(_modnorm(xc, norm_g[l, 2], mc[6], mc[7]), ffn_w1[l, 1], ffn_w2[l, 1])
    return _rmsnorm(x, final_g)
```

```python
import functools
import math

import jax
import jax.numpy as jnp
from jax import lax
from jax.experimental import pallas as pl
from jax.experimental.pallas import tpu as pltpu

F32 = jnp.float32
BF16 = jnp.bfloat16

D_MODEL = 1024
D_FF = 2816
N_MOD = 9
HEAD_DIM = 64
RMS_EPS = 1e-6
GN_EPS = 64e-5
RWKV_WIDTH = 512
RWKV_PROJ = 1792
S5_WIDTH = 512
S5_GROUP_CH = 16
S5_GROUPS = 32
S5_STATE = 64
ATTN_HEADS = 16
ATTN_KV_HEADS = 4
ATTN_BLOCK = 128
ATTN_WINDOW = 128
ATTN_Q_W = 1024
ATTN_KV_W = 256
ROPE_BASE = 10000.0
NEG_INF = -1e30

ROW_TILE = 256
FFN_ROW_TILE = 512
FFN_F_TILE = 1408
RWKV_CHUNK = 64
S5_STEPS = 64
VMEM_LIMIT = 56 * 1024 * 1024

_NN = (((1,), (0,)), ((), ()))
_NT = (((1,), (1,)), ((), ()))


def _dot(a, b, dims=_NN):
    return lax.dot_general(a, b, dims, preferred_element_type=F32)


def _split2(a):
    hi = a.astype(BF16)
    lo = (a - hi.astype(F32)).astype(BF16)
    return hi, lo


def _split3(a):
    hi = a.astype(BF16)
    r1 = a - hi.astype(F32)
    mid = r1.astype(BF16)
    lo = (r1 - mid.astype(F32)).astype(BF16)
    return hi, mid, lo


def _mm1(a, b, dims=_NN):
    return _dot(a.astype(BF16), b.astype(BF16), dims)


def _mm3(a, b, dims=_NN):
    ah, al = _split2(a)
    bh, bl = _split2(b)
    return _dot(ah, bh, dims) + (_dot(ah, bl, dims) + _dot(al, bh, dims))


def _mm_const_rhs(a, c_bf16, dims=_NN):
    hi, mid, lo = _split3(a)
    return _dot(hi, c_bf16, dims) + (_dot(mid, c_bf16, dims) + _dot(lo, c_bf16, dims))


def _mm_const_lhs(c_bf16, a, dims=_NN):
    hi, mid, lo = _split3(a)
    return _dot(c_bf16, hi, dims) + (_dot(c_bf16, mid, dims) + _dot(c_bf16, lo, dims))


def _sigmoid(x):
    return 1.0 / (1.0 + jnp.exp(-x))


def _silu(x):
    return x * _sigmoid(x)


def _params(*sem):
    return pltpu.CompilerParams(dimension_semantics=sem, vmem_limit_bytes=VMEM_LIMIT)


def _modnorm(x, g, shift, scale):
    xn = x * lax.rsqrt(jnp.mean(x * x, axis=-1, keepdims=True) + RMS_EPS)
    return (xn * g) * (1.0 + scale) + shift


def _head_ones(width):
    r = lax.broadcasted_iota(jnp.int32, (width, width), 0) >> 6
    c = lax.broadcasted_iota(jnp.int32, (width, width), 1) >> 6
    return (r == c).astype(BF16)


def _mod_kernel(c_ref, w_ref, b_ref, o_ref):
    o_ref[...] = _mm3(c_ref[...], w_ref[...]) + b_ref[...]


def _mod_vectors(cs, mod_w, mod_b):
    depth = mod_w.shape[0]
    n = mod_w.shape[2]
    tn = 1152
    return pl.pallas_call(
        _mod_kernel,
        grid=(depth, n // tn),
        in_specs=[pl.BlockSpec((8, D_MODEL), lambda l, j: (0, 0)),
                  pl.BlockSpec((None, D_MODEL, tn), lambda l, j: (l, 0, j)),
                  pl.BlockSpec((None, 1, tn), lambda l, j: (l, 0, j))],
        out_specs=pl.BlockSpec((None, 8, tn), lambda l, j: (l, 0, j)),
        out_shape=jax.ShapeDtypeStruct((depth, 8, n), F32),
        compiler_params=_params("parallel", "parallel"),
        name="mod_vectors",
    )(cs, mod_w, mod_b.reshape(depth, 1, n))


def _ffn_kernel(x_ref, m_ref, g_ref, w1g_ref, w1u_ref, w2_ref, fg_ref, o_ref, h_ref, acc_ref, *, mod0, final):
    j = pl.program_id(1)

    @pl.when(j == 0)
    def _():
        h = _modnorm(x_ref[...], g_ref[...], m_ref[mod0:mod0 + 1, :], m_ref[mod0 + 1:mod0 + 2, :])
        h_ref[...] = h.astype(BF16)

    h = h_ref[...]
    gate = _dot(h, w1g_ref[...])
    up = _dot(h, w1u_ref[...])
    act = (_silu(gate) * up).astype(BF16)
    part = _dot(act, w2_ref[...])

    @pl.when(j == 0)
    def _():
        acc_ref[...] = part

    @pl.when(j > 0)
    def _():
        acc_ref[...] += part

    @pl.when(j == pl.num_programs(1) - 1)
    def _():
        y = x_ref[...] + (0.5 * m_ref[mod0 + 2:mod0 + 3, :]) * acc_ref[...]
        if final:
            y = y * lax.rsqrt(jnp.mean(y * y, axis=-1, keepdims=True) + RMS_EPS) * fg_ref[...]
        o_ref[...] = y


def _ffn(x, modrows, g, w1, w2, final_g, *, mod0, row_tile_of_mod, final=False):
    rows = x.shape[0]
    tm, tf = FFN_ROW_TILE, FFN_F_TILE
    nf = D_FF // tf
    step = tm // ROW_TILE
    return pl.pallas_call(
        functools.partial(_ffn_kernel, mod0=mod0, final=final),
        grid=(rows // tm, nf),
        in_specs=[pl.BlockSpec((tm, D_MODEL), lambda i, j: (i, 0)),
                  pl.BlockSpec((None, N_MOD, D_MODEL), lambda i, j: (row_tile_of_mod + i * step, 0, 0)),
                  pl.BlockSpec((1, D_MODEL), lambda i, j: (0, 0)),
                  pl.BlockSpec((D_MODEL, tf), lambda i, j: (0, j)),
                  pl.BlockSpec((D_MODEL, tf), lambda i, j: (0, nf + j)),
                  pl.BlockSpec((tf, D_MODEL), lambda i, j: (j, 0)),
                  pl.BlockSpec((1, D_MODEL), lambda i, j: (0, 0))],
        out_specs=pl.BlockSpec((tm, D_MODEL), lambda i, j: (i, 0)),
        out_shape=jax.ShapeDtypeStruct((rows, D_MODEL), F32),
        scratch_shapes=[pltpu.VMEM((tm, D_MODEL), BF16), pltpu.VMEM((tm, D_MODEL), F32)],
        compiler_params=_params("parallel", "arbitrary"),
        name="ffn",
    )(x, modrows, g.reshape(1, D_MODEL), w1, w1, w2, final_g.reshape(1, D_MODEL))


def _x_tile(bsz, n_lat_tiles):
    return lambda b, k: (jnp.where(k == 0, b, bsz + b * n_lat_tiles + k - 1), 0)


def _ab_in_kernel(x_ref, m_ref, g_ref, w_ref, pa_ref, pb_ref):
    h = _modnorm(x_ref[...], g_ref[...], m_ref[3:4, :], m_ref[4:5, :]).astype(BF16)
    p = _dot(h, w_ref[...])
    pa_ref[...] = p[:, :RWKV_PROJ]
    pb_ref[...] = p[:, RWKV_PROJ:]


def _ab_in_proj(x, modrows, g, w, bsz, n_lat_tiles):
    nt = n_lat_tiles + 1
    seq = nt * ROW_TILE
    xt = _x_tile(bsz, n_lat_tiles)
    return pl.pallas_call(
        _ab_in_kernel,
        grid=(bsz, nt),
        in_specs=[pl.BlockSpec((ROW_TILE, D_MODEL), xt),
                  pl.BlockSpec((None, N_MOD, D_MODEL), lambda b, k: (xt(b, k)[0], 0, 0)),
                  pl.BlockSpec((1, D_MODEL), lambda b, k: (0, 0)),
                  pl.BlockSpec(w.shape, lambda b, k: (0, 0))],
        out_specs=[pl.BlockSpec((None, ROW_TILE, RWKV_PROJ), lambda b, k: (b, k, 0)),
                   pl.BlockSpec((None, ROW_TILE, S5_WIDTH), lambda b, k: (b, k, 0))],
        out_shape=[jax.ShapeDtypeStruct((bsz, seq, RWKV_PROJ), F32),
                   jax.ShapeDtypeStruct((bsz, seq, S5_WIDTH), F32)],
        compiler_params=_params("parallel", "parallel"),
        name="ab_in_proj",
    )(x, modrows, g.reshape(1, D_MODEL), w)


def _rwkv_prep_kernel(cur_ref, prev_ref, next_ref, mu_ref, wl_ref, vec_ref,
                      r_ref, v_ref, a_ref, k_ref, b_ref, lw_ref, g_ref, bonus_ref):
    k_idx = pl.program_id(1)
    nt = pl.num_programs(1)
    cur = cur_ref[...]
    rows = cur.shape[0]
    row = lax.broadcasted_iota(jnp.int32, cur.shape, 0)
    use_prev = k_idx >= 2
    use_next = jnp.logical_and(k_idx >= 1, k_idx < nt - 1)
    prev_row = jnp.where(use_prev, prev_ref[7:8, :], 0.0)
    next_row = jnp.where(use_next, next_ref[0:1, :], 0.0)
    prev = jnp.where(row == 0, prev_row, pltpu.roll(cur, 1, axis=0))
    nxt = jnp.where(row == rows - 1, next_row, pltpu.roll(cur, rows - 1, axis=0))
    p = cur + mu_ref[0:1, :] * (prev - cur) + mu_ref[1:2, :] * (nxt - cur)

    W = RWKV_WIDTH
    r, k, v = p[:, :W], p[:, W:2 * W], p[:, 2 * W:3 * W]
    lora_in = p[:, 3 * W:]
    lane = lax.broadcasted_iota(jnp.int32, lora_in.shape, 1)
    lora_act = jnp.where(lane < 64, jnp.tanh(lora_in), jnp.where(lane < 128, lora_in, _sigmoid(lora_in)))
    lo = _mm3(lora_act, wl_ref[...])

    k_k, k_a, r_k = vec_ref[0:1, :], vec_ref[1:2, :], vec_ref[2:3, :]
    ones = _head_ones(W)
    kk = k * k_k
    kk = kk * lax.rsqrt(jnp.maximum(_mm_const_rhs(kk * kk, ones), 1e-12))
    r_ref[...] = r
    v_ref[...] = v
    a_ref[...] = -kk
    k_sum = jnp.zeros_like(k)
    for d in range(2):
        z = -(vec_ref[3 + d:4 + d, :] + lo[:, d * W:(d + 1) * W])
        softplus = jnp.maximum(z, 0.0) + jnp.log(1.0 + jnp.exp(-jnp.abs(z)))
        lw_ref[d] = -jnp.exp(-softplus - 0.5)
        a = _sigmoid(vec_ref[5 + d:6 + d, :] + lo[:, (2 + d) * W:(3 + d) * W])
        kd = k * (1.0 + (a - 1.0) * k_a)
        k_ref[d] = kd
        b_ref[d] = kk * a
        k_sum = k_sum + kd
    g_ref[...] = lo[:, 4 * W:]
    bonus_ref[...] = _mm_const_rhs(r * (0.5 * k_sum) * r_k, ones) * v


def _rwkv_prep(pa, mu, w_lora, vecs):
    bsz, seq, _ = pa.shape
    nt = seq // ROW_TILE
    sub = ROW_TILE // 8
    last_blk = seq // 8 - 1
    W = RWKV_WIDTH
    one = jax.ShapeDtypeStruct((bsz, seq, W), F32)
    two = jax.ShapeDtypeStruct((2, bsz, seq, W), F32)
    spec1 = pl.BlockSpec((None, ROW_TILE, W), lambda b, k: (b, k, 0))
    spec2 = pl.BlockSpec((2, None, ROW_TILE, W), lambda b, k: (0, b, k, 0))
    return pl.pallas_call(
        _rwkv_prep_kernel,
        grid=(bsz, nt),
        in_specs=[pl.BlockSpec((None, ROW_TILE, RWKV_PROJ), lambda b, k: (b, k, 0)),
                  pl.BlockSpec((None, 8, RWKV_PROJ), lambda b, k: (b, jnp.maximum(k * sub - 1, 0), 0)),
                  pl.BlockSpec((None, 8, RWKV_PROJ), lambda b, k: (b, jnp.minimum((k + 1) * sub, last_blk), 0)),
                  pl.BlockSpec(mu.shape, lambda b, k: (0, 0)),
                  pl.BlockSpec(w_lora.shape, lambda b, k: (0, 0)),
                  pl.BlockSpec(vecs.shape, lambda b, k: (0, 0))],
        out_specs=[spec1, spec1, spec1, spec2, spec2, spec2, spec1, spec1],
        out_shape=[one, one, one, two, two, two, one, one],
        compiler_params=_params("parallel", "parallel"),
        name="rwkv_prep",
    )(pa, pa, pa, mu, w_lora, vecs)


def _rwkv_scan_kernel(r_ref, v_ref, a_ref, k_ref, b_ref, lw_ref, y_ref, h_ref, *, bsz):
    C = RWKV_CHUNK
    d = pl.program_id(0) // bsz
    fwd = d == 0

    @pl.when(pl.program_id(1) == 0)
    def _():
        h_ref[...] = jnp.zeros_like(h_ref)

    rr = lax.broadcasted_iota(jnp.int32, (C, C), 0)
    cc = lax.broadcasted_iota(jnp.int32, (C, C), 1)
    tri = (jnp.where(fwd, rr - cc, cc - rr) >= 0).astype(BF16)
    lw = lw_ref[...]
    cum = _mm_const_lhs(tri, lw)
    e = jnp.exp(cum)
    einv = jnp.exp(-cum)
    at = a_ref[...] * jnp.exp(cum - lw)
    rt = r_ref[...] * e
    bt = b_ref[...] * einv
    kt = k_ref[...] * einv
    v = v_ref[...]

    r2 = lax.broadcasted_iota(jnp.int32, (2 * C, 2 * C), 0)
    c2 = lax.broadcasted_iota(jnp.int32, (2 * C, 2 * C), 1)
    same = (r2 >> 6) == (c2 >> 6)
    order = jnp.where(fwd, r2 - c2, c2 - r2)
    strict = jnp.logical_and(same, order > 0)
    incl = jnp.logical_and(same, order >= 0)
    top = r2 < C
    eye = (r2 == c2).astype(F32)
    lane0 = lax.broadcasted_iota(jnp.int32, (C, 2 * C), 1) < C
    ones_c = jnp.ones((C, 2 * C), BF16)

    def stack(x):
        return jnp.concatenate([jnp.where(lane0, x, 0.0), jnp.where(lane0, 0.0, x)], axis=0)

    for p in range(RWKV_WIDTH // 128):
        sl = slice(128 * p, 128 * (p + 1))
        a_s, r_s, b_s, k_s, v_s = stack(at[:, sl]), stack(rt[:, sl]), stack(bt[:, sl]), stack(kt[:, sl]), stack(v[:, sl])
        x4 = jnp.concatenate([a_s, r_s], axis=0)
        s_all = _mm3(x4, jnp.concatenate([bt[:, sl], kt[:, sl]], axis=0), _NT)
        s_top, s_bot = s_all[:2 * C], s_all[2 * C:]
        s_top_sw, s_bot_sw = pltpu.roll(s_top, C, axis=1), pltpu.roll(s_bot, C, axis=1)
        n_ab = jnp.where(strict, jnp.where(top, s_top, s_top_sw), 0.0)
        n_ak = jnp.where(strict, jnp.where(top, s_top_sw, s_top), 0.0)
        n_rb = jnp.where(incl, jnp.where(top, s_bot, s_bot_sw), 0.0)
        n_rk = jnp.where(incl, jnp.where(top, s_bot_sw, s_bot), 0.0)
        tinv = eye + n_ab
        pw = n_ab
        for _ in range(int(math.log2(C)) - 1):
            pw = _mm3(pw, pw)
            tinv = tinv + _mm3(tinv, pw)
        h = h_ref[p]
        xh = _mm3(x4, h)
        kv = _mm3(jnp.concatenate([n_ak, n_rk], axis=0), v_s)
        u = _mm3(tinv, xh[:2 * C] + kv[:2 * C])
        y_s = xh[2 * C:] + _mm3(n_rb, u) + kv[2 * C:]
        y_ref[:, sl] = y_s[:C] + y_s[C:]
        lhs_t = jnp.concatenate([b_s, k_s], axis=0).T
        dh = _mm3(lhs_t, jnp.concatenate([u, v_s], axis=0))
        tot = _mm_const_rhs(lw[:, sl].T, ones_c)
        h_ref[p] = jnp.exp(tot) * (h + dh)


def _rwkv_scan(r, v, a, k, b, lw, n_ctx_chunks):
    bsz, seq, W = r.shape
    C = RWKV_CHUNK
    nch = seq // C

    def chunk(g, j):
        back = jnp.where(j < n_ctx_chunks, n_ctx_chunks - 1 - j, nch - 1 - (j - n_ctx_chunks))
        return jnp.where(g < bsz, j, back)

    one = pl.BlockSpec((None, C, W), lambda g, j: (g % bsz, chunk(g, j), 0))
    two = pl.BlockSpec((None, None, C, W), lambda g, j: (g // bsz, g % bsz, chunk(g, j), 0))
    return pl.pallas_call(
        functools.partial(_rwkv_scan_kernel, bsz=bsz),
        grid=(2 * bsz, nch),
        in_specs=[one, one, one, two, two, two],
        out_specs=two,
        out_shape=jax.ShapeDtypeStruct((2, bsz, seq, W), F32),
        scratch_shapes=[pltpu.VMEM((W // 128, 128, 128), F32)],
        compiler_params=_params("parallel", "arbitrary"),
        name="rwkv_scan",
    )(r, v, a, k, b, lw)


def _s5_scan_kernel(u_ref, lam_ref, bmat_ref, cmat_ref, y_ref, bu_ref, st_ref):
    S = S5_STEPS
    half = 8 * S5_STATE

    @pl.when(pl.program_id(0) == 0)
    def _():
        st_ref[...] = jnp.zeros_like(st_ref)

    u = u_ref[...].reshape(S * 8, S5_WIDTH)
    is_fwd = (lax.broadcasted_iota(jnp.int32, (S * 8, 128), 0) & 7) < 4
    for q in range(S5_WIDTH // 128):
        uq = u[:, 128 * q:128 * (q + 1)]
        lhs = jnp.concatenate([jnp.where(is_fwd, uq, 0.0), jnp.where(is_fwd, 0.0, uq)], axis=1)
        bu_ref[:, 2 * half * q:2 * half * (q + 1)] = _mm3(lhs, bmat_ref[q])

    for q in range(S5_WIDTH // 128):
        base = 2 * half * q
        ar = lam_ref[:, base:base + half]
        ai = lam_ref[:, base + half:base + 2 * half]

        def step(t, carry):
            xr, xi = carry
            rows = pl.ds(pl.multiple_of(t * 8, 8), 8)
            nr = ar * xr - ai * xi + bu_ref[rows, base:base + half]
            ni = ar * xi + ai * xr + bu_ref[rows, base + half:base + 2 * half]
            bu_ref[rows, base:base + half] = nr
            bu_ref[rows, base + half:base + 2 * half] = ni
            return nr, ni

        xr, xi = lax.fori_loop(0, S, step, (st_ref[:, base:base + half], st_ref[:, base + half:base + 2 * half]),
                               unroll=4)
        st_ref[:, base:base + half] = xr
        st_ref[:, base + half:base + 2 * half] = xi

    for q in range(S5_WIDTH // 128):
        xq = bu_ref[:, 2 * half * q:2 * half * (q + 1)]
        yf = _mm3(xq, cmat_ref[0, q])
        yb = _mm3(xq, cmat_ref[1, q])
        y_ref[:, :, 128 * q:128 * (q + 1)] = jnp.where(is_fwd, yf, yb).reshape(S, 8, 128)


def _s5_scan(u8, lam, bmat, cmat):
    seq = u8.shape[0]
    S = S5_STEPS
    nstate = 2 * S5_GROUPS * S5_STATE
    return pl.pallas_call(
        _s5_scan_kernel,
        grid=(seq // S,),
        in_specs=[pl.BlockSpec((S, 8, S5_WIDTH), lambda i: (i, 0, 0)),
                  pl.BlockSpec(lam.shape, lambda i: (0, 0)),
                  pl.BlockSpec(bmat.shape, lambda i: (0, 0, 0)),
                  pl.BlockSpec(cmat.shape, lambda i: (0, 0, 0, 0))],
        out_specs=pl.BlockSpec((S, 8, S5_WIDTH), lambda i: (i, 0, 0)),
        out_shape=jax.ShapeDtypeStruct(u8.shape, F32),
        scratch_shapes=[pltpu.VMEM((S * 8, nstate), F32), pltpu.VMEM((8, nstate), F32)],
        compiler_params=_params("arbitrary"),
        name="s5_scan",
    )(u8, lam, bmat, cmat)


def _s5_discretise(a_re, a_im, log_step, b_re, b_im, c_re, c_im):
    lam_re = jnp.minimum(a_re, -1e-4)
    lam_im = a_im
    dt = jnp.exp(log_step)[..., None]
    mag = jnp.exp(lam_re * dt)
    ab_re, ab_im = mag * jnp.cos(lam_im * dt), mag * jnp.sin(lam_im * dt)
    den = lam_re * lam_re + lam_im * lam_im
    f_re = ((ab_re - 1.0) * lam_re + ab_im * lam_im) / den
    f_im = (ab_im * lam_re - (ab_re - 1.0) * lam_im) / den
    bb_re = f_re[..., None] * b_re - f_im[..., None] * b_im
    bb_im = f_re[..., None] * b_im + f_im[..., None] * b_re
    nq = S5_WIDTH // 128
    eye8 = jnp.eye(8, dtype=F32)

    def lanes(t):
        return t.reshape(2, nq, 8 * S5_STATE)

    lam = jnp.concatenate([lanes(ab_re), lanes(ab_im)], axis=-1).reshape(2, nq * 16 * S5_STATE)
    lam = jnp.repeat(lam, 4, axis=0)

    def in_block(t):
        t = t.reshape(2, nq, 8, S5_STATE, S5_GROUP_CH)
        return jnp.einsum('dqgpi,gh->dqgihp', t, eye8).reshape(2, nq, 128, 8 * S5_STATE)

    bmat = jnp.concatenate([in_block(bb_re), in_block(bb_im)], axis=-1)
    bmat = jnp.concatenate([bmat[0], bmat[1]], axis=1)

    def out_block(t):
        t = t.reshape(2, nq, 8, S5_GROUP_CH, S5_STATE)
        return jnp.einsum('dqgip,gh->dqgphi', t, eye8).reshape(2, nq, 8 * S5_STATE, 128)

    cmat = jnp.concatenate([out_block(c_re), -out_block(c_im)], axis=2)
    return lam, bmat, cmat


def _ab_out_kernel(x_ref, m_ref, y_ref, g_ref, bonus_ref, ys_ref, u_ref, vec_ref, gluw_ref, w_ref, o_ref):
    W = RWKV_WIDTH
    ones = _head_ones(W)
    y = y_ref[0] + y_ref[1]
    mu = _mm_const_rhs(y, ones) * (1.0 / HEAD_DIM)
    yc = y - mu
    var = _mm_const_rhs(yc * yc, ones) * (1.0 / HEAD_DIM)
    ya = (yc * lax.rsqrt(var + GN_EPS) * vec_ref[0:1, :] + vec_ref[1:2, :] + bonus_ref[...]) * g_ref[...]

    s = ys_ref[...] + vec_ref[2:3, :] * u_ref[...]
    z = 0.5 * s * (1.0 + jnp.tanh(math.sqrt(2.0 / math.pi) * (s + 0.044715 * (s * s * s))))
    yb = z * _sigmoid(_mm1(z, gluw_ref[...]) + vec_ref[3:4, :])
    out = _dot(ya.astype(BF16), w_ref[:W, :]) + _dot(yb.astype(BF16), w_ref[W:, :])
    o_ref[...] = x_ref[...] + m_ref[5:6, :] * out


def _ab_out(x, modrows, y, g, bonus, ys, u, vecs, glu_w, out_w, bsz, n_lat_tiles):
    nt = n_lat_tiles + 1
    W = RWKV_WIDTH
    xt = _x_tile(bsz, n_lat_tiles)
    seq_spec = pl.BlockSpec((None, ROW_TILE, W), lambda b, k: (b, k, 0))
    return pl.pallas_call(
        _ab_out_kernel,
        grid=(bsz, nt),
        in_specs=[pl.BlockSpec((ROW_TILE, D_MODEL), xt),
                  pl.BlockSpec((None, N_MOD, D_MODEL), lambda b, k: (xt(b, k)[0], 0, 0)),
                  pl.BlockSpec((2, None, ROW_TILE, W), lambda b, k: (0, b, k, 0)),
                  seq_spec, seq_spec, seq_spec, seq_spec,
                  pl.BlockSpec(vecs.shape, lambda b, k: (0, 0)),
                  pl.BlockSpec(glu_w.shape, lambda b, k: (0, 0)),
                  pl.BlockSpec(out_w.shape, lambda b, k: (0, 0))],
        out_specs=pl.BlockSpec((ROW_TILE, D_MODEL), xt),
        out_shape=jax.ShapeDtypeStruct(x.shape, F32),
        compiler_params=_params("parallel", "parallel"),
        name="ab_out",
    )(x, modrows, y, g, bonus, ys, u, vecs, glu_w, out_w)


def _attn_in_kernel(x_ref, m_ref, g_ref, w_ref, cos_ref, sin_ref, o_ref):
    h = _modnorm(x_ref[...], g_ref[...], m_ref[3:4, :], m_ref[4:5, :]).astype(BF16)
    p = _dot(h, w_ref[...])
    qk_w = ATTN_Q_W + ATTN_KV_W
    qk = p[:, :qk_w]
    half = HEAD_DIM // 2
    lane = lax.broadcasted_iota(jnp.int32, qk.shape, 1)
    first = (lane & (HEAD_DIM - 1)) < half
    partner = jnp.where(first, pltpu.roll(qk, qk_w - half, axis=1), pltpu.roll(qk, half, axis=1))
    reps = qk_w // 128
    cos = jnp.concatenate([cos_ref[...]] * reps, axis=1)
    sin = jnp.concatenate([sin_ref[...]] * reps, axis=1)
    o_ref[:, :qk_w] = qk * cos + partner * sin
    o_ref[:, qk_w:] = p[:, qk_w:]


def _attn_in_proj(x, modrows, g, w, cos_t, sin_t, n_ctx_tiles, n_lat_tiles):
    rows = x.shape[0]
    width = w.shape[1]

    def rope_tile(i):
        return (jnp.where(i < n_ctx_tiles, 0, 1 + (i - n_ctx_tiles) % n_lat_tiles), 0)

    return pl.pallas_call(
        _attn_in_kernel,
        grid=(rows // ROW_TILE,),
        in_specs=[pl.BlockSpec((ROW_TILE, D_MODEL), lambda i: (i, 0)),
                  pl.BlockSpec((None, N_MOD, D_MODEL), lambda i: (i, 0, 0)),
                  pl.BlockSpec((1, D_MODEL), lambda i: (0, 0)),
                  pl.BlockSpec(w.shape, lambda i: (0, 0)),
                  pl.BlockSpec((ROW_TILE, 128), rope_tile),
                  pl.BlockSpec((ROW_TILE, 128), rope_tile)],
        out_specs=pl.BlockSpec((ROW_TILE, width), lambda i: (i, 0)),
        out_shape=jax.ShapeDtypeStruct((rows, width), F32),
        compiler_params=_params("parallel"),
        name="attn_in_proj",
    )(x, modrows, g.reshape(1, D_MODEL), w, cos_t, sin_t)


def _rope_tables(n_lat, grid_w):
    half = HEAD_DIM // 2
    t = jnp.arange(n_lat)
    row_id = (t // grid_w).astype(F32)
    col_id = (t % grid_w).astype(F32)
    inv_freq = ROPE_BASE ** (-jnp.arange(0, half, 2, dtype=F32) / half)
    ang = jnp.concatenate([row_id[:, None] * inv_freq, col_id[:, None] * inv_freq], axis=-1)
    cos, sin = jnp.cos(ang), jnp.sin(ang)
    cos_t = jnp.concatenate([cos, cos, cos, cos], axis=-1)
    sin_t = jnp.concatenate([-sin, sin, -sin, sin], axis=-1)
    cos_t = jnp.concatenate([jnp.ones((ROW_TILE, 128), F32), cos_t], axis=0)
    sin_t = jnp.concatenate([jnp.zeros((ROW_TILE, 128), F32), sin_t], axis=0)
    return cos_t, sin_t


def _attn_kernel(sink_ref, q_ref, kp_ref, kc_ref, kn_ref, vp_ref, vc_ref, vn_ref, kx_ref, vx_ref, o_ref, *, nblk):
    i = pl.program_id(1)
    Q = ATTN_BLOCK
    G = ATTN_HEADS // ATTN_KV_HEADS
    k_all = jnp.concatenate([kp_ref[...], kc_ref[...], kn_ref[...], kx_ref[...]], axis=0).astype(BF16)
    v_all = jnp.concatenate([vp_ref[...], vc_ref[...], vn_ref[...], vx_ref[...]], axis=0).astype(BF16)
    nk = k_all.shape[0]
    qi = lax.broadcasted_iota(jnp.int32, (G * Q, nk), 0) & (Q - 1)
    cj = lax.broadcasted_iota(jnp.int32, (G * Q, nk), 1)
    mj = cj - Q
    blk = jnp.where(cj < Q, i - 1, jnp.where(cj < 2 * Q, i, i + 1))
    valid = jnp.logical_and(jnp.abs(mj - qi) <= ATTN_WINDOW, jnp.logical_and(blk >= 0, blk < nblk))
    valid = jnp.logical_or(valid, cj >= 3 * Q)
    q = q_ref[...] * (HEAD_DIM ** -0.5)
    for kh in range(ATTN_KV_HEADS):
        q4 = jnp.concatenate([q[:, (kh * G + g) * HEAD_DIM:(kh * G + g + 1) * HEAD_DIM] for g in range(G)], axis=0)
        kk = k_all[:, kh * HEAD_DIM:(kh + 1) * HEAD_DIM]
        vv = v_all[:, kh * HEAD_DIM:(kh + 1) * HEAD_DIM]
        s = _dot(q4.astype(BF16), kk, _NT)
        s = jnp.where(valid, s, NEG_INF)
        row_head = lax.broadcasted_iota(jnp.int32, (G * Q, 1), 0) >> 7
        sink = jnp.zeros((G * Q, 1), F32)
        for g in range(G):
            sink = jnp.where(row_head == g, sink_ref[kh * G + g], sink)
        m = jnp.maximum(jnp.max(s, axis=-1, keepdims=True), sink)
        pr = jnp.exp(s - m)
        den = jnp.sum(pr, axis=-1, keepdims=True) + jnp.exp(sink - m)
        o = _dot(pr.astype(BF16), vv) / den
        for g in range(G):
            hq = kh * G + g
            o_ref[:, hq * HEAD_DIM:(hq + 1) * HEAD_DIM] = o[g * Q:(g + 1) * Q]


def _attention(qkv, sink, bsz, n_lat, n_ctx):
    Q = ATTN_BLOCK
    nblk = n_lat // Q
    lat0 = bsz * n_ctx // Q
    kcol, vcol = ATTN_Q_W // ATTN_KV_W, ATTN_Q_W // ATTN_KV_W + 1

    def kv_spec(col, off):
        return pl.BlockSpec((Q, ATTN_KV_W),
                            lambda b, i: (lat0 + b * nblk + jnp.clip(i + off, 0, nblk - 1), col))

    def ctx_spec(col):
        return pl.BlockSpec((n_ctx, ATTN_KV_W), lambda b, i: (b, col))

    return pl.pallas_call(
        functools.partial(_attn_kernel, nblk=nblk),
        grid=(bsz, nblk),
        in_specs=[pl.BlockSpec(memory_space=pltpu.SMEM),
                  pl.BlockSpec((Q, ATTN_Q_W), lambda b, i: (lat0 + b * nblk + i, 0)),
                  kv_spec(kcol, -1), kv_spec(kcol, 0), kv_spec(kcol, 1),
                  kv_spec(vcol, -1), kv_spec(vcol, 0), kv_spec(vcol, 1),
                  ctx_spec(kcol), ctx_spec(vcol)],
        out_specs=pl.BlockSpec((Q, ATTN_Q_W), lambda b, i: (b * nblk + i, 0)),
        out_shape=jax.ShapeDtypeStruct((bsz * n_lat, ATTN_Q_W), F32),
        compiler_params=_params("parallel", "parallel"),
        name="window_attention",
    )(sink, qkv, qkv, qkv, qkv, qkv, qkv, qkv, qkv, qkv)


def _attn_out_kernel(x_ref, m_ref, o_ref_in, w_ref, out_ref):
    out_ref[...] = x_ref[...] + m_ref[5:6, :] * _dot(o_ref_in[...].astype(BF16), w_ref[...])


def _attn_out(x, modrows, o, w, n_ctx_tiles):
    rows = o.shape[0]
    return pl.pallas_call(
        _attn_out_kernel,
        grid=(rows // ROW_TILE,),
        in_specs=[pl.BlockSpec((ROW_TILE, D_MODEL), lambda i: (n_ctx_tiles + i, 0)),
                  pl.BlockSpec((None, N_MOD, D_MODEL), lambda i: (n_ctx_tiles + i, 0, 0)),
                  pl.BlockSpec((ROW_TILE, D_MODEL), lambda i: (i, 0)),
                  pl.BlockSpec(w.shape, lambda i: (0, 0))],
        out_specs=pl.BlockSpec((ROW_TILE, D_MODEL), lambda i: (i, 0)),
        out_shape=jax.ShapeDtypeStruct((rows, D_MODEL), F32),
        compiler_params=_params("parallel"),
        name="attn_out",
    )(x, modrows, o, w)


def kernel(x, c, ctx, c_ctx, norm_g, mod_w, mod_b, ffn_w1, ffn_w2, ab_in_w, ab_out_w, rwkv_mu, rwkv_w0, rwkv_w2, rwkv_a0, rwkv_a2, rwkv_g2, rwkv_k_k, rwkv_k_a, rwkv_r_k, rwkv_lnx_g, rwkv_lnx_b, s5_a_re, s5_a_im, s5_log_step, s5_b_re, s5_b_im, s5_c_re, s5_c_im, s5_d, s5_glu_w, s5_glu_b, attn_in_w, attn_out_w, attn_sink, final_g):
    bsz, n_lat, _ = x.shape
    n_ctx = ctx.shape[1]
    depth = mod_w.shape[0]
    grid_w = 64
    assert n_ctx == ROW_TILE and n_lat % FFN_ROW_TILE == 0 and (bsz * n_ctx) % FFN_ROW_TILE == 0 and bsz <= 4
    seq = n_ctx + n_lat
    n_ctx_tiles = bsz
    n_lat_tiles = n_lat // ROW_TILE
    W = RWKV_WIDTH

    cs = jnp.zeros((8, D_MODEL), F32).at[:bsz].set(c).at[bsz].set(c_ctx)
    mods = _mod_vectors(jax.nn.silu(cs), mod_w, mod_b).reshape(depth, 8, N_MOD, D_MODEL)
    tile_row = jnp.concatenate([jnp.full((n_ctx_tiles,), bsz, jnp.int32),
                                jnp.repeat(jnp.arange(bsz, dtype=jnp.int32), n_lat_tiles)])
    modrows = mods[:, tile_row]

    xs = jnp.concatenate([ctx.reshape(bsz * n_ctx, D_MODEL), x.reshape(bsz * n_lat, D_MODEL)], axis=0)
    w1 = ffn_w1.astype(BF16)
    w2 = ffn_w2.astype(BF16)

    for l in range(depth):
        last = l == depth - 1
        mr = modrows[l]
        xs = _ffn(xs, mr, norm_g[l, 0], w1[l, 0], w2[l, 0], final_g, mod0=0, row_tile_of_mod=0)
        if l % 2 == 0:
            e = l // 2
            pa, pb = _ab_in_proj(xs, mr, norm_g[l, 1], ab_in_w[e].astype(BF16), bsz, n_lat_tiles)
            zeros = jnp.zeros((64, W), F32)
            w_lora = jnp.concatenate([
                jnp.concatenate([rwkv_w2[e, 0], rwkv_w2[e, 1], zeros, zeros, zeros], axis=1),
                jnp.concatenate([zeros, zeros, rwkv_a2[e, 0], rwkv_a2[e, 1], zeros], axis=1),
                jnp.concatenate([jnp.zeros((128, 4 * W), F32), rwkv_g2[e]], axis=1)], axis=0)
            vecs = jnp.stack([rwkv_k_k[e], rwkv_k_a[e], rwkv_r_k[e].reshape(W), rwkv_w0[e, 0], rwkv_w0[e, 1],
                              rwkv_a0[e, 0], rwkv_a0[e, 1], jnp.zeros((W,), F32)])
            r, v, a, kd, bv, lw, g, bonus = _rwkv_prep(pa, rwkv_mu[e], w_lora, vecs)
            y = _rwkv_scan(r, v, a, kd, bv, lw, n_ctx // RWKV_CHUNK)
            lam, bmat, cmat = _s5_discretise(s5_a_re[e], s5_a_im[e], s5_log_step[e], s5_b_re[e], s5_b_im[e],
                                             s5_c_re[e], s5_c_im[e])
            u_t = jnp.swapaxes(pb, 0, 1)
            u_back = jnp.concatenate([u_t[:n_ctx][::-1], u_t[n_ctx:][::-1]], axis=0)
            pad = jnp.zeros((seq, 4 - bsz, S5_WIDTH), F32)
            u8 = jnp.concatenate([u_t, pad, u_back, pad], axis=1)
            y8 = _s5_scan(u8, lam, bmat, cmat)
            y_back = y8[:, 4:4 + bsz]
            ys = y8[:, :bsz] + jnp.concatenate([y_back[:n_ctx][::-1], y_back[n_ctx:][::-1]], axis=0)
            ys = jnp.swapaxes(ys, 0, 1)
            vecs_out = jnp.stack([rwkv_lnx_g[e], rwkv_lnx_b[e], s5_d[e], s5_glu_b[e]] + [jnp.zeros((W,), F32)] * 4)
            xs = _ab_out(xs, mr, y, g, bonus, ys, pb, vecs_out, s5_glu_w[e].astype(BF16),
                         ab_out_w[e].astype(BF16), bsz, n_lat_tiles)
            rows_mod0 = 0
        else:
            o = l // 2
            cos_t, sin_t = _rope_tables(n_lat, grid_w)
            qkv = _attn_in_proj(xs, mr, norm_g[l, 1], attn_in_w[o].astype(BF16), cos_t, sin_t,
                                n_ctx_tiles, n_lat_tiles)
            att = _attention(qkv, attn_sink[o], bsz, n_lat, n_ctx)
            if last:
                xs = _attn_out(xs, mr, att, attn_out_w[o].astype(BF16), n_ctx_tiles)
                rows_mod0 = n_ctx_tiles
            else:
                raise NotImplementedError("context update after an attention layer")
        xs = _ffn(xs, mr, norm_g[l, 2], w1[l, 1], w2[l, 1], final_g, mod0=6, row_tile_of_mod=rows_mod0,
                  final=last)
    return xs.reshape(bsz, n_lat, D_MODEL)
```

```python
import functools
import math

import jax
import jax.numpy as jnp
from jax import lax
from jax.experimental import pallas as pl
from jax.experimental.pallas import tpu as pltpu

F32 = jnp.float32
BF16 = jnp.bfloat16

D_MODEL = 1024
D_FF = 2816
N_MOD = 9
HEAD_DIM = 64
RMS_EPS = 1e-6
GN_EPS = 64e-5
RWKV_WIDTH = 512
RWKV_PROJ = 1792
S5_WIDTH = 512
S5_GROUP_CH = 16
S5_GROUPS = 32
S5_STATE = 64
ATTN_HEADS = 16
ATTN_KV_HEADS = 4
ATTN_BLOCK = 128
ATTN_WINDOW = 128
ATTN_Q_W = 1024
ATTN_KV_W = 256
ROPE_BASE = 10000.0
NEG_INF = -1e30

ROW_TILE = 256
FFN_ROW_TILE = 512
FFN_F_TILE = 1408
RWKV_CHUNK = 64
S5_STEPS = 64
VMEM_LIMIT = 56 * 1024 * 1024

_NN = (((1,), (0,)), ((), ()))
_NT = (((1,), (1,)), ((), ()))


def _dot(a, b, dims=_NN):
    return lax.dot_general(a, b, dims, preferred_element_type=F32)


def _split2(a):
    hi = a.astype(BF16)
    lo = (a - hi.astype(F32)).astype(BF16)
    return hi, lo


def _split3(a):
    hi = a.astype(BF16)
    r1 = a - hi.astype(F32)
    mid = r1.astype(BF16)
    lo = (r1 - mid.astype(F32)).astype(BF16)
    return hi, mid, lo


def _mm1(a, b, dims=_NN):
    return _dot(a.astype(BF16), b.astype(BF16), dims)


def _mm3(a, b, dims=_NN):
    ah, al = _split2(a)
    bh, bl = _split2(b)
    return _dot(ah, bh, dims) + (_dot(ah, bl, dims) + _dot(al, bh, dims))


def _mm_const_rhs(a, c_bf16, dims=_NN):
    hi, mid, lo = _split3(a)
    return _dot(hi, c_bf16, dims) + (_dot(mid, c_bf16, dims) + _dot(lo, c_bf16, dims))


def _mm_const_lhs(c_bf16, a, dims=_NN):
    hi, mid, lo = _split3(a)
    return _dot(c_bf16, hi, dims) + (_dot(c_bf16, mid, dims) + _dot(c_bf16, lo, dims))


def _sigmoid(x):
    return 1.0 / (1.0 + jnp.exp(-x))


def _silu(x):
    return x * _sigmoid(x)


def _params(*sem):
    return pltpu.CompilerParams(dimension_semantics=sem, vmem_limit_bytes=VMEM_LIMIT)


def _modnorm(x, g, shift, scale):
    xn = x * lax.rsqrt(jnp.mean(x * x, axis=-1, keepdims=True) + RMS_EPS)
    return (xn * g) * (1.0 + scale) + shift


def _head_ones(width):
    r = lax.broadcasted_iota(jnp.int32, (width, width), 0) >> 6
    c = lax.broadcasted_iota(jnp.int32, (width, width), 1) >> 6
    return (r == c).astype(BF16)


def _mod_kernel(c_ref, w_ref, b_ref, o_ref):
    o_ref[...] = _mm3(c_ref[...], w_ref[...]) + b_ref[...]


def _mod_vectors(cs, mod_w, mod_b):
    depth = mod_w.shape[0]
    n = mod_w.shape[2]
    tn = 1152
    return pl.pallas_call(
        _mod_kernel,
        grid=(depth, n // tn),
        in_specs=[pl.BlockSpec((8, D_MODEL), lambda l, j: (0, 0)),
                  pl.BlockSpec((None, D_MODEL, tn), lambda l, j: (l, 0, j)),
                  pl.BlockSpec((None, 1, tn), lambda l, j: (l, 0, j))],
        out_specs=pl.BlockSpec((None, 8, tn), lambda l, j: (l, 0, j)),
        out_shape=jax.ShapeDtypeStruct((depth, 8, n), F32),
        compiler_params=_params("parallel", "parallel"),
        name="mod_vectors",
    )(cs, mod_w, mod_b.reshape(depth, 1, n))


def _ffn_kernel(x_ref, m_ref, g_ref, w1g_ref, w1u_ref, w2_ref, fg_ref, o_ref, h_ref, acc_ref, *, mod0, final):
    j = pl.program_id(1)

    @pl.when(j == 0)
    def _():
        h = _modnorm(x_ref[...], g_ref[...], m_ref[mod0:mod0 + 1, :], m_ref[mod0 + 1:mod0 + 2, :])
        h_ref[...] = h.astype(BF16)

    h = h_ref[...]
    gate = _dot(h, w1g_ref[...])
    up = _dot(h, w1u_ref[...])
    act = (_silu(gate) * up).astype(BF16)
    part = _dot(act, w2_ref[...])

    @pl.when(j == 0)
    def _():
        acc_ref[...] = part

    @pl.when(j > 0)
    def _():
        acc_ref[...] += part

    @pl.when(j == pl.num_programs(1) - 1)
    def _():
        y = x_ref[...] + (0.5 * m_ref[mod0 + 2:mod0 + 3, :]) * acc_ref[...]
        if final:
            y = y * lax.rsqrt(jnp.mean(y * y, axis=-1, keepdims=True) + RMS_EPS) * fg_ref[...]
        o_ref[...] = y


def _ffn(x, modrows, g, w1, w2, final_g, *, mod0, row_tile_of_mod, final=False):
    rows = x.shape[0]
    tm, tf = FFN_ROW_TILE, FFN_F_TILE
    nf = D_FF // tf
    step = tm // ROW_TILE
    return pl.pallas_call(
        functools.partial(_ffn_kernel, mod0=mod0, final=final),
        grid=(rows // tm, nf),
        in_specs=[pl.BlockSpec((tm, D_MODEL), lambda i, j: (i, 0)),
                  pl.BlockSpec((None, N_MOD, D_MODEL), lambda i, j: (row_tile_of_mod + i * step, 0, 0)),
                  pl.BlockSpec((1, D_MODEL), lambda i, j: (0, 0)),
                  pl.BlockSpec((D_MODEL, tf), lambda i, j: (0, j)),
                  pl.BlockSpec((D_MODEL, tf), lambda i, j: (0, nf + j)),
                  pl.BlockSpec((tf, D_MODEL), lambda i, j: (j, 0)),
                  pl.BlockSpec((1, D_MODEL), lambda i, j: (0, 0))],
        out_specs=pl.BlockSpec((tm, D_MODEL), lambda i, j: (i, 0)),
        out_shape=jax.ShapeDtypeStruct((rows, D_MODEL), F32),
        scratch_shapes=[pltpu.VMEM((tm, D_MODEL), BF16), pltpu.VMEM((tm, D_MODEL), F32)],
        compiler_params=_params("parallel", "arbitrary"),
        name="ffn",
    )(x, modrows, g.reshape(1, D_MODEL), w1, w1, w2, final_g.reshape(1, D_MODEL))


def _x_tile(bsz, n_lat_tiles):
    return lambda b, k: (jnp.where(k == 0, b, bsz + b * n_lat_tiles + k - 1), 0)


def _ab_in_kernel(x_ref, m_ref, g_ref, w_ref, pa_ref, pb_ref):
    h = _modnorm(x_ref[...], g_ref[...], m_ref[3:4, :], m_ref[4:5, :]).astype(BF16)
    p = _dot(h, w_ref[...])
    pa_ref[...] = p[:, :RWKV_PROJ]
    pb_ref[...] = p[:, RWKV_PROJ:]


def _ab_in_proj(x, modrows, g, w, bsz, n_lat_tiles):
    nt = n_lat_tiles + 1
    seq = nt * ROW_TILE
    xt = _x_tile(bsz, n_lat_tiles)
    return pl.pallas_call(
        _ab_in_kernel,
        grid=(bsz, nt),
        in_specs=[pl.BlockSpec((ROW_TILE, D_MODEL), xt),
                  pl.BlockSpec((None, N_MOD, D_MODEL), lambda b, k: (xt(b, k)[0], 0, 0)),
                  pl.BlockSpec((1, D_MODEL), lambda b, k: (0, 0)),
                  pl.BlockSpec(w.shape, lambda b, k: (0, 0))],
        out_specs=[pl.BlockSpec((None, ROW_TILE, RWKV_PROJ), lambda b, k: (b, k, 0)),
                   pl.BlockSpec((None, ROW_TILE, S5_WIDTH), lambda b, k: (b, k, 0))],
        out_shape=[jax.ShapeDtypeStruct((bsz, seq, RWKV_PROJ), F32),
                   jax.ShapeDtypeStruct((bsz, seq, S5_WIDTH), F32)],
        compiler_params=_params("parallel", "parallel"),
        name="ab_in_proj",
    )(x, modrows, g.reshape(1, D_MODEL), w)


def _rwkv_prep_kernel(cur_ref, prev_ref, next_ref, mu_ref, wl_ref, vec_ref,
                      r_ref, v_ref, a_ref, k_ref, b_ref, lw_ref, g_ref, bonus_ref):
    k_idx = pl.program_id(1)
    nt = pl.num_programs(1)
    cur = cur_ref[...]
    rows = cur.shape[0]
    row = lax.broadcasted_iota(jnp.int32, cur.shape, 0)
    use_prev = k_idx >= 2
    use_next = jnp.logical_and(k_idx >= 1, k_idx < nt - 1)
    prev_row = jnp.where(use_prev, prev_ref[7:8, :], 0.0)
    next_row = jnp.where(use_next, next_ref[0:1, :], 0.0)
    prev = jnp.where(row == 0, prev_row, pltpu.roll(cur, 1, axis=0))
    nxt = jnp.where(row == rows - 1, next_row, pltpu.roll(cur, rows - 1, axis=0))
    p = cur + mu_ref[0:1, :] * (prev - cur) + mu_ref[1:2, :] * (nxt - cur)

    W = RWKV_WIDTH
    r, k, v = p[:, :W], p[:, W:2 * W], p[:, 2 * W:3 * W]
    lora_in = p[:, 3 * W:]
    lane = lax.broadcasted_iota(jnp.int32, lora_in.shape, 1)
    lora_act = jnp.where(lane < 64, jnp.tanh(lora_in), jnp.where(lane < 128, lora_in, _sigmoid(lora_in)))
    lo = _mm3(lora_act, wl_ref[...])

    k_k, k_a, r_k = vec_ref[0:1, :], vec_ref[1:2, :], vec_ref[2:3, :]
    ones = _head_ones(W)
    kk = k * k_k
    kk = kk * lax.rsqrt(jnp.maximum(_mm_const_rhs(kk * kk, ones), 1e-12))
    r_ref[...] = r
    v_ref[...] = v
    a_ref[...] = -kk
    k_sum = jnp.zeros_like(k)
    for d in range(2):
        z = -(vec_ref[3 + d:4 + d, :] + lo[:, d * W:(d + 1) * W])
        softplus = jnp.maximum(z, 0.0) + jnp.log(1.0 + jnp.exp(-jnp.abs(z)))
        lw_ref[d] = -jnp.exp(-softplus - 0.5)
        a = _sigmoid(vec_ref[5 + d:6 + d, :] + lo[:, (2 + d) * W:(3 + d) * W])
        kd = k * (1.0 + (a - 1.0) * k_a)
        k_ref[d] = kd
        b_ref[d] = kk * a
        k_sum = k_sum + kd
    g_ref[...] = lo[:, 4 * W:]
    bonus_ref[...] = _mm_const_rhs(r * (0.5 * k_sum) * r_k, ones) * v


def _rwkv_prep(pa, mu, w_lora, vecs):
    bsz, seq, _ = pa.shape
    nt = seq // ROW_TILE
    sub = ROW_TILE // 8
    last_blk = seq // 8 - 1
    W = RWKV_WIDTH
    one = jax.ShapeDtypeStruct((bsz, seq, W), F32)
    two = jax.ShapeDtypeStruct((2, bsz, seq, W), F32)
    spec1 = pl.BlockSpec((None, ROW_TILE, W), lambda b, k: (b, k, 0))
    spec2 = pl.BlockSpec((2, None, ROW_TILE, W), lambda b, k: (0, b, k, 0))
    return pl.pallas_call(
        _rwkv_prep_kernel,
        grid=(bsz, nt),
        in_specs=[pl.BlockSpec((None, ROW_TILE, RWKV_PROJ), lambda b, k: (b, k, 0)),
                  pl.BlockSpec((None, 8, RWKV_PROJ), lambda b, k: (b, jnp.maximum(k * sub - 1, 0), 0)),
                  pl.BlockSpec((None, 8, RWKV_PROJ), lambda b, k: (b, jnp.minimum((k + 1) * sub, last_blk), 0)),
                  pl.BlockSpec(mu.shape, lambda b, k: (0, 0)),
                  pl.BlockSpec(w_lora.shape, lambda b, k: (0, 0)),
                  pl.BlockSpec(vecs.shape, lambda b, k: (0, 0))],
        out_specs=[spec1, spec1, spec1, spec2, spec2, spec2, spec1, spec1],
        out_shape=[one, one, one, two, two, two, one, one],
        compiler_params=_params("parallel", "parallel"),
        name="rwkv_prep",
    )(pa, pa, pa, mu, w_lora, vecs)


RWKV_QUAD = 4 * HEAD_DIM


def _rwkv_chunk(fwd, r, v, a, k, b, lw, h_ref, y_ref):
    C = RWKV_CHUNK
    QW = RWKV_QUAD
    rr = lax.broadcasted_iota(jnp.int32, (C, C), 0)
    cc = lax.broadcasted_iota(jnp.int32, (C, C), 1)
    tri = ((rr >= cc) if fwd else (rr <= cc)).astype(BF16)
    hi, mid, lo = _split3(lw)
    cum = _dot(jnp.concatenate([tri, tri, tri], axis=1), jnp.concatenate([hi, mid, lo], axis=0))
    last = cum[C - 1:C, :] if fwd else cum[0:1, :]
    einv = jnp.exp(-cum)
    to_end = jnp.exp(last - cum)
    at = a * jnp.exp(cum - lw)
    rt = r * jnp.exp(cum)
    bt = b * einv
    kt = k * einv
    b_end = b * to_end
    k_end = k * to_end
    p_end = jnp.exp(last)

    tq = lax.broadcasted_iota(jnp.int32, (C, QW), 0)
    sq = lax.broadcasted_iota(jnp.int32, (C, QW), 1) & (C - 1)
    strict = (tq > sq) if fwd else (tq < sq)
    incl = (tq >= sq) if fwd else (tq <= sq)
    eye_cat = (tq == sq).astype(F32)
    rb = lax.broadcasted_iota(jnp.int32, (QW, QW), 0)
    cb = lax.broadcasted_iota(jnp.int32, (QW, QW), 1)
    same_head = (rb >> 6) == (cb >> 6)
    diag = rb == cb

    def stack4(x):
        return jnp.where(same_head, jnp.concatenate([x, x, x, x], axis=0), jnp.zeros((), x.dtype))

    for q in range(RWKV_WIDTH // QW):
        sl = slice(QW * q, QW * (q + 1))
        at_q, rt_q, v_q = at[:, sl], rt[:, sl], v[:, sl]
        v_s = stack4(v_q.astype(BF16))
        lhs = jnp.concatenate([at_q, rt_q], axis=0).astype(BF16)
        rhs = jnp.concatenate([stack4(bt[:, sl].astype(BF16)), stack4(kt[:, sl].astype(BF16))], axis=0)
        s_all = _dot(lhs, rhs, _NT)
        n_ab = jnp.where(strict, s_all[:C, :QW], 0.0)
        n_ak = jnp.where(strict, s_all[:C, QW:], 0.0)
        n_rb = jnp.where(incl, s_all[C:, :QW], 0.0)
        n_rk = jnp.where(incl, s_all[C:, QW:], 0.0)
        tinv = eye_cat + n_ab
        pw = n_ab
        levels = int(math.log2(C))
        for i in range(levels):
            ph, pl_ = _split2(pw)
            lhs3 = jnp.concatenate([ph, ph, pl_], axis=1)
            sq = jnp.concatenate([stack4(ph), stack4(pl_), stack4(ph)], axis=0)
            if i == 0:
                pw = _dot(lhs3, sq)
                continue
            th, tl = _split2(tinv)
            up = jnp.concatenate([stack4(th), stack4(tl), stack4(th)], axis=0)
            if i == levels - 1:
                tinv = tinv + _dot(lhs3, up)
            else:
                res = _dot(lhs3, jnp.concatenate([sq, up], axis=1))
                pw = res[:, :QW]
                tinv = tinv + res[:, QW:]
        kv = _dot(jnp.concatenate([n_ak, n_rk], axis=0).astype(BF16), v_s)
        aw = _dot(tinv.astype(BF16),
                  jnp.concatenate([stack4(at_q.astype(BF16)), stack4(kv[:C].astype(BF16))], axis=1))
        aw_b = aw.astype(BF16)
        c1 = _dot(n_rb.astype(BF16), jnp.concatenate([stack4(aw_b[:, :QW]), stack4(aw_b[:, QW:])], axis=1))
        qm = rt_q + c1[:, :QW]
        z = c1[:, QW:] + kv[C:]
        bk_t = jnp.concatenate([b_end[:, sl], k_end[:, sl]], axis=0).T.astype(BF16)
        low = jnp.concatenate([jnp.zeros((C, QW), BF16), v_q.astype(BF16)], axis=1)
        c2 = _dot(bk_t, jnp.concatenate([aw_b, low], axis=0))
        m_upd = jnp.where(same_head, c2[:, :QW], 0.0)
        g_upd = jnp.where(same_head, c2[:, QW:], 0.0)
        h = h_ref[q]
        d = _dot(jnp.concatenate([m_upd, qm], axis=0).astype(BF16), h.astype(BF16))
        y_ref[:, sl] = d[QW:] + z
        p_col = jnp.sum(jnp.where(diag, p_end[:, sl], 0.0), axis=1, keepdims=True)
        h_ref[q] = p_col * h + d[:QW] + g_upd


def _rwkv_scan_kernel(rf_ref, vf_ref, af_ref, rb_ref, vb_ref, ab_ref, kf_ref, bf_ref, lwf_ref,
                      kb_ref, bb_ref, lwb_ref, yf_ref, yb_ref, h_ref):
    @pl.when(pl.program_id(1) == 0)
    def _():
        h_ref[...] = jnp.zeros_like(h_ref)

    _rwkv_chunk(True, rf_ref[...], vf_ref[...], af_ref[...], kf_ref[...], bf_ref[...], lwf_ref[...],
                h_ref.at[0], yf_ref)
    _rwkv_chunk(False, rb_ref[...], vb_ref[...], ab_ref[...], kb_ref[...], bb_ref[...], lwb_ref[...],
                h_ref.at[1], yb_ref)


def _rwkv_scan(r, v, a, k, b, lw, n_ctx_chunks):
    bsz, seq, W = r.shape
    C = RWKV_CHUNK
    nch = seq // C

    def back(j):
        return jnp.where(j < n_ctx_chunks, n_ctx_chunks - 1 - j, nch - 1 - (j - n_ctx_chunks))

    one_f = pl.BlockSpec((None, C, W), lambda g, j: (g, j, 0))
    one_b = pl.BlockSpec((None, C, W), lambda g, j: (g, back(j), 0))
    two_f = pl.BlockSpec((None, None, C, W), lambda g, j: (0, g, j, 0))
    two_b = pl.BlockSpec((None, None, C, W), lambda g, j: (1, g, back(j), 0))
    out = jax.ShapeDtypeStruct((bsz, seq, W), F32)
    return pl.pallas_call(
        _rwkv_scan_kernel,
        grid=(bsz, nch),
        in_specs=[one_f, one_f, one_f, one_b, one_b, one_b, two_f, two_f, two_f, two_b, two_b, two_b],
        out_specs=[one_f, one_b],
        out_shape=[out, out],
        scratch_shapes=[pltpu.VMEM((2, W // RWKV_QUAD, RWKV_QUAD, RWKV_QUAD), F32)],
        compiler_params=_params("parallel", "arbitrary"),
        name="rwkv_scan",
    )(r, v, a, r, v, a, k, b, lw, k, b, lw)


def _s5_scan_kernel(uf_ref, ub_ref, lam_ref, bmat_ref, cmat_ref, yf_ref, yb_ref, bu_ref, st_ref, *, bsz):
    S = S5_STEPS
    half = 8 * S5_STATE
    n_nat = 2 * bsz * S

    @pl.when(pl.program_id(0) == 0)
    def _():
        st_ref[...] = jnp.zeros_like(st_ref)

    def perm_mask(rho, col):
        t, s = rho >> 3, rho & 7
        src = jnp.where(s < 4, s * S + t, bsz * S + (s - 4) * S + (S - 1 - t))
        return jnp.logical_and((s & 3) < bsz, col == src).astype(BF16)

    perm = perm_mask(lax.broadcasted_iota(jnp.int32, (S * 8, n_nat), 0),
                     lax.broadcasted_iota(jnp.int32, (S * 8, n_nat), 1))
    perm_t = perm_mask(lax.broadcasted_iota(jnp.int32, (n_nat, S * 8), 1),
                       lax.broadcasted_iota(jnp.int32, (n_nat, S * 8), 0))
    u_nat = jnp.concatenate([uf_ref[...].reshape(bsz * S, S5_WIDTH), ub_ref[...].reshape(bsz * S, S5_WIDTH)], axis=0)
    u = _dot(perm, u_nat.astype(BF16))
    is_fwd = (lax.broadcasted_iota(jnp.int32, (S * 8, 128), 0) & 7) < 4
    for q in range(S5_WIDTH // 128):
        uq = u[:, 128 * q:128 * (q + 1)]
        lhs = jnp.concatenate([jnp.where(is_fwd, uq, 0.0), jnp.where(is_fwd, 0.0, uq)], axis=1)
        bu_ref[:, 2 * half * q:2 * half * (q + 1)] = _dot(lhs.astype(BF16), bmat_ref[q])

    for q in range(S5_WIDTH // 128):
        base = 2 * half * q
        ar = lam_ref[:, base:base + half]
        ai = lam_ref[:, base + half:base + 2 * half]

        def step(t, carry):
            xr, xi = carry
            rows = pl.ds(pl.multiple_of(t * 8, 8), 8)
            nr = ar * xr - ai * xi + bu_ref[rows, base:base + half]
            ni = ar * xi + ai * xr + bu_ref[rows, base + half:base + 2 * half]
            bu_ref[rows, base:base + half] = nr
            bu_ref[rows, base + half:base + 2 * half] = ni
            return nr, ni

        xr, xi = lax.fori_loop(0, S, step, (st_ref[:, base:base + half], st_ref[:, base + half:base + 2 * half]),
                               unroll=4)
        st_ref[:, base:base + half] = xr
        st_ref[:, base + half:base + 2 * half] = xi

    for q in range(S5_WIDTH // 128):
        xq = bu_ref[:, 2 * half * q:2 * half * (q + 1)]
        y2 = _dot(xq.astype(BF16), cmat_ref[q])
        hi, lo = _split2(jnp.where(is_fwd, y2[:, :128], y2[:, 128:]))
        y_nat = _dot(perm_t, hi) + _dot(perm_t, lo)
        yf_ref[:, :, 128 * q:128 * (q + 1)] = y_nat[:bsz * S].reshape(bsz, S, 128)
        yb_ref[:, :, 128 * q:128 * (q + 1)] = y_nat[bsz * S:].reshape(bsz, S, 128)


def _s5_scan(u, lam, bmat, cmat, n_ctx_blocks):
    bsz, seq, _ = u.shape
    S = S5_STEPS
    nblk = seq // S
    nstate = 2 * S5_GROUPS * S5_STATE

    def back(i):
        return jnp.where(i < n_ctx_blocks, n_ctx_blocks - 1 - i, nblk - 1 - (i - n_ctx_blocks))

    spec_f = pl.BlockSpec((bsz, S, S5_WIDTH), lambda i: (0, i, 0))
    spec_b = pl.BlockSpec((bsz, S, S5_WIDTH), lambda i: (0, back(i), 0))
    out = jax.ShapeDtypeStruct(u.shape, F32)
    return pl.pallas_call(
        functools.partial(_s5_scan_kernel, bsz=bsz),
        grid=(nblk,),
        in_specs=[spec_f, spec_b,
                  pl.BlockSpec(lam.shape, lambda i: (0, 0)),
                  pl.BlockSpec(bmat.shape, lambda i: (0, 0, 0)),
                  pl.BlockSpec(cmat.shape, lambda i: (0, 0, 0))],
        out_specs=[spec_f, spec_b],
        out_shape=[out, out],
        scratch_shapes=[pltpu.VMEM((S * 8, nstate), F32), pltpu.VMEM((8, nstate), F32)],
        compiler_params=_params("arbitrary"),
        name="s5_scan",
    )(u, u, lam, bmat, cmat)


def _s5_discretise(a_re, a_im, log_step, b_re, b_im, c_re, c_im):
    lam_re = jnp.minimum(a_re, -1e-4)
    lam_im = a_im
    dt = jnp.exp(log_step)[..., None]
    mag = jnp.exp(lam_re * dt)
    ab_re, ab_im = mag * jnp.cos(lam_im * dt), mag * jnp.sin(lam_im * dt)
    den = lam_re * lam_re + lam_im * lam_im
    f_re = ((ab_re - 1.0) * lam_re + ab_im * lam_im) / den
    f_im = (ab_im * lam_re - (ab_re - 1.0) * lam_im) / den
    bb_re = f_re[..., None] * b_re - f_im[..., None] * b_im
    bb_im = f_re[..., None] * b_im + f_im[..., None] * b_re
    nq = S5_WIDTH // 128
    eye8 = jnp.eye(8, dtype=F32)

    def lanes(t):
        return t.reshape(2, nq, 8 * S5_STATE)

    lam = jnp.concatenate([lanes(ab_re), lanes(ab_im)], axis=-1).reshape(2, nq * 16 * S5_STATE)
    lam = jnp.repeat(lam, 4, axis=0)

    def in_block(t):
        t = t.reshape(2, nq, 8, S5_STATE, S5_GROUP_CH)
        return jnp.einsum('dqgpi,gh->dqgihp', t, eye8).reshape(2, nq, 128, 8 * S5_STATE)

    bmat = jnp.concatenate([in_block(bb_re), in_block(bb_im)], axis=-1)
    bmat = jnp.concatenate([bmat[0], bmat[1]], axis=1)

    def out_block(t):
        t = t.reshape(2, nq, 8, S5_GROUP_CH, S5_STATE)
        return jnp.einsum('dqgip,gh->dqgphi', t, eye8).reshape(2, nq, 8 * S5_STATE, 128)

    cmat = jnp.concatenate([out_block(c_re), -out_block(c_im)], axis=2)
    cmat = jnp.concatenate([cmat[0], cmat[1]], axis=-1)
    return lam, bmat.astype(BF16), cmat.astype(BF16)


def _ab_out_kernel(x_ref, m_ref, yf_ref, yb_ref, g_ref, bonus_ref, sf_ref, sb_ref, u_ref, vec_ref, gluw_ref, w_ref,
                   o_ref):
    W = RWKV_WIDTH
    ones = _head_ones(W)
    y = yf_ref[...] + yb_ref[...]
    mu = _mm_const_rhs(y, ones) * (1.0 / HEAD_DIM)
    yc = y - mu
    var = _mm_const_rhs(yc * yc, ones) * (1.0 / HEAD_DIM)
    ya = (yc * lax.rsqrt(var + GN_EPS) * vec_ref[0:1, :] + vec_ref[1:2, :] + bonus_ref[...]) * g_ref[...]

    s = (sf_ref[...] + sb_ref[...]) + vec_ref[2:3, :] * u_ref[...]
    z = 0.5 * s * (1.0 + jnp.tanh(math.sqrt(2.0 / math.pi) * (s + 0.044715 * (s * s * s))))
    yb = z * _sigmoid(_mm1(z, gluw_ref[...]) + vec_ref[3:4, :])
    out = _dot(ya.astype(BF16), w_ref[:W, :]) + _dot(yb.astype(BF16), w_ref[W:, :])
    o_ref[...] = x_ref[...] + m_ref[5:6, :] * out


def _ab_out(x, modrows, yf, yb, g, bonus, sf, sb, u, vecs, glu_w, out_w, bsz, n_lat_tiles):
    nt = n_lat_tiles + 1
    W = RWKV_WIDTH
    xt = _x_tile(bsz, n_lat_tiles)
    seq_spec = pl.BlockSpec((None, ROW_TILE, W), lambda b, k: (b, k, 0))
    return pl.pallas_call(
        _ab_out_kernel,
        grid=(bsz, nt),
        in_specs=[pl.BlockSpec((ROW_TILE, D_MODEL), xt),
                  pl.BlockSpec((None, N_MOD, D_MODEL), lambda b, k: (xt(b, k)[0], 0, 0)),
                  seq_spec, seq_spec, seq_spec, seq_spec, seq_spec, seq_spec, seq_spec,
                  pl.BlockSpec(vecs.shape, lambda b, k: (0, 0)),
                  pl.BlockSpec(glu_w.shape, lambda b, k: (0, 0)),
                  pl.BlockSpec(out_w.shape, lambda b, k: (0, 0))],
        out_specs=pl.BlockSpec((ROW_TILE, D_MODEL), xt),
        out_shape=jax.ShapeDtypeStruct(x.shape, F32),
        compiler_params=_params("parallel", "parallel"),
        name="ab_out",
    )(x, modrows, yf, yb, g, bonus, sf, sb, u, vecs, glu_w, out_w)


def _attn_in_kernel(x_ref, m_ref, g_ref, w_ref, cos_ref, sin_ref, o_ref):
    h = _modnorm(x_ref[...], g_ref[...], m_ref[3:4, :], m_ref[4:5, :]).astype(BF16)
    p = _dot(h, w_ref[...])
    qk_w = ATTN_Q_W + ATTN_KV_W
    qk = p[:, :qk_w]
    half = HEAD_DIM // 2
    lane = lax.broadcasted_iota(jnp.int32, qk.shape, 1)
    first = (lane & (HEAD_DIM - 1)) < half
    partner = jnp.where(first, pltpu.roll(qk, qk_w - half, axis=1), pltpu.roll(qk, half, axis=1))
    reps = qk_w // 128
    cos = jnp.concatenate([cos_ref[...]] * reps, axis=1)
    sin = jnp.concatenate([sin_ref[...]] * reps, axis=1)
    o_ref[:, :qk_w] = qk * cos + partner * sin
    o_ref[:, qk_w:] = p[:, qk_w:]


def _attn_in_proj(x, modrows, g, w, cos_t, sin_t, n_ctx_tiles, n_lat_tiles):
    rows = x.shape[0]
    width = w.shape[1]

    def rope_tile(i):
        return (jnp.where(i < n_ctx_tiles, 0, 1 + (i - n_ctx_tiles) % n_lat_tiles), 0)

    return pl.pallas_call(
        _attn_in_kernel,
        grid=(rows // ROW_TILE,),
        in_specs=[pl.BlockSpec((ROW_TILE, D_MODEL), lambda i: (i, 0)),
                  pl.BlockSpec((None, N_MOD, D_MODEL), lambda i: (i, 0, 0)),
                  pl.BlockSpec((1, D_MODEL), lambda i: (0, 0)),
                  pl.BlockSpec(w.shape, lambda i: (0, 0)),
                  pl.BlockSpec((ROW_TILE, 128), rope_tile),
                  pl.BlockSpec((ROW_TILE, 128), rope_tile)],
        out_specs=pl.BlockSpec((ROW_TILE, width), lambda i: (i, 0)),
        out_shape=jax.ShapeDtypeStruct((rows, width), F32),
        compiler_params=_params("parallel"),
        name="attn_in_proj",
    )(x, modrows, g.reshape(1, D_MODEL), w, cos_t, sin_t)


def _rope_tables(n_lat, grid_w):
    half = HEAD_DIM // 2
    t = jnp.arange(n_lat)
    row_id = (t // grid_w).astype(F32)
    col_id = (t % grid_w).astype(F32)
    inv_freq = ROPE_BASE ** (-jnp.arange(0, half, 2, dtype=F32) / half)
    ang = jnp.concatenate([row_id[:, None] * inv_freq, col_id[:, None] * inv_freq], axis=-1)
    cos, sin = jnp.cos(ang), jnp.sin(ang)
    cos_t = jnp.concatenate([cos, cos, cos, cos], axis=-1)
    sin_t = jnp.concatenate([-sin, sin, -sin, sin], axis=-1)
    cos_t = jnp.concatenate([jnp.ones((ROW_TILE, 128), F32), cos_t], axis=0)
    sin_t = jnp.concatenate([jnp.zeros((ROW_TILE, 128), F32), sin_t], axis=0)
    return cos_t, sin_t


def _attn_kernel(sink_ref, q_ref, kp_ref, kc_ref, kn_ref, vp_ref, vc_ref, vn_ref, kx_ref, vx_ref, o_ref, *, nblk):
    i = pl.program_id(1)
    Q = ATTN_BLOCK
    G = ATTN_HEADS // ATTN_KV_HEADS
    k_all = jnp.concatenate([kp_ref[...], kc_ref[...], kn_ref[...], kx_ref[...]], axis=0).astype(BF16)
    v_all = jnp.concatenate([vp_ref[...], vc_ref[...], vn_ref[...], vx_ref[...]], axis=0).astype(BF16)
    nk = k_all.shape[0]
    qi = lax.broadcasted_iota(jnp.int32, (G * Q, nk), 0) & (Q - 1)
    cj = lax.broadcasted_iota(jnp.int32, (G * Q, nk), 1)
    mj = cj - Q
    blk = jnp.where(cj < Q, i - 1, jnp.where(cj < 2 * Q, i, i + 1))
    valid = jnp.logical_and(jnp.abs(mj - qi) <= ATTN_WINDOW, jnp.logical_and(blk >= 0, blk < nblk))
    valid = jnp.logical_or(valid, cj >= 3 * Q)
    q = q_ref[...] * (HEAD_DIM ** -0.5)
    for kh in range(ATTN_KV_HEADS):
        q4 = jnp.concatenate([q[:, (kh * G + g) * HEAD_DIM:(kh * G + g + 1) * HEAD_DIM] for g in range(G)], axis=0)
        kk = k_all[:, kh * HEAD_DIM:(kh + 1) * HEAD_DIM]
        vv = v_all[:, kh * HEAD_DIM:(kh + 1) * HEAD_DIM]
        s = _dot(q4.astype(BF16), kk, _NT)
        s = jnp.where(valid, s, NEG_INF)
        row_head = lax.broadcasted_iota(jnp.int32, (G * Q, 1), 0) >> 7
        sink = jnp.zeros((G * Q, 1), F32)
        for g in range(G):
            sink = jnp.where(row_head == g, sink_ref[kh * G + g], sink)
        m = jnp.maximum(jnp.max(s, axis=-1, keepdims=True), sink)
        pr = jnp.exp(s - m)
        den = jnp.sum(pr, axis=-1, keepdims=True) + jnp.exp(sink - m)
        o = _dot(pr.astype(BF16), vv) / den
        for g in range(G):
            hq = kh * G + g
            o_ref[:, hq * HEAD_DIM:(hq + 1) * HEAD_DIM] = o[g * Q:(g + 1) * Q]


def _attention(qkv, sink, bsz, n_lat, n_ctx):
    Q = ATTN_BLOCK
    nblk = n_lat // Q
    lat0 = bsz * n_ctx // Q
    kcol, vcol = ATTN_Q_W // ATTN_KV_W, ATTN_Q_W // ATTN_KV_W + 1

    def kv_spec(col, off):
        return pl.BlockSpec((Q, ATTN_KV_W),
                            lambda b, i: (lat0 + b * nblk + jnp.clip(i + off, 0, nblk - 1), col))

    def ctx_spec(col):
        return pl.BlockSpec((n_ctx, ATTN_KV_W), lambda b, i: (b, col))

    return pl.pallas_call(
        functools.partial(_attn_kernel, nblk=nblk),
        grid=(bsz, nblk),
        in_specs=[pl.BlockSpec(memory_space=pltpu.SMEM),
                  pl.BlockSpec((Q, ATTN_Q_W), lambda b, i: (lat0 + b * nblk + i, 0)),
                  kv_spec(kcol, -1), kv_spec(kcol, 0), kv_spec(kcol, 1),
                  kv_spec(vcol, -1), kv_spec(vcol, 0), kv_spec(vcol, 1),
                  ctx_spec(kcol), ctx_spec(vcol)],
        out_specs=pl.BlockSpec((Q, ATTN_Q_W), lambda b, i: (b * nblk + i, 0)),
        out_shape=jax.ShapeDtypeStruct((bsz * n_lat, ATTN_Q_W), F32),
        compiler_params=_params("parallel", "parallel"),
        name="window_attention",
    )(sink, qkv, qkv, qkv, qkv, qkv, qkv, qkv, qkv, qkv)


def _attn_out_kernel(x_ref, m_ref, o_ref_in, w_ref, out_ref):
    out_ref[...] = x_ref[...] + m_ref[5:6, :] * _dot(o_ref_in[...].astype(BF16), w_ref[...])


def _attn_out(x, modrows, o, w, n_ctx_tiles):
    rows = o.shape[0]
    return pl.pallas_call(
        _attn_out_kernel,
        grid=(rows // ROW_TILE,),
        in_specs=[pl.BlockSpec((ROW_TILE, D_MODEL), lambda i: (n_ctx_tiles + i, 0)),
                  pl.BlockSpec((None, N_MOD, D_MODEL), lambda i: (n_ctx_tiles + i, 0, 0)),
                  pl.BlockSpec((ROW_TILE, D_MODEL), lambda i: (i, 0)),
                  pl.BlockSpec(w.shape, lambda i: (0, 0))],
        out_specs=pl.BlockSpec((ROW_TILE, D_MODEL), lambda i: (i, 0)),
        out_shape=jax.ShapeDtypeStruct((rows, D_MODEL), F32),
        compiler_params=_params("parallel"),
        name="attn_out",
    )(x, modrows, o, w)


def kernel(x, c, ctx, c_ctx, norm_g, mod_w, mod_b, ffn_w1, ffn_w2, ab_in_w, ab_out_w, rwkv_mu, rwkv_w0, rwkv_w2, rwkv_a0, rwkv_a2, rwkv_g2, rwkv_k_k, rwkv_k_a, rwkv_r_k, rwkv_lnx_g, rwkv_lnx_b, s5_a_re, s5_a_im, s5_log_step, s5_b_re, s5_b_im, s5_c_re, s5_c_im, s5_d, s5_glu_w, s5_glu_b, attn_in_w, attn_out_w, attn_sink, final_g):
    bsz, n_lat, _ = x.shape
    n_ctx = ctx.shape[1]
    depth = mod_w.shape[0]
    grid_w = 64
    assert n_ctx == ROW_TILE and n_lat % FFN_ROW_TILE == 0 and (bsz * n_ctx) % FFN_ROW_TILE == 0 and bsz <= 4
    seq = n_ctx + n_lat
    n_ctx_tiles = bsz
    n_lat_tiles = n_lat // ROW_TILE
    W = RWKV_WIDTH

    cs = jnp.zeros((8, D_MODEL), F32).at[:bsz].set(c).at[bsz].set(c_ctx)
    mods = _mod_vectors(jax.nn.silu(cs), mod_w, mod_b).reshape(depth, 8, N_MOD, D_MODEL)
    tile_row = jnp.concatenate([jnp.full((n_ctx_tiles,), bsz, jnp.int32),
                                jnp.repeat(jnp.arange(bsz, dtype=jnp.int32), n_lat_tiles)])
    modrows = mods[:, tile_row]

    xs = jnp.concatenate([ctx.reshape(bsz * n_ctx, D_MODEL), x.reshape(bsz * n_lat, D_MODEL)], axis=0)
    w1 = ffn_w1.astype(BF16)
    w2 = ffn_w2.astype(BF16)

    for l in range(depth):
        last = l == depth - 1
        mr = modrows[l]
        xs = _ffn(xs, mr, norm_g[l, 0], w1[l, 0], w2[l, 0], final_g, mod0=0, row_tile_of_mod=0)
        if l % 2 == 0:
            e = l // 2
            pa, pb = _ab_in_proj(xs, mr, norm_g[l, 1], ab_in_w[e].astype(BF16), bsz, n_lat_tiles)
            zeros = jnp.zeros((64, W), F32)
            w_lora = jnp.concatenate([
                jnp.concatenate([rwkv_w2[e, 0], rwkv_w2[e, 1], zeros, zeros, zeros], axis=1),
                jnp.concatenate([zeros, zeros, rwkv_a2[e, 0], rwkv_a2[e, 1], zeros], axis=1),
                jnp.concatenate([jnp.zeros((128, 4 * W), F32), rwkv_g2[e]], axis=1)], axis=0)
            vecs = jnp.stack([rwkv_k_k[e], rwkv_k_a[e], rwkv_r_k[e].reshape(W), rwkv_w0[e, 0], rwkv_w0[e, 1],
                              rwkv_a0[e, 0], rwkv_a0[e, 1], jnp.zeros((W,), F32)])
            r, v, a, kd, bv, lw, g, bonus = _rwkv_prep(pa, rwkv_mu[e], w_lora, vecs)
            yf, yb = _rwkv_scan(r, v, a, kd, bv, lw, n_ctx // RWKV_CHUNK)
            lam, bmat, cmat = _s5_discretise(s5_a_re[e], s5_a_im[e], s5_log_step[e], s5_b_re[e], s5_b_im[e],
                                             s5_c_re[e], s5_c_im[e])
            sf, sb = _s5_scan(pb, lam, bmat, cmat, n_ctx // S5_STEPS)
            vecs_out = jnp.stack([rwkv_lnx_g[e], rwkv_lnx_b[e], s5_d[e], s5_glu_b[e]] + [jnp.zeros((W,), F32)] * 4)
            xs = _ab_out(xs, mr, yf, yb, g, bonus, sf, sb, pb, vecs_out, s5_glu_w[e].astype(BF16),
                         ab_out_w[e].astype(BF16), bsz, n_lat_tiles)
            rows_mod0 = 0
        else:
            o = l // 2
            cos_t, sin_t = _rope_tables(n_lat, grid_w)
            qkv = _attn_in_proj(xs, mr, norm_g[l, 1], attn_in_w[o].astype(BF16), cos_t, sin_t,
                                n_ctx_tiles, n_lat_tiles)
            att = _attention(qkv, attn_sink[o], bsz, n_lat, n_ctx)
            if last:
                xs = _attn_out(xs, mr, att, attn_out_w[o].astype(BF16), n_ctx_tiles)
                rows_mod0 = n_ctx_tiles
            else:
                raise NotImplementedError("context update after an attention layer")
        xs = _ffn(xs, mr, norm_g[l, 2], w1[l, 1], w2[l, 1], final_g, mod0=6, row_tile_of_mod=rows_mod0,
                  final=last)
    return xs.reshape(bsz, n_lat, D_MODEL)
```

```python
import functools
import math

import jax
import jax.numpy as jnp
from jax import lax
from jax.experimental import pallas as pl
from jax.experimental.pallas import tpu as pltpu

F32 = jnp.float32
BF16 = jnp.bfloat16

D_MODEL = 1024
D_FF = 2816
N_MOD = 9
HEAD_DIM = 64
RMS_EPS = 1e-6
GN_EPS = 64e-5
RWKV_WIDTH = 512
RWKV_PROJ = 1792
S5_WIDTH = 512
S5_GROUP_CH = 16
S5_GROUPS = 32
S5_STATE = 64
ATTN_HEADS = 16
ATTN_KV_HEADS = 4
ATTN_BLOCK = 128
ATTN_WINDOW = 128
ATTN_Q_W = 1024
ATTN_KV_W = 256
ROPE_BASE = 10000.0
NEG_INF = -1e30

ROW_TILE = 256
FFN_ROW_TILE = 512
FFN_F_CHUNKS = (1280, 1536)
RWKV_CHUNK = 64
S5_STEPS = 64
VMEM_LIMIT = 56 * 1024 * 1024

_NN = (((1,), (0,)), ((), ()))
_NT = (((1,), (1,)), ((), ()))


def _dot(a, b, dims=_NN):
    return lax.dot_general(a, b, dims, preferred_element_type=F32)


def _split2(a):
    hi = a.astype(BF16)
    lo = (a - hi.astype(F32)).astype(BF16)
    return hi, lo


def _split3(a):
    hi = a.astype(BF16)
    r1 = a - hi.astype(F32)
    mid = r1.astype(BF16)
    lo = (r1 - mid.astype(F32)).astype(BF16)
    return hi, mid, lo


def _mm1(a, b, dims=_NN):
    return _dot(a.astype(BF16), b.astype(BF16), dims)


def _mm3(a, b, dims=_NN):
    ah, al = _split2(a)
    bh, bl = _split2(b)
    return _dot(ah, bh, dims) + (_dot(ah, bl, dims) + _dot(al, bh, dims))


def _mm_const_rhs(a, c_bf16, dims=_NN):
    hi, mid, lo = _split3(a)
    return _dot(hi, c_bf16, dims) + (_dot(mid, c_bf16, dims) + _dot(lo, c_bf16, dims))


def _mm_const_lhs(c_bf16, a, dims=_NN):
    hi, mid, lo = _split3(a)
    return _dot(c_bf16, hi, dims) + (_dot(c_bf16, mid, dims) + _dot(c_bf16, lo, dims))


def _sigmoid(x):
    return 1.0 / (1.0 + jnp.exp(-x))


def _silu(x):
    return x * _sigmoid(x)


def _params(*sem):
    return pltpu.CompilerParams(dimension_semantics=sem, vmem_limit_bytes=VMEM_LIMIT)


def _modnorm(x, g, shift, scale):
    xn = x * lax.rsqrt(jnp.mean(x * x, axis=-1, keepdims=True) + RMS_EPS)
    return (xn * g) * (1.0 + scale) + shift


def _head_ones(width):
    r = lax.broadcasted_iota(jnp.int32, (width, width), 0) >> 6
    c = lax.broadcasted_iota(jnp.int32, (width, width), 1) >> 6
    return (r == c).astype(BF16)


def _mod_kernel(c_ref, w_ref, b_ref, o_ref):
    o_ref[...] = _mm3(c_ref[...], w_ref[...]) + b_ref[...]


def _mod_vectors(cs, mod_w, mod_b):
    depth = mod_w.shape[0]
    n = mod_w.shape[2]
    tn = 1152
    return pl.pallas_call(
        _mod_kernel,
        grid=(depth, n // tn),
        in_specs=[pl.BlockSpec((8, D_MODEL), lambda l, j: (0, 0)),
                  pl.BlockSpec((None, D_MODEL, tn), lambda l, j: (l, 0, j)),
                  pl.BlockSpec((None, 1, tn), lambda l, j: (l, 0, j))],
        out_specs=pl.BlockSpec((None, 8, tn), lambda l, j: (l, 0, j)),
        out_shape=jax.ShapeDtypeStruct((depth, 8, n), F32),
        compiler_params=_params("parallel", "parallel"),
        name="mod_vectors",
    )(cs, mod_w, mod_b.reshape(depth, 1, n))


def _ffn_kernel(x_ref, m_ref, g_ref, w1_ref, w2_ref, fg_ref, o_ref, *, mod0, final):
    x = x_ref[...]
    h = _modnorm(x, g_ref[...], m_ref[mod0:mod0 + 1, :], m_ref[mod0 + 1:mod0 + 2, :]).astype(BF16)
    acc = None
    f0 = 0
    for fc in FFN_F_CHUNKS:
        gate = _dot(h, w1_ref[:, f0:f0 + fc])
        up = _dot(h, w1_ref[:, D_FF + f0:D_FF + f0 + fc])
        part = _dot((_silu(gate) * up).astype(BF16), w2_ref[f0:f0 + fc, :])
        acc = part if acc is None else acc + part
        f0 += fc
    y = x + (0.5 * m_ref[mod0 + 2:mod0 + 3, :]) * acc
    if final:
        y = y * lax.rsqrt(jnp.mean(y * y, axis=-1, keepdims=True) + RMS_EPS) * fg_ref[...]
    o_ref[...] = y


def _ffn(x, modrows, g, w1, w2, final_g, *, mod0, row_tile_of_mod, final=False):
    rows = x.shape[0]
    tm = FFN_ROW_TILE
    step = tm // ROW_TILE
    once = pl.Buffered(1)
    return pl.pallas_call(
        functools.partial(_ffn_kernel, mod0=mod0, final=final),
        grid=(rows // tm,),
        in_specs=[pl.BlockSpec((tm, D_MODEL), lambda i: (i, 0)),
                  pl.BlockSpec((None, N_MOD, D_MODEL), lambda i: (row_tile_of_mod + i * step, 0, 0)),
                  pl.BlockSpec((1, D_MODEL), lambda i: (0, 0)),
                  pl.BlockSpec(w1.shape, lambda i: (0, 0), pipeline_mode=once),
                  pl.BlockSpec(w2.shape, lambda i: (0, 0), pipeline_mode=once),
                  pl.BlockSpec((1, D_MODEL), lambda i: (0, 0))],
        out_specs=pl.BlockSpec((tm, D_MODEL), lambda i: (i, 0)),
        out_shape=jax.ShapeDtypeStruct((rows, D_MODEL), F32),
        compiler_params=_params("parallel"),
        name="ffn",
    )(x, modrows, g.reshape(1, D_MODEL), w1, w2, final_g.reshape(1, D_MODEL))


def _x_tile(bsz, n_lat_tiles):
    return lambda b, k: (jnp.where(k == 0, b, bsz + b * n_lat_tiles + k - 1), 0)


def _ab_in_kernel(x_ref, m_ref, g_ref, w_ref, pa_ref, pb_ref):
    h = _modnorm(x_ref[...], g_ref[...], m_ref[3:4, :], m_ref[4:5, :]).astype(BF16)
    p = _dot(h, w_ref[...])
    pa_ref[...] = p[:, :RWKV_PROJ]
    pb_ref[...] = p[:, RWKV_PROJ:]


def _ab_in_proj(x, modrows, g, w, bsz, n_lat_tiles):
    nt = n_lat_tiles + 1
    seq = nt * ROW_TILE
    xt = _x_tile(bsz, n_lat_tiles)
    return pl.pallas_call(
        _ab_in_kernel,
        grid=(bsz, nt),
        in_specs=[pl.BlockSpec((ROW_TILE, D_MODEL), xt),
                  pl.BlockSpec((None, N_MOD, D_MODEL), lambda b, k: (xt(b, k)[0], 0, 0)),
                  pl.BlockSpec((1, D_MODEL), lambda b, k: (0, 0)),
                  pl.BlockSpec(w.shape, lambda b, k: (0, 0))],
        out_specs=[pl.BlockSpec((None, ROW_TILE, RWKV_PROJ), lambda b, k: (b, k, 0)),
                   pl.BlockSpec((None, ROW_TILE, S5_WIDTH), lambda b, k: (b, k, 0))],
        out_shape=[jax.ShapeDtypeStruct((bsz, seq, RWKV_PROJ), F32),
                   jax.ShapeDtypeStruct((bsz, seq, S5_WIDTH), F32)],
        compiler_params=_params("parallel", "parallel"),
        name="ab_in_proj",
    )(x, modrows, g.reshape(1, D_MODEL), w)


def _rwkv_prep_kernel(cur_ref, prev_ref, next_ref, mu_ref, wl_ref, vec_ref,
                      r_ref, v_ref, a_ref, k_ref, b_ref, lw_ref, g_ref, bonus_ref):
    k_idx = pl.program_id(1)
    nt = pl.num_programs(1)
    cur = cur_ref[...]
    rows = cur.shape[0]
    row = lax.broadcasted_iota(jnp.int32, cur.shape, 0)
    use_prev = k_idx >= 2
    use_next = jnp.logical_and(k_idx >= 1, k_idx < nt - 1)
    prev_row = jnp.where(use_prev, prev_ref[7:8, :], 0.0)
    next_row = jnp.where(use_next, next_ref[0:1, :], 0.0)
    prev = jnp.where(row == 0, prev_row, pltpu.roll(cur, 1, axis=0))
    nxt = jnp.where(row == rows - 1, next_row, pltpu.roll(cur, rows - 1, axis=0))
    p = cur + mu_ref[0:1, :] * (prev - cur) + mu_ref[1:2, :] * (nxt - cur)

    W = RWKV_WIDTH
    r, k, v = p[:, :W], p[:, W:2 * W], p[:, 2 * W:3 * W]
    lora_in = p[:, 3 * W:]
    lane = lax.broadcasted_iota(jnp.int32, lora_in.shape, 1)
    lora_act = jnp.where(lane < 64, jnp.tanh(lora_in), jnp.where(lane < 128, lora_in, _sigmoid(lora_in)))
    lo = _mm3(lora_act, wl_ref[...])

    k_k, k_a, r_k = vec_ref[0:1, :], vec_ref[1:2, :], vec_ref[2:3, :]
    ones = _head_ones(W)
    kk = k * k_k
    kk = kk * lax.rsqrt(jnp.maximum(_mm_const_rhs(kk * kk, ones), 1e-12))
    r_ref[...] = r
    v_ref[...] = v
    a_ref[...] = -kk
    k_sum = jnp.zeros_like(k)
    for d in range(2):
        z = -(vec_ref[3 + d:4 + d, :] + lo[:, d * W:(d + 1) * W])
        softplus = jnp.maximum(z, 0.0) + jnp.log(1.0 + jnp.exp(-jnp.abs(z)))
        lw_ref[d] = -jnp.exp(-softplus - 0.5)
        a = _sigmoid(vec_ref[5 + d:6 + d, :] + lo[:, (2 + d) * W:(3 + d) * W])
        kd = k * (1.0 + (a - 1.0) * k_a)
        k_ref[d] = kd
        b_ref[d] = kk * a
        k_sum = k_sum + kd
    g_ref[...] = lo[:, 4 * W:]
    bonus_ref[...] = _mm_const_rhs(r * (0.5 * k_sum) * r_k, ones) * v


def _rwkv_prep(pa, mu, w_lora, vecs):
    bsz, seq, _ = pa.shape
    nt = seq // ROW_TILE
    sub = ROW_TILE // 8
    last_blk = seq // 8 - 1
    W = RWKV_WIDTH
    one = jax.ShapeDtypeStruct((bsz, seq, W), F32)
    two = jax.ShapeDtypeStruct((2, bsz, seq, W), F32)
    spec1 = pl.BlockSpec((None, ROW_TILE, W), lambda b, k: (b, k, 0))
    spec2 = pl.BlockSpec((2, None, ROW_TILE, W), lambda b, k: (0, b, k, 0))
    return pl.pallas_call(
        _rwkv_prep_kernel,
        grid=(bsz, nt),
        in_specs=[pl.BlockSpec((None, ROW_TILE, RWKV_PROJ), lambda b, k: (b, k, 0)),
                  pl.BlockSpec((None, 8, RWKV_PROJ), lambda b, k: (b, jnp.maximum(k * sub - 1, 0), 0)),
                  pl.BlockSpec((None, 8, RWKV_PROJ), lambda b, k: (b, jnp.minimum((k + 1) * sub, last_blk), 0)),
                  pl.BlockSpec(mu.shape, lambda b, k: (0, 0)),
                  pl.BlockSpec(w_lora.shape, lambda b, k: (0, 0)),
                  pl.BlockSpec(vecs.shape, lambda b, k: (0, 0))],
        out_specs=[spec1, spec1, spec1, spec2, spec2, spec2, spec1, spec1],
        out_shape=[one, one, one, two, two, two, one, one],
        compiler_params=_params("parallel", "parallel"),
        name="rwkv_prep",
    )(pa, pa, pa, mu, w_lora, vecs)


RWKV_TILE_HEADS = 2
RWKV_TILE = RWKV_TILE_HEADS * HEAD_DIM


def _rwkv_scaled(fwd, r, v, a, k, b, lw):
    C = RWKV_CHUNK
    rr = lax.broadcasted_iota(jnp.int32, (C, C), 0)
    cc = lax.broadcasted_iota(jnp.int32, (C, C), 1)
    tri = ((rr >= cc) if fwd else (rr <= cc)).astype(BF16)
    hi, mid, lo = _split3(lw)
    cum = _dot(jnp.concatenate([tri, tri, tri], axis=1), jnp.concatenate([hi, mid, lo], axis=0))
    last = cum[C - 1:C, :] if fwd else cum[0:1, :]
    einv = jnp.exp(-cum)
    to_end = jnp.exp(last - cum)
    return dict(at=a * jnp.exp(cum - lw), rt=r * jnp.exp(cum), bt=b * einv, kt=k * einv,
                b_end=b * to_end, k_end=k * to_end, p_end=jnp.exp(last), v=v)


def _rwkv_chunks(dirs, h_ref, y_refs):
    C = RWKV_CHUNK
    TW = RWKV_TILE
    tq = lax.broadcasted_iota(jnp.int32, (C, TW), 0)
    sq = lax.broadcasted_iota(jnp.int32, (C, TW), 1) & (C - 1)
    eye_cat = (tq == sq).astype(F32)
    rb = lax.broadcasted_iota(jnp.int32, (TW, TW), 0)
    cb = lax.broadcasted_iota(jnp.int32, (TW, TW), 1)
    same_head = (rb >> 6) == (cb >> 6)
    diag = rb == cb

    def stack(x):
        return jnp.where(same_head, jnp.concatenate([x] * RWKV_TILE_HEADS, axis=0), jnp.zeros((), x.dtype))

    chains = []
    for di, (fwd, op) in enumerate(dirs):
        strict = (tq > sq) if fwd else (tq < sq)
        incl = (tq >= sq) if fwd else (tq <= sq)
        for q in range(RWKV_WIDTH // TW):
            sl = slice(TW * q, TW * (q + 1))
            ch = {name: val[:, sl] for name, val in op.items()}
            ch.update(strict=strict, incl=incl, di=di, q=q, sl=sl)
            chains.append(ch)

    for ch in chains:
        lhs = jnp.concatenate([ch["at"], ch["rt"]], axis=0).astype(BF16)
        rhs = jnp.concatenate([stack(ch["bt"].astype(BF16)), stack(ch["kt"].astype(BF16))], axis=0)
        s_all = _dot(lhs, rhs, _NT)
        ch["n_ab"] = jnp.where(ch["strict"], s_all[:C, :TW], 0.0)
        ch["n_ak"] = jnp.where(ch["strict"], s_all[:C, TW:], 0.0)
        ch["n_rb"] = jnp.where(ch["incl"], s_all[C:, :TW], 0.0)
        ch["n_rk"] = jnp.where(ch["incl"], s_all[C:, TW:], 0.0)
        ch["tinv"] = eye_cat + ch["n_ab"]
        ch["pw"] = ch["n_ab"]

    levels = int(math.log2(C))
    for i in range(levels):
        for ch in chains:
            pw_b = ch["pw"].astype(BF16)
            if i == 0:
                ch["pw"] = _dot(pw_b, stack(pw_b))
                v_s = stack(ch["v"].astype(BF16))
                ch["kv"] = _dot(jnp.concatenate([ch["n_ak"], ch["n_rk"]], axis=0).astype(BF16), v_s)
            elif i == levels - 1:
                ch["tinv"] = ch["tinv"] + _dot(pw_b, stack(ch["tinv"].astype(BF16)))
            else:
                res = _dot(pw_b, jnp.concatenate([stack(pw_b), stack(ch["tinv"].astype(BF16))], axis=1))
                ch["pw"] = res[:, :TW]
                ch["tinv"] = ch["tinv"] + res[:, TW:]

    for ch in chains:
        nh, nl = _split2(ch["n_ab"])
        xh, xl = _split2(ch["tinv"])
        ch["xh"] = xh
        ch["resid"] = (eye_cat - ch["tinv"]) + _dot(jnp.concatenate([nh, nh, nl], axis=1),
                                                     jnp.concatenate([stack(xh), stack(xl), stack(xh)], axis=0))
    for ch in chains:
        ch["tinv"] = ch["tinv"] + _dot(ch["xh"], stack(ch["resid"].astype(BF16)))

    for ch in chains:
        rhs = jnp.concatenate([stack(ch["at"].astype(BF16)), stack(ch["kv"][:C].astype(BF16))], axis=1)
        ch["aw"] = _dot(ch["tinv"].astype(BF16), rhs).astype(BF16)

    for ch in chains:
        aw = ch["aw"]
        c1 = _dot(ch["n_rb"].astype(BF16), jnp.concatenate([stack(aw[:, :TW]), stack(aw[:, TW:])], axis=1))
        ch["qm"] = ch["rt"] + c1[:, :TW]
        ch["z"] = c1[:, TW:] + ch["kv"][C:]
        bk_t = jnp.concatenate([ch["b_end"], ch["k_end"]], axis=0).T.astype(BF16)
        low = jnp.concatenate([jnp.zeros((C, TW), BF16), ch["v"].astype(BF16)], axis=1)
        c2 = _dot(bk_t, jnp.concatenate([aw, low], axis=0))
        ch["m_upd"] = jnp.where(same_head, c2[:, :TW], 0.0)
        ch["g_upd"] = jnp.where(same_head, c2[:, TW:], 0.0)

    for ch in chains:
        h = h_ref[ch["di"], ch["q"]]
        d = _dot(jnp.concatenate([ch["m_upd"], ch["qm"]], axis=0).astype(BF16), h.astype(BF16))
        y_refs[ch["di"]][:, ch["sl"]] = d[TW:] + ch["z"]
        p_col = jnp.sum(jnp.where(diag, ch["p_end"], 0.0), axis=1, keepdims=True)
        h_ref[ch["di"], ch["q"]] = p_col * h + d[:TW] + ch["g_upd"]


def _rwkv_scan_kernel(rf_ref, vf_ref, af_ref, rb_ref, vb_ref, ab_ref, kf_ref, bf_ref, lwf_ref,
                      kb_ref, bb_ref, lwb_ref, yf_ref, yb_ref, h_ref):
    @pl.when(pl.program_id(1) == 0)
    def _():
        h_ref[...] = jnp.zeros_like(h_ref)

    fwd = _rwkv_scaled(True, rf_ref[...], vf_ref[...], af_ref[...], kf_ref[...], bf_ref[...], lwf_ref[...])
    bwd = _rwkv_scaled(False, rb_ref[...], vb_ref[...], ab_ref[...], kb_ref[...], bb_ref[...], lwb_ref[...])
    _rwkv_chunks([(True, fwd), (False, bwd)], h_ref, [yf_ref, yb_ref])


def _rwkv_scan(r, v, a, k, b, lw, n_ctx_chunks):
    bsz, seq, W = r.shape
    C = RWKV_CHUNK
    nch = seq // C

    def back(j):
        return jnp.where(j < n_ctx_chunks, n_ctx_chunks - 1 - j, nch - 1 - (j - n_ctx_chunks))

    one_f = pl.BlockSpec((None, C, W), lambda g, j: (g, j, 0))
    one_b = pl.BlockSpec((None, C, W), lambda g, j: (g, back(j), 0))
    two_f = pl.BlockSpec((None, None, C, W), lambda g, j: (0, g, j, 0))
    two_b = pl.BlockSpec((None, None, C, W), lambda g, j: (1, g, back(j), 0))
    out = jax.ShapeDtypeStruct((bsz, seq, W), F32)
    return pl.pallas_call(
        _rwkv_scan_kernel,
        grid=(bsz, nch),
        in_specs=[one_f, one_f, one_f, one_b, one_b, one_b, two_f, two_f, two_f, two_b, two_b, two_b],
        out_specs=[one_f, one_b],
        out_shape=[out, out],
        scratch_shapes=[pltpu.VMEM((2, W // RWKV_TILE, RWKV_TILE, RWKV_TILE), F32)],
        compiler_params=_params("parallel", "arbitrary"),
        name="rwkv_scan",
    )(r, v, a, r, v, a, k, b, lw, k, b, lw)


def _s5_scan_kernel(uf_ref, ub_ref, lam_ref, bmat_ref, cmat_ref, yf_ref, yb_ref, bu_ref, st_ref, *, bsz):
    S = S5_STEPS
    half = 8 * S5_STATE
    n_nat = 2 * bsz * S

    @pl.when(pl.program_id(0) == 0)
    def _():
        st_ref[...] = jnp.zeros_like(st_ref)

    def perm_mask(rho, col):
        t, s = rho >> 3, rho & 7
        src = jnp.where(s < 4, s * S + t, bsz * S + (s - 4) * S + (S - 1 - t))
        return jnp.logical_and((s & 3) < bsz, col == src).astype(BF16)

    perm = perm_mask(lax.broadcasted_iota(jnp.int32, (S * 8, n_nat), 0),
                     lax.broadcasted_iota(jnp.int32, (S * 8, n_nat), 1))
    perm_t = perm_mask(lax.broadcasted_iota(jnp.int32, (n_nat, S * 8), 1),
                       lax.broadcasted_iota(jnp.int32, (n_nat, S * 8), 0))
    u_nat = jnp.concatenate([uf_ref[...].reshape(bsz * S, S5_WIDTH), ub_ref[...].reshape(bsz * S, S5_WIDTH)], axis=0)
    u = _dot(perm, u_nat.astype(BF16))
    is_fwd = (lax.broadcasted_iota(jnp.int32, (S * 8, 128), 0) & 7) < 4
    for q in range(S5_WIDTH // 128):
        uq = u[:, 128 * q:128 * (q + 1)]
        lhs = jnp.concatenate([jnp.where(is_fwd, uq, 0.0), jnp.where(is_fwd, 0.0, uq)], axis=1)
        bu_ref[:, 2 * half * q:2 * half * (q + 1)] = _dot(lhs.astype(BF16), bmat_ref[q])

    nq = S5_WIDTH // 128
    lam = lam_ref[...]

    def step(t, carry):
        rows = pl.ds(pl.multiple_of(t * 8, 8), 8)
        new = []
        for q in range(nq):
            base = 2 * half * q
            ar, ai = lam[:, base:base + half], lam[:, base + half:base + 2 * half]
            xr, xi = carry[2 * q], carry[2 * q + 1]
            nr = ar * xr - ai * xi + bu_ref[rows, base:base + half]
            ni = ar * xi + ai * xr + bu_ref[rows, base + half:base + 2 * half]
            bu_ref[rows, base:base + half] = nr
            bu_ref[rows, base + half:base + 2 * half] = ni
            new += [nr, ni]
        return tuple(new)

    init = tuple(st_ref[:, half * j:half * (j + 1)] for j in range(2 * nq))
    final = lax.fori_loop(0, S, step, init, unroll=2)
    for j in range(2 * nq):
        st_ref[:, half * j:half * (j + 1)] = final[j]

    for q in range(S5_WIDTH // 128):
        xq = bu_ref[:, 2 * half * q:2 * half * (q + 1)]
        y2 = _dot(xq.astype(BF16), cmat_ref[q])
        hi, lo = _split2(jnp.where(is_fwd, y2[:, :128], y2[:, 128:]))
        y_nat = _dot(perm_t, hi) + _dot(perm_t, lo)
        yf_ref[:, :, 128 * q:128 * (q + 1)] = y_nat[:bsz * S].reshape(bsz, S, 128)
        yb_ref[:, :, 128 * q:128 * (q + 1)] = y_nat[bsz * S:].reshape(bsz, S, 128)


def _s5_scan(u, lam, bmat, cmat, n_ctx_blocks):
    bsz, seq, _ = u.shape
    S = S5_STEPS
    nblk = seq // S
    nstate = 2 * S5_GROUPS * S5_STATE

    def back(i):
        return jnp.where(i < n_ctx_blocks, n_ctx_blocks - 1 - i, nblk - 1 - (i - n_ctx_blocks))

    spec_f = pl.BlockSpec((bsz, S, S5_WIDTH), lambda i: (0, i, 0))
    spec_b = pl.BlockSpec((bsz, S, S5_WIDTH), lambda i: (0, back(i), 0))
    out = jax.ShapeDtypeStruct(u.shape, F32)
    return pl.pallas_call(
        functools.partial(_s5_scan_kernel, bsz=bsz),
        grid=(nblk,),
        in_specs=[spec_f, spec_b,
                  pl.BlockSpec(lam.shape, lambda i: (0, 0)),
                  pl.BlockSpec(bmat.shape, lambda i: (0, 0, 0)),
                  pl.BlockSpec(cmat.shape, lambda i: (0, 0, 0))],
        out_specs=[spec_f, spec_b],
        out_shape=[out, out],
        scratch_shapes=[pltpu.VMEM((S * 8, nstate), F32), pltpu.VMEM((8, nstate), F32)],
        compiler_params=_params("arbitrary"),
        name="s5_scan",
    )(u, u, lam, bmat, cmat)


def _s5_discretise(a_re, a_im, log_step, b_re, b_im, c_re, c_im):
    lam_re = jnp.minimum(a_re, -1e-4)
    lam_im = a_im
    dt = jnp.exp(log_step)[..., None]
    mag = jnp.exp(lam_re * dt)
    ab_re, ab_im = mag * jnp.cos(lam_im * dt), mag * jnp.sin(lam_im * dt)
    den = lam_re * lam_re + lam_im * lam_im
    f_re = ((ab_re - 1.0) * lam_re + ab_im * lam_im) / den
    f_im = (ab_im * lam_re - (ab_re - 1.0) * lam_im) / den
    bb_re = f_re[..., None] * b_re - f_im[..., None] * b_im
    bb_im = f_re[..., None] * b_im + f_im[..., None] * b_re
    nq = S5_WIDTH // 128
    eye8 = jnp.eye(8, dtype=F32)

    def lanes(t):
        return t.reshape(2, nq, 8 * S5_STATE)

    lam = jnp.concatenate([lanes(ab_re), lanes(ab_im)], axis=-1).reshape(2, nq * 16 * S5_STATE)
    lam = jnp.repeat(lam, 4, axis=0)

    def in_block(t):
        t = t.reshape(2, nq, 8, S5_STATE, S5_GROUP_CH)
        return jnp.einsum('dqgpi,gh->dqgihp', t, eye8).reshape(2, nq, 128, 8 * S5_STATE)

    bmat = jnp.concatenate([in_block(bb_re), in_block(bb_im)], axis=-1)
    bmat = jnp.concatenate([bmat[0], bmat[1]], axis=1)

    def out_block(t):
        t = t.reshape(2, nq, 8, S5_GROUP_CH, S5_STATE)
        return jnp.einsum('dqgip,gh->dqgphi', t, eye8).reshape(2, nq, 8 * S5_STATE, 128)

    cmat = jnp.concatenate([out_block(c_re), -out_block(c_im)], axis=2)
    cmat = jnp.concatenate([cmat[0], cmat[1]], axis=-1)
    return lam, bmat.astype(BF16), cmat.astype(BF16)


def _ab_out_kernel(x_ref, m_ref, yf_ref, yb_ref, g_ref, bonus_ref, sf_ref, sb_ref, u_ref, vec_ref, gluw_ref, w_ref,
                   o_ref):
    W = RWKV_WIDTH
    ones = _head_ones(W)
    y = yf_ref[...] + yb_ref[...]
    mu = _mm_const_rhs(y, ones) * (1.0 / HEAD_DIM)
    yc = y - mu
    var = _mm_const_rhs(yc * yc, ones) * (1.0 / HEAD_DIM)
    ya = (yc * lax.rsqrt(var + GN_EPS) * vec_ref[0:1, :] + vec_ref[1:2, :] + bonus_ref[...]) * g_ref[...]

    s = (sf_ref[...] + sb_ref[...]) + vec_ref[2:3, :] * u_ref[...]
    z = 0.5 * s * (1.0 + jnp.tanh(math.sqrt(2.0 / math.pi) * (s + 0.044715 * (s * s * s))))
    yb = z * _sigmoid(_mm1(z, gluw_ref[...]) + vec_ref[3:4, :])
    out = _dot(ya.astype(BF16), w_ref[:W, :]) + _dot(yb.astype(BF16), w_ref[W:, :])
    o_ref[...] = x_ref[...] + m_ref[5:6, :] * out


def _ab_out(x, modrows, yf, yb, g, bonus, sf, sb, u, vecs, glu_w, out_w, bsz, n_lat_tiles):
    nt = n_lat_tiles + 1
    W = RWKV_WIDTH
    xt = _x_tile(bsz, n_lat_tiles)
    seq_spec = pl.BlockSpec((None, ROW_TILE, W), lambda b, k: (b, k, 0))
    return pl.pallas_call(
        _ab_out_kernel,
        grid=(bsz, nt),
        in_specs=[pl.BlockSpec((ROW_TILE, D_MODEL), xt),
                  pl.BlockSpec((None, N_MOD, D_MODEL), lambda b, k: (xt(b, k)[0], 0, 0)),
                  seq_spec, seq_spec, seq_spec, seq_spec, seq_spec, seq_spec, seq_spec,
                  pl.BlockSpec(vecs.shape, lambda b, k: (0, 0)),
                  pl.BlockSpec(glu_w.shape, lambda b, k: (0, 0)),
                  pl.BlockSpec(out_w.shape, lambda b, k: (0, 0))],
        out_specs=pl.BlockSpec((ROW_TILE, D_MODEL), xt),
        out_shape=jax.ShapeDtypeStruct(x.shape, F32),
        compiler_params=_params("parallel", "parallel"),
        name="ab_out",
    )(x, modrows, yf, yb, g, bonus, sf, sb, u, vecs, glu_w, out_w)


def _attn_in_kernel(x_ref, m_ref, g_ref, w_ref, cos_ref, sin_ref, o_ref):
    h = _modnorm(x_ref[...], g_ref[...], m_ref[3:4, :], m_ref[4:5, :]).astype(BF16)
    p = _dot(h, w_ref[...])
    qk_w = ATTN_Q_W + ATTN_KV_W
    qk = p[:, :qk_w]
    half = HEAD_DIM // 2
    lane = lax.broadcasted_iota(jnp.int32, qk.shape, 1)
    first = (lane & (HEAD_DIM - 1)) < half
    partner = jnp.where(first, pltpu.roll(qk, qk_w - half, axis=1), pltpu.roll(qk, half, axis=1))
    reps = qk_w // 128
    cos = jnp.concatenate([cos_ref[...]] * reps, axis=1)
    sin = jnp.concatenate([sin_ref[...]] * reps, axis=1)
    o_ref[:, :qk_w] = qk * cos + partner * sin
    o_ref[:, qk_w:] = p[:, qk_w:]


def _attn_in_proj(x, modrows, g, w, cos_t, sin_t, n_ctx_tiles, n_lat_tiles):
    rows = x.shape[0]
    width = w.shape[1]

    def rope_tile(i):
        return (jnp.where(i < n_ctx_tiles, 0, 1 + (i - n_ctx_tiles) % n_lat_tiles), 0)

    return pl.pallas_call(
        _attn_in_kernel,
        grid=(rows // ROW_TILE,),
        in_specs=[pl.BlockSpec((ROW_TILE, D_MODEL), lambda i: (i, 0)),
                  pl.BlockSpec((None, N_MOD, D_MODEL), lambda i: (i, 0, 0)),
                  pl.BlockSpec((1, D_MODEL), lambda i: (0, 0)),
                  pl.BlockSpec(w.shape, lambda i: (0, 0)),
                  pl.BlockSpec((ROW_TILE, 128), rope_tile),
                  pl.BlockSpec((ROW_TILE, 128), rope_tile)],
        out_specs=pl.BlockSpec((ROW_TILE, width), lambda i: (i, 0)),
        out_shape=jax.ShapeDtypeStruct((rows, width), F32),
        compiler_params=_params("parallel"),
        name="attn_in_proj",
    )(x, modrows, g.reshape(1, D_MODEL), w, cos_t, sin_t)


def _rope_tables(n_lat, grid_w):
    half = HEAD_DIM // 2
    t = jnp.arange(n_lat)
    row_id = (t // grid_w).astype(F32)
    col_id = (t % grid_w).astype(F32)
    inv_freq = ROPE_BASE ** (-jnp.arange(0, half, 2, dtype=F32) / half)
    ang = jnp.concatenate([row_id[:, None] * inv_freq, col_id[:, None] * inv_freq], axis=-1)
    cos, sin = jnp.cos(ang), jnp.sin(ang)
    cos_t = jnp.concatenate([cos, cos, cos, cos], axis=-1)
    sin_t = jnp.concatenate([-sin, sin, -sin, sin], axis=-1)
    cos_t = jnp.concatenate([jnp.ones((ROW_TILE, 128), F32), cos_t], axis=0)
    sin_t = jnp.concatenate([jnp.zeros((ROW_TILE, 128), F32), sin_t], axis=0)
    return cos_t, sin_t


def _attn_kernel(sink_ref, q_ref, kp_ref, kc_ref, kn_ref, vp_ref, vc_ref, vn_ref, kx_ref, vx_ref, o_ref, *, nblk):
    i = pl.program_id(1)
    Q = ATTN_BLOCK
    G = ATTN_HEADS // ATTN_KV_HEADS
    k_all = jnp.concatenate([kp_ref[...], kc_ref[...], kn_ref[...], kx_ref[...]], axis=0).astype(BF16)
    v_all = jnp.concatenate([vp_ref[...], vc_ref[...], vn_ref[...], vx_ref[...]], axis=0).astype(BF16)
    nk = k_all.shape[0]
    qi = lax.broadcasted_iota(jnp.int32, (G * Q, nk), 0) & (Q - 1)
    cj = lax.broadcasted_iota(jnp.int32, (G * Q, nk), 1)
    mj = cj - Q
    blk = jnp.where(cj < Q, i - 1, jnp.where(cj < 2 * Q, i, i + 1))
    valid = jnp.logical_and(jnp.abs(mj - qi) <= ATTN_WINDOW, jnp.logical_and(blk >= 0, blk < nblk))
    valid = jnp.logical_or(valid, cj >= 3 * Q)
    q = q_ref[...] * (HEAD_DIM ** -0.5)
    row_head = lax.broadcasted_iota(jnp.int32, (G * Q, 1), 0) >> 7
    heads = range(ATTN_KV_HEADS)
    scores = []
    for kh in heads:
        q4 = jnp.concatenate([q[:, (kh * G + g) * HEAD_DIM:(kh * G + g + 1) * HEAD_DIM] for g in range(G)], axis=0)
        scores.append(_dot(q4.astype(BF16), k_all[:, kh * HEAD_DIM:(kh + 1) * HEAD_DIM], _NT))
    probs, dens = [], []
    for kh in heads:
        s = jnp.where(valid, scores[kh], NEG_INF)
        sink = jnp.zeros((G * Q, 1), F32)
        for g in range(G):
            sink = jnp.where(row_head == g, sink_ref[kh * G + g], sink)
        m = jnp.maximum(jnp.max(s, axis=-1, keepdims=True), sink)
        pr = jnp.exp(s - m)
        dens.append(jnp.sum(pr, axis=-1, keepdims=True) + jnp.exp(sink - m))
        probs.append(pr.astype(BF16))
    for kh in heads:
        o = _dot(probs[kh], v_all[:, kh * HEAD_DIM:(kh + 1) * HEAD_DIM]) / dens[kh]
        for g in range(G):
            hq = kh * G + g
            o_ref[:, hq * HEAD_DIM:(hq + 1) * HEAD_DIM] = o[g * Q:(g + 1) * Q]


def _attention(qkv, sink, bsz, n_lat, n_ctx):
    Q = ATTN_BLOCK
    nblk = n_lat // Q
    lat0 = bsz * n_ctx // Q
    kcol, vcol = ATTN_Q_W // ATTN_KV_W, ATTN_Q_W // ATTN_KV_W + 1

    def kv_spec(col, off):
        return pl.BlockSpec((Q, ATTN_KV_W),
                            lambda b, i: (lat0 + b * nblk + jnp.clip(i + off, 0, nblk - 1), col))

    def ctx_spec(col):
        return pl.BlockSpec((n_ctx, ATTN_KV_W), lambda b, i: (b, col))

    return pl.pallas_call(
        functools.partial(_attn_kernel, nblk=nblk),
        grid=(bsz, nblk),
        in_specs=[pl.BlockSpec(memory_space=pltpu.SMEM),
                  pl.BlockSpec((Q, ATTN_Q_W), lambda b, i: (lat0 + b * nblk + i, 0)),
                  kv_spec(kcol, -1), kv_spec(kcol, 0), kv_spec(kcol, 1),
                  kv_spec(vcol, -1), kv_spec(vcol, 0), kv_spec(vcol, 1),
                  ctx_spec(kcol), ctx_spec(vcol)],
        out_specs=pl.BlockSpec((Q, ATTN_Q_W), lambda b, i: (b * nblk + i, 0)),
        out_shape=jax.ShapeDtypeStruct((bsz * n_lat, ATTN_Q_W), F32),
        compiler_params=_params("parallel", "parallel"),
        name="window_attention",
    )(sink, qkv, qkv, qkv, qkv, qkv, qkv, qkv, qkv, qkv)


def _attn_out_kernel(x_ref, m_ref, o_ref_in, w_ref, out_ref):
    out_ref[...] = x_ref[...] + m_ref[5:6, :] * _dot(o_ref_in[...].astype(BF16), w_ref[...])


def _attn_out(x, modrows, o, w, n_ctx_tiles):
    rows = o.shape[0]
    return pl.pallas_call(
        _attn_out_kernel,
        grid=(rows // ROW_TILE,),
        in_specs=[pl.BlockSpec((ROW_TILE, D_MODEL), lambda i: (n_ctx_tiles + i, 0)),
                  pl.BlockSpec((None, N_MOD, D_MODEL), lambda i: (n_ctx_tiles + i, 0, 0)),
                  pl.BlockSpec((ROW_TILE, D_MODEL), lambda i: (i, 0)),
                  pl.BlockSpec(w.shape, lambda i: (0, 0))],
        out_specs=pl.BlockSpec((ROW_TILE, D_MODEL), lambda i: (i, 0)),
        out_shape=jax.ShapeDtypeStruct((rows, D_MODEL), F32),
        compiler_params=_params("parallel"),
        name="attn_out",
    )(x, modrows, o, w)


def kernel(x, c, ctx, c_ctx, norm_g, mod_w, mod_b, ffn_w1, ffn_w2, ab_in_w, ab_out_w, rwkv_mu, rwkv_w0, rwkv_w2, rwkv_a0, rwkv_a2, rwkv_g2, rwkv_k_k, rwkv_k_a, rwkv_r_k, rwkv_lnx_g, rwkv_lnx_b, s5_a_re, s5_a_im, s5_log_step, s5_b_re, s5_b_im, s5_c_re, s5_c_im, s5_d, s5_glu_w, s5_glu_b, attn_in_w, attn_out_w, attn_sink, final_g):
    bsz, n_lat, _ = x.shape
    n_ctx = ctx.shape[1]
    depth = mod_w.shape[0]
    grid_w = 64
    assert n_ctx == ROW_TILE and n_lat % FFN_ROW_TILE == 0 and (bsz * n_ctx) % FFN_ROW_TILE == 0 and bsz <= 4
    seq = n_ctx + n_lat
    n_ctx_tiles = bsz
    n_lat_tiles = n_lat // ROW_TILE
    W = RWKV_WIDTH

    cs = jnp.zeros((8, D_MODEL), F32).at[:bsz].set(c).at[bsz].set(c_ctx)
    mods = _mod_vectors(jax.nn.silu(cs), mod_w, mod_b).reshape(depth, 8, N_MOD, D_MODEL)
    tile_row = jnp.concatenate([jnp.full((n_ctx_tiles,), bsz, jnp.int32),
                                jnp.repeat(jnp.arange(bsz, dtype=jnp.int32), n_lat_tiles)])
    modrows = mods[:, tile_row]

    xs = jnp.concatenate([ctx.reshape(bsz * n_ctx, D_MODEL), x.reshape(bsz * n_lat, D_MODEL)], axis=0)
    w1 = ffn_w1.astype(BF16)
    w2 = ffn_w2.astype(BF16)

    for l in range(depth):
        last = l == depth - 1
        mr = modrows[l]
        xs = _ffn(xs, mr, norm_g[l, 0], w1[l, 0], w2[l, 0], final_g, mod0=0, row_tile_of_mod=0)
        if l % 2 == 0:
            e = l // 2
            pa, pb = _ab_in_proj(xs, mr, norm_g[l, 1], ab_in_w[e].astype(BF16), bsz, n_lat_tiles)
            zeros = jnp.zeros((64, W), F32)
            w_lora = jnp.concatenate([
                jnp.concatenate([rwkv_w2[e, 0], rwkv_w2[e, 1], zeros, zeros, zeros], axis=1),
                jnp.concatenate([zeros, zeros, rwkv_a2[e, 0], rwkv_a2[e, 1], zeros], axis=1),
                jnp.concatenate([jnp.zeros((128, 4 * W), F32), rwkv_g2[e]], axis=1)], axis=0)
            vecs = jnp.stack([rwkv_k_k[e], rwkv_k_a[e], rwkv_r_k[e].reshape(W), rwkv_w0[e, 0], rwkv_w0[e, 1],
                              rwkv_a0[e, 0], rwkv_a0[e, 1], jnp.zeros((W,), F32)])
            r, v, a, kd, bv, lw, g, bonus = _rwkv_prep(pa, rwkv_mu[e], w_lora, vecs)
            yf, yb = _rwkv_scan(r, v, a, kd, bv, lw, n_ctx // RWKV_CHUNK)
            lam, bmat, cmat = _s5_discretise(s5_a_re[e], s5_a_im[e], s5_log_step[e], s5_b_re[e], s5_b_im[e],
                                             s5_c_re[e], s5_c_im[e])
            sf, sb = _s5_scan(pb, lam, bmat, cmat, n_ctx // S5_STEPS)
            vecs_out = jnp.stack([rwkv_lnx_g[e], rwkv_lnx_b[e], s5_d[e], s5_glu_b[e]] + [jnp.zeros((W,), F32)] * 4)
            xs = _ab_out(xs, mr, yf, yb, g, bonus, sf, sb, pb, vecs_out, s5_glu_w[e].astype(BF16),
                         ab_out_w[e].astype(BF16), bsz, n_lat_tiles)
            rows_mod0 = 0
        else:
            o = l // 2
            cos_t, sin_t = _rope_tables(n_lat, grid_w)
            qkv = _attn_in_proj(xs, mr, norm_g[l, 1], attn_in_w[o].astype(BF16), cos_t, sin_t,
                                n_ctx_tiles, n_lat_tiles)
            att = _attention(qkv, attn_sink[o], bsz, n_lat, n_ctx)
            if last:
                xs = _attn_out(xs, mr, att, attn_out_w[o].astype(BF16), n_ctx_tiles)
                rows_mod0 = n_ctx_tiles
            else:
                raise NotImplementedError("context update after an attention layer")
        xs = _ffn(xs, mr, norm_g[l, 2], w1[l, 1], w2[l, 1], final_g, mod0=6, row_tile_of_mod=rows_mod0,
                  final=last)
    return xs.reshape(bsz, n_lat, D_MODEL)
```

```python
import functools
import math

import jax
import jax.numpy as jnp
from jax import lax
from jax.experimental import pallas as pl
from jax.experimental.pallas import tpu as pltpu

F32 = jnp.float32
BF16 = jnp.bfloat16

D_MODEL = 1024
D_FF = 2816
N_MOD = 9
HEAD_DIM = 64
RMS_EPS = 1e-6
GN_EPS = 64e-5
RWKV_WIDTH = 512
RWKV_PROJ = 1792
S5_WIDTH = 512
S5_GROUP_CH = 16
S5_GROUPS = 32
S5_STATE = 64
ATTN_HEADS = 16
ATTN_KV_HEADS = 4
ATTN_BLOCK = 128
ATTN_WINDOW = 128
ATTN_Q_W = 1024
ATTN_KV_W = 256
ROPE_BASE = 10000.0
NEG_INF = -1e30

ROW_TILE = 256
FFN_ROW_TILE = 512
FFN_F_CHUNKS = (1280, 1536)
RWKV_CHUNK = 64
S5_STEPS = 64
VMEM_LIMIT = 56 * 1024 * 1024

_NN = (((1,), (0,)), ((), ()))
_NT = (((1,), (1,)), ((), ()))


def _dot(a, b, dims=_NN):
    return lax.dot_general(a, b, dims, preferred_element_type=F32)


def _split2(a):
    hi = a.astype(BF16)
    lo = (a - hi.astype(F32)).astype(BF16)
    return hi, lo


def _split3(a):
    hi = a.astype(BF16)
    r1 = a - hi.astype(F32)
    mid = r1.astype(BF16)
    lo = (r1 - mid.astype(F32)).astype(BF16)
    return hi, mid, lo


def _mm1(a, b, dims=_NN):
    return _dot(a.astype(BF16), b.astype(BF16), dims)


def _mm3(a, b, dims=_NN):
    ah, al = _split2(a)
    bh, bl = _split2(b)
    return _dot(ah, bh, dims) + (_dot(ah, bl, dims) + _dot(al, bh, dims))


def _pack3(w):
    hi, lo = _split2(w)
    return jnp.concatenate([hi, lo, hi], axis=0)


def _dot3_packed(a, w3):
    hi, lo = _split2(a)
    return _dot(jnp.concatenate([hi, hi, lo], axis=1), w3)


def _head_sum(a, ones3):
    hi, mid, lo = _split3(a)
    return _dot(jnp.concatenate([hi, mid, lo], axis=1), ones3)


def _sigmoid(x):
    return 1.0 / (1.0 + jnp.exp(-x))


def _silu(x):
    return x * _sigmoid(x)


def _params(*sem):
    return pltpu.CompilerParams(dimension_semantics=sem, vmem_limit_bytes=VMEM_LIMIT)


def _modnorm(x, g, shift, scale):
    xn = x * lax.rsqrt(jnp.mean(x * x, axis=-1, keepdims=True) + RMS_EPS)
    return (xn * g) * (1.0 + scale) + shift


def _head_ones3(width):
    r = lax.broadcasted_iota(jnp.int32, (width, width), 0) >> 6
    c = lax.broadcasted_iota(jnp.int32, (width, width), 1) >> 6
    ones = (r == c).astype(BF16)
    return jnp.concatenate([ones, ones, ones], axis=0)


def _mod_kernel(c_ref, w_ref, b_ref, o_ref):
    o_ref[...] = _mm3(c_ref[...], w_ref[...]) + b_ref[...]


def _mod_vectors(cs, mod_w, mod_b):
    depth = mod_w.shape[0]
    n = mod_w.shape[2]
    tn = 1152
    return pl.pallas_call(
        _mod_kernel,
        grid=(depth, n // tn),
        in_specs=[pl.BlockSpec((8, D_MODEL), lambda l, j: (0, 0)),
                  pl.BlockSpec((None, D_MODEL, tn), lambda l, j: (l, 0, j)),
                  pl.BlockSpec((None, 1, tn), lambda l, j: (l, 0, j))],
        out_specs=pl.BlockSpec((None, 8, tn), lambda l, j: (l, 0, j)),
        out_shape=jax.ShapeDtypeStruct((depth, 8, n), F32),
        compiler_params=_params("parallel", "parallel"),
        name="mod_vectors",
    )(cs, mod_w, mod_b.reshape(depth, 1, n))


def _ffn_kernel(x_ref, xc_ref, m_ref, g_ref, w1_ref, w2_ref, fg_ref, o_ref, *, mod0, final, n_ctx_tiles):
    x = x_ref[...]
    if n_ctx_tiles:
        x = jnp.where(pl.program_id(0) < n_ctx_tiles, xc_ref[...], x)
    h = _modnorm(x, g_ref[...], m_ref[mod0:mod0 + 1, :], m_ref[mod0 + 1:mod0 + 2, :]).astype(BF16)
    acc = None
    f0 = 0
    for fc in FFN_F_CHUNKS:
        gate = _dot(h, w1_ref[:, f0:f0 + fc])
        up = _dot(h, w1_ref[:, D_FF + f0:D_FF + f0 + fc])
        part = _dot((_silu(gate) * up).astype(BF16), w2_ref[f0:f0 + fc, :])
        acc = part if acc is None else acc + part
        f0 += fc
    y = x + (0.5 * m_ref[mod0 + 2:mod0 + 3, :]) * acc
    if final:
        y = y * lax.rsqrt(jnp.mean(y * y, axis=-1, keepdims=True) + RMS_EPS) * fg_ref[...]
    o_ref[...] = y


def _ffn(x, modrows, g, w1, w2, final_g, *, mod0, row_tile_of_mod, final=False, x_ctx=None):
    tm = FFN_ROW_TILE
    nct = 0 if x_ctx is None else x_ctx.shape[0] // tm
    rows = x.shape[0] + nct * tm
    step = tm // ROW_TILE
    once = pl.Buffered(1)
    if x_ctx is None:
        x_ctx = x
        ctx_spec = pl.BlockSpec((8, D_MODEL), lambda i: (0, 0))
    else:
        ctx_spec = pl.BlockSpec((tm, D_MODEL), lambda i: (jnp.minimum(i, nct - 1), 0))
    return pl.pallas_call(
        functools.partial(_ffn_kernel, mod0=mod0, final=final, n_ctx_tiles=nct),
        grid=(rows // tm,),
        in_specs=[pl.BlockSpec((tm, D_MODEL), lambda i: (jnp.maximum(i - nct, 0), 0)),
                  ctx_spec,
                  pl.BlockSpec((None, N_MOD, D_MODEL), lambda i: (row_tile_of_mod + i * step, 0, 0)),
                  pl.BlockSpec((1, D_MODEL), lambda i: (0, 0)),
                  pl.BlockSpec(w1.shape, lambda i: (0, 0), pipeline_mode=once),
                  pl.BlockSpec(w2.shape, lambda i: (0, 0), pipeline_mode=once),
                  pl.BlockSpec((1, D_MODEL), lambda i: (0, 0))],
        out_specs=pl.BlockSpec((tm, D_MODEL), lambda i: (i, 0)),
        out_shape=jax.ShapeDtypeStruct((rows, D_MODEL), F32),
        compiler_params=_params("parallel"),
        name="ffn",
    )(x, x_ctx, modrows, g.reshape(1, D_MODEL), w1, w2, final_g.reshape(1, D_MODEL))


def _x_tile(bsz, n_lat_tiles):
    return lambda b, k: (jnp.where(k == 0, b, bsz + b * n_lat_tiles + k - 1), 0)


def _ab_in_kernel(x_ref, m_ref, g_ref, w_ref, pa_ref, pb_ref):
    h = _modnorm(x_ref[...], g_ref[...], m_ref[3:4, :], m_ref[4:5, :]).astype(BF16)
    p = _dot(h, w_ref[...])
    pa_ref[...] = p[:, :RWKV_PROJ]
    pb_ref[...] = p[:, RWKV_PROJ:]


def _ab_in_proj(x, modrows, g, w, bsz, n_lat_tiles):
    nt = n_lat_tiles + 1
    seq = nt * ROW_TILE
    xt = _x_tile(bsz, n_lat_tiles)
    return pl.pallas_call(
        _ab_in_kernel,
        grid=(bsz, nt),
        in_specs=[pl.BlockSpec((ROW_TILE, D_MODEL), xt),
                  pl.BlockSpec((None, N_MOD, D_MODEL), lambda b, k: (xt(b, k)[0], 0, 0)),
                  pl.BlockSpec((1, D_MODEL), lambda b, k: (0, 0)),
                  pl.BlockSpec(w.shape, lambda b, k: (0, 0))],
        out_specs=[pl.BlockSpec((None, ROW_TILE, RWKV_PROJ), lambda b, k: (b, k, 0)),
                   pl.BlockSpec((None, ROW_TILE, S5_WIDTH), lambda b, k: (b, k, 0))],
        out_shape=[jax.ShapeDtypeStruct((bsz, seq, RWKV_PROJ), F32),
                   jax.ShapeDtypeStruct((bsz, seq, S5_WIDTH), F32)],
        compiler_params=_params("parallel", "parallel"),
        name="ab_in_proj",
    )(x, modrows, g.reshape(1, D_MODEL), w)


def _rwkv_prep_kernel(cur_ref, prev_ref, next_ref, mu_ref, wl_ref, vec_ref, ones_ref,
                      r_ref, v_ref, a_ref, k_ref, b_ref, lw_ref, g_ref, bonus_ref):
    k_idx = pl.program_id(1)
    nt = pl.num_programs(1)
    cur = cur_ref[...]
    rows = cur.shape[0]
    row = lax.broadcasted_iota(jnp.int32, cur.shape, 0)
    use_prev = k_idx >= 2
    use_next = jnp.logical_and(k_idx >= 1, k_idx < nt - 1)
    prev_row = jnp.where(use_prev, prev_ref[7:8, :], 0.0)
    next_row = jnp.where(use_next, next_ref[0:1, :], 0.0)
    prev = jnp.where(row == 0, prev_row, pltpu.roll(cur, 1, axis=0))
    nxt = jnp.where(row == rows - 1, next_row, pltpu.roll(cur, rows - 1, axis=0))
    p = cur + mu_ref[0:1, :] * (prev - cur) + mu_ref[1:2, :] * (nxt - cur)

    W = RWKV_WIDTH
    r, k, v = p[:, :W], p[:, W:2 * W], p[:, 2 * W:3 * W]
    lora_in = p[:, 3 * W:]
    lane = lax.broadcasted_iota(jnp.int32, lora_in.shape, 1)
    lora_act = jnp.where(lane < 64, jnp.tanh(lora_in), jnp.where(lane < 128, lora_in, _sigmoid(lora_in)))
    lo = _dot3_packed(lora_act, wl_ref[...])

    k_k, k_a, r_k = vec_ref[0:1, :], vec_ref[1:2, :], vec_ref[2:3, :]
    ones = ones_ref[...]
    kk = k * k_k
    kk = kk * lax.rsqrt(jnp.maximum(_head_sum(kk * kk, ones), 1e-12))
    r_ref[...] = r
    v_ref[...] = v
    a_ref[...] = -kk
    k_sum = jnp.zeros_like(k)
    for d in range(2):
        z = -(vec_ref[3 + d:4 + d, :] + lo[:, d * W:(d + 1) * W])
        softplus = jnp.maximum(z, 0.0) + jnp.log(1.0 + jnp.exp(-jnp.abs(z)))
        lw_ref[d] = -jnp.exp(-softplus - 0.5)
        a = _sigmoid(vec_ref[5 + d:6 + d, :] + lo[:, (2 + d) * W:(3 + d) * W])
        kd = k * (1.0 + (a - 1.0) * k_a)
        k_ref[d] = kd
        b_ref[d] = kk * a
        k_sum = k_sum + kd
    g_ref[...] = lo[:, 4 * W:]
    bonus_ref[...] = _head_sum(r * (0.5 * k_sum) * r_k, ones) * v


def _rwkv_prep(pa, mu, w_lora, vecs, ones):
    bsz, seq, _ = pa.shape
    nt = seq // ROW_TILE
    sub = ROW_TILE // 8
    last_blk = seq // 8 - 1
    W = RWKV_WIDTH
    one = jax.ShapeDtypeStruct((bsz, seq, W), F32)
    two = jax.ShapeDtypeStruct((2, bsz, seq, W), F32)
    spec1 = pl.BlockSpec((None, ROW_TILE, W), lambda b, k: (b, k, 0))
    spec2 = pl.BlockSpec((2, None, ROW_TILE, W), lambda b, k: (0, b, k, 0))
    return pl.pallas_call(
        _rwkv_prep_kernel,
        grid=(bsz, nt),
        in_specs=[pl.BlockSpec((None, ROW_TILE, RWKV_PROJ), lambda b, k: (b, k, 0)),
                  pl.BlockSpec((None, 8, RWKV_PROJ), lambda b, k: (b, jnp.maximum(k * sub - 1, 0), 0)),
                  pl.BlockSpec((None, 8, RWKV_PROJ), lambda b, k: (b, jnp.minimum((k + 1) * sub, last_blk), 0)),
                  pl.BlockSpec(mu.shape, lambda b, k: (0, 0)),
                  pl.BlockSpec(w_lora.shape, lambda b, k: (0, 0)),
                  pl.BlockSpec(vecs.shape, lambda b, k: (0, 0)),
                  pl.BlockSpec(ones.shape, lambda b, k: (0, 0))],
        out_specs=[spec1, spec1, spec1, spec2, spec2, spec2, spec1, spec1],
        out_shape=[one, one, one, two, two, two, one, one],
        compiler_params=_params("parallel", "parallel"),
        name="rwkv_prep",
    )(pa, pa, pa, mu, w_lora, vecs, ones)


RWKV_TILE_HEADS = 2
RWKV_TILE = RWKV_TILE_HEADS * HEAD_DIM
RWKV_BATCHES_PER_STEP = 2


def _rwkv_scaled(fwd, r, v, a, k, b, lw):
    C = RWKV_CHUNK
    rr = lax.broadcasted_iota(jnp.int32, (C, C), 0)
    cc = lax.broadcasted_iota(jnp.int32, (C, C), 1)
    tri = ((rr >= cc) if fwd else (rr <= cc)).astype(BF16)
    hi, mid, lo = _split3(lw)
    cum = _dot(jnp.concatenate([tri, tri, tri], axis=1), jnp.concatenate([hi, mid, lo], axis=0))
    last = cum[C - 1:C, :] if fwd else cum[0:1, :]
    einv = jnp.exp(-cum)
    to_end = jnp.exp(last - cum)
    return dict(at=a * jnp.exp(cum - lw), rt=r * jnp.exp(cum), bt=b * einv, kt=k * einv,
                b_end=b * to_end, k_end=k * to_end, p_end=jnp.exp(last), v=v)


def _rwkv_chunks(dirs, h_ref, y_refs):
    C = RWKV_CHUNK
    TW = RWKV_TILE
    tq = lax.broadcasted_iota(jnp.int32, (C, TW), 0)
    sq = lax.broadcasted_iota(jnp.int32, (C, TW), 1) & (C - 1)
    eye_cat = (tq == sq).astype(F32)
    rb = lax.broadcasted_iota(jnp.int32, (TW, TW), 0)
    cb = lax.broadcasted_iota(jnp.int32, (TW, TW), 1)
    same_head = (rb >> 6) == (cb >> 6)
    diag = rb == cb

    def stack(x):
        return jnp.where(same_head, jnp.concatenate([x] * RWKV_TILE_HEADS, axis=0), jnp.zeros((), x.dtype))

    chains = []
    for di, (fwd, op) in enumerate(dirs):
        strict = (tq > sq) if fwd else (tq < sq)
        incl = (tq >= sq) if fwd else (tq <= sq)
        for q in range(RWKV_WIDTH // TW):
            sl = slice(TW * q, TW * (q + 1))
            ch = {name: val[:, sl] for name, val in op.items()}
            ch.update(strict=strict, incl=incl, di=di, q=q, sl=sl)
            chains.append(ch)

    for ch in chains:
        lhs = jnp.concatenate([ch["at"], ch["rt"]], axis=0).astype(BF16)
        rhs = jnp.concatenate([stack(ch["bt"].astype(BF16)), stack(ch["kt"].astype(BF16))], axis=0)
        s_all = _dot(lhs, rhs, _NT)
        ch["n_ab"] = jnp.where(ch["strict"], s_all[:C, :TW], 0.0)
        ch["n_ak"] = jnp.where(ch["strict"], s_all[:C, TW:], 0.0)
        ch["n_rb"] = jnp.where(ch["incl"], s_all[C:, :TW], 0.0)
        ch["n_rk"] = jnp.where(ch["incl"], s_all[C:, TW:], 0.0)
        ch["tinv"] = eye_cat + ch["n_ab"]
        ch["pw"] = ch["n_ab"]

    levels = int(math.log2(C))
    for i in range(levels):
        for ch in chains:
            pw_b = ch["pw"].astype(BF16)
            if i == 0:
                ch["pw"] = _dot(pw_b, stack(pw_b))
                v_s = stack(ch["v"].astype(BF16))
                ch["kv"] = _dot(jnp.concatenate([ch["n_ak"], ch["n_rk"]], axis=0).astype(BF16), v_s)
            elif i == levels - 1:
                ch["tinv"] = ch["tinv"] + _dot(pw_b, stack(ch["tinv"].astype(BF16)))
            else:
                res = _dot(pw_b, jnp.concatenate([stack(pw_b), stack(ch["tinv"].astype(BF16))], axis=1))
                ch["pw"] = res[:, :TW]
                ch["tinv"] = ch["tinv"] + res[:, TW:]

    for ch in chains:
        nh, nl = _split2(ch["n_ab"])
        xh, xl = _split2(ch["tinv"])
        ch["xh"] = xh
        ch["resid"] = (eye_cat - ch["tinv"]) + _dot(jnp.concatenate([nh, nh, nl], axis=1),
                                                     jnp.concatenate([stack(xh), stack(xl), stack(xh)], axis=0))
    for ch in chains:
        ch["tinv"] = ch["tinv"] + _dot(ch["xh"], stack(ch["resid"].astype(BF16)))

    for ch in chains:
        rhs = jnp.concatenate([stack(ch["at"].astype(BF16)), stack(ch["kv"][:C].astype(BF16))], axis=1)
        ch["aw"] = _dot(ch["tinv"].astype(BF16), rhs).astype(BF16)

    for ch in chains:
        aw = ch["aw"]
        c1 = _dot(ch["n_rb"].astype(BF16), jnp.concatenate([stack(aw[:, :TW]), stack(aw[:, TW:])], axis=1))
        ch["qm"] = ch["rt"] + c1[:, :TW]
        ch["z"] = c1[:, TW:] + ch["kv"][C:]
        bk_t = jnp.concatenate([ch["b_end"], ch["k_end"]], axis=0).T.astype(BF16)
        low = jnp.concatenate([jnp.zeros((C, TW), BF16), ch["v"].astype(BF16)], axis=1)
        c2 = _dot(bk_t, jnp.concatenate([aw, low], axis=0))
        ch["m_upd"] = jnp.where(same_head, c2[:, :TW], 0.0)
        ch["g_upd"] = jnp.where(same_head, c2[:, TW:], 0.0)

    for ch in chains:
        h = h_ref[ch["di"], ch["q"]]
        d = _dot(jnp.concatenate([ch["m_upd"], ch["qm"]], axis=0).astype(BF16), h.astype(BF16))
        y_refs[ch["di"]][:, ch["sl"]] = d[TW:] + ch["z"]
        p_col = jnp.sum(jnp.where(diag, ch["p_end"], 0.0), axis=1, keepdims=True)
        h_ref[ch["di"], ch["q"]] = p_col * h + d[:TW] + ch["g_upd"]


def _rwkv_scan_kernel(rf_ref, vf_ref, af_ref, rb_ref, vb_ref, ab_ref, kf_ref, bf_ref, lwf_ref,
                      kb_ref, bb_ref, lwb_ref, yf_ref, yb_ref, h_ref, *, nb):
    @pl.when(pl.program_id(1) == 0)
    def _():
        h_ref[...] = jnp.zeros_like(h_ref)

    dirs, y_refs = [], []
    for i in range(nb):
        dirs.append((True, _rwkv_scaled(True, rf_ref[i], vf_ref[i], af_ref[i], kf_ref[i], bf_ref[i], lwf_ref[i])))
        dirs.append((False, _rwkv_scaled(False, rb_ref[i], vb_ref[i], ab_ref[i], kb_ref[i], bb_ref[i], lwb_ref[i])))
        y_refs += [yf_ref.at[i], yb_ref.at[i]]
    _rwkv_chunks(dirs, h_ref, y_refs)


def _rwkv_scan(r, v, a, k, b, lw, n_ctx_chunks):
    bsz, seq, W = r.shape
    C = RWKV_CHUNK
    nch = seq // C

    def back(j):
        return jnp.where(j < n_ctx_chunks, n_ctx_chunks - 1 - j, nch - 1 - (j - n_ctx_chunks))

    nb = RWKV_BATCHES_PER_STEP if bsz % RWKV_BATCHES_PER_STEP == 0 else 1
    one_f = pl.BlockSpec((nb, C, W), lambda g, j: (g, j, 0))
    one_b = pl.BlockSpec((nb, C, W), lambda g, j: (g, back(j), 0))
    two_f = pl.BlockSpec((None, nb, C, W), lambda g, j: (0, g, j, 0))
    two_b = pl.BlockSpec((None, nb, C, W), lambda g, j: (1, g, back(j), 0))
    out = jax.ShapeDtypeStruct((bsz, seq, W), F32)
    return pl.pallas_call(
        functools.partial(_rwkv_scan_kernel, nb=nb),
        grid=(bsz // nb, nch),
        in_specs=[one_f, one_f, one_f, one_b, one_b, one_b, two_f, two_f, two_f, two_b, two_b, two_b],
        out_specs=[one_f, one_b],
        out_shape=[out, out],
        scratch_shapes=[pltpu.VMEM((2 * nb, W // RWKV_TILE, RWKV_TILE, RWKV_TILE), F32)],
        compiler_params=_params("parallel", "arbitrary"),
        name="rwkv_scan",
    )(r, v, a, r, v, a, k, b, lw, k, b, lw)


def _s5_scan_kernel(uf_ref, ub_ref, lam_ref, bmat_ref, cmat_ref, yf_ref, yb_ref, bu_ref, st_ref, *, bsz):
    S = S5_STEPS
    half = 8 * S5_STATE
    n_nat = 2 * bsz * S

    @pl.when(pl.program_id(0) == 0)
    def _():
        st_ref[...] = jnp.zeros_like(st_ref)

    def perm_mask(rho, col):
        t, s = rho >> 3, rho & 7
        src = jnp.where(s < 4, s * S + t, bsz * S + (s - 4) * S + (S - 1 - t))
        return jnp.logical_and((s & 3) < bsz, col == src).astype(BF16)

    perm = perm_mask(lax.broadcasted_iota(jnp.int32, (S * 8, n_nat), 0),
                     lax.broadcasted_iota(jnp.int32, (S * 8, n_nat), 1))
    perm_t = perm_mask(lax.broadcasted_iota(jnp.int32, (n_nat, S * 8), 1),
                       lax.broadcasted_iota(jnp.int32, (n_nat, S * 8), 0))
    u_nat = jnp.concatenate([uf_ref[...].reshape(bsz * S, S5_WIDTH), ub_ref[...].reshape(bsz * S, S5_WIDTH)], axis=0)
    u = _dot(perm, u_nat.astype(BF16))
    is_fwd = (lax.broadcasted_iota(jnp.int32, (S * 8, 128), 0) & 7) < 4
    for q in range(S5_WIDTH // 128):
        uq = u[:, 128 * q:128 * (q + 1)]
        lhs = jnp.concatenate([jnp.where(is_fwd, uq, 0.0), jnp.where(is_fwd, 0.0, uq)], axis=1)
        bu_ref[:, 2 * half * q:2 * half * (q + 1)] = _dot(lhs.astype(BF16), bmat_ref[q])

    nq = S5_WIDTH // 128
    lam = lam_ref[...]

    def step(t, carry):
        rows = pl.ds(pl.multiple_of(t * 8, 8), 8)
        new = []
        for q in range(nq):
            base = 2 * half * q
            ar, ai = lam[:, base:base + half], lam[:, base + half:base + 2 * half]
            xr, xi = carry[2 * q], carry[2 * q + 1]
            nr = ar * xr - ai * xi + bu_ref[rows, base:base + half]
            ni = ar * xi + ai * xr + bu_ref[rows, base + half:base + 2 * half]
            bu_ref[rows, base:base + half] = nr
            bu_ref[rows, base + half:base + 2 * half] = ni
            new += [nr, ni]
        return tuple(new)

    init = tuple(st_ref[:, half * j:half * (j + 1)] for j in range(2 * nq))
    final = lax.fori_loop(0, S, step, init, unroll=2)
    for j in range(2 * nq):
        st_ref[:, half * j:half * (j + 1)] = final[j]

    ys = []
    for q in range(nq):
        xq = bu_ref[:, 2 * half * q:2 * half * (q + 1)]
        y2 = _dot(xq.astype(BF16), cmat_ref[q])
        ys.append(jnp.where(is_fwd, y2[:, :128], y2[:, 128:]))
    hi, lo = _split2(jnp.concatenate(ys, axis=1))
    y_nat = _dot(jnp.concatenate([perm_t, perm_t], axis=1), jnp.concatenate([hi, lo], axis=0))
    yf_ref[...] = y_nat[:bsz * S].reshape(bsz, S, S5_WIDTH)
    yb_ref[...] = y_nat[bsz * S:].reshape(bsz, S, S5_WIDTH)


def _s5_scan(u, lam, bmat, cmat, n_ctx_blocks):
    bsz, seq, _ = u.shape
    S = S5_STEPS
    nblk = seq // S
    nstate = 2 * S5_GROUPS * S5_STATE

    def back(i):
        return jnp.where(i < n_ctx_blocks, n_ctx_blocks - 1 - i, nblk - 1 - (i - n_ctx_blocks))

    spec_f = pl.BlockSpec((bsz, S, S5_WIDTH), lambda i: (0, i, 0))
    spec_b = pl.BlockSpec((bsz, S, S5_WIDTH), lambda i: (0, back(i), 0))
    out = jax.ShapeDtypeStruct(u.shape, F32)
    return pl.pallas_call(
        functools.partial(_s5_scan_kernel, bsz=bsz),
        grid=(nblk,),
        in_specs=[spec_f, spec_b,
                  pl.BlockSpec(lam.shape, lambda i: (0, 0)),
                  pl.BlockSpec(bmat.shape, lambda i: (0, 0, 0)),
                  pl.BlockSpec(cmat.shape, lambda i: (0, 0, 0))],
        out_specs=[spec_f, spec_b],
        out_shape=[out, out],
        scratch_shapes=[pltpu.VMEM((S * 8, nstate), F32), pltpu.VMEM((8, nstate), F32)],
        compiler_params=_params("arbitrary"),
        name="s5_scan",
    )(u, u, lam, bmat, cmat)


def _s5_discretise(a_re, a_im, log_step, b_re, b_im, c_re, c_im):
    lam_re = jnp.minimum(a_re, -1e-4)
    lam_im = a_im
    dt = jnp.exp(log_step)[..., None]
    mag = jnp.exp(lam_re * dt)
    ab_re, ab_im = mag * jnp.cos(lam_im * dt), mag * jnp.sin(lam_im * dt)
    den = lam_re * lam_re + lam_im * lam_im
    f_re = ((ab_re - 1.0) * lam_re + ab_im * lam_im) / den
    f_im = (ab_im * lam_re - (ab_re - 1.0) * lam_im) / den
    bb_re = f_re[..., None] * b_re - f_im[..., None] * b_im
    bb_im = f_re[..., None] * b_im + f_im[..., None] * b_re
    nq = S5_WIDTH // 128
    eye8 = jnp.eye(8, dtype=F32)

    def lanes(t):
        return t.reshape(2, nq, 8 * S5_STATE)

    lam = jnp.concatenate([lanes(ab_re), lanes(ab_im)], axis=-1).reshape(2, nq * 16 * S5_STATE)
    lam = jnp.repeat(lam, 4, axis=0)

    def in_block(t):
        t = t.reshape(2, nq, 8, S5_STATE, S5_GROUP_CH)
        return jnp.einsum('dqgpi,gh->dqgihp', t, eye8).reshape(2, nq, 128, 8 * S5_STATE)

    bmat = jnp.concatenate([in_block(bb_re), in_block(bb_im)], axis=-1)
    bmat = jnp.concatenate([bmat[0], bmat[1]], axis=1)

    def out_block(t):
        t = t.reshape(2, nq, 8, S5_GROUP_CH, S5_STATE)
        return jnp.einsum('dqgip,gh->dqgphi', t, eye8).reshape(2, nq, 8 * S5_STATE, 128)

    cmat = jnp.concatenate([out_block(c_re), -out_block(c_im)], axis=2)
    cmat = jnp.concatenate([cmat[0], cmat[1]], axis=-1)
    return lam, bmat.astype(BF16), cmat.astype(BF16)


def _ab_out_kernel(x_ref, m_ref, yf_ref, yb_ref, g_ref, bonus_ref, sf_ref, sb_ref, u_ref, vec_ref, gluw_ref, w_ref,
                   ones_ref, o_ref):
    W = RWKV_WIDTH
    ones = ones_ref[...]
    y = yf_ref[...] + yb_ref[...]
    mu = _head_sum(y, ones) * (1.0 / HEAD_DIM)
    yc = y - mu
    var = _head_sum(yc * yc, ones) * (1.0 / HEAD_DIM)
    ya = (yc * lax.rsqrt(var + GN_EPS) * vec_ref[0:1, :] + vec_ref[1:2, :] + bonus_ref[...]) * g_ref[...]

    s = (sf_ref[...] + sb_ref[...]) + vec_ref[2:3, :] * u_ref[...]
    z = 0.5 * s * (1.0 + jnp.tanh(math.sqrt(2.0 / math.pi) * (s + 0.044715 * (s * s * s))))
    yb = z * _sigmoid(_mm1(z, gluw_ref[...]) + vec_ref[3:4, :])
    out = _dot(ya.astype(BF16), w_ref[:W, :]) + _dot(yb.astype(BF16), w_ref[W:, :])
    o_ref[...] = x_ref[...] + m_ref[5:6, :] * out


def _ab_out(x, modrows, yf, yb, g, bonus, sf, sb, u, vecs, glu_w, out_w, ones, bsz, n_lat_tiles):
    nt = n_lat_tiles + 1
    W = RWKV_WIDTH
    xt = _x_tile(bsz, n_lat_tiles)
    seq_spec = pl.BlockSpec((None, ROW_TILE, W), lambda b, k: (b, k, 0))
    return pl.pallas_call(
        _ab_out_kernel,
        grid=(bsz, nt),
        in_specs=[pl.BlockSpec((ROW_TILE, D_MODEL), xt),
                  pl.BlockSpec((None, N_MOD, D_MODEL), lambda b, k: (xt(b, k)[0], 0, 0)),
                  seq_spec, seq_spec, seq_spec, seq_spec, seq_spec, seq_spec, seq_spec,
                  pl.BlockSpec(vecs.shape, lambda b, k: (0, 0)),
                  pl.BlockSpec(glu_w.shape, lambda b, k: (0, 0)),
                  pl.BlockSpec(out_w.shape, lambda b, k: (0, 0)),
                  pl.BlockSpec(ones.shape, lambda b, k: (0, 0))],
        out_specs=pl.BlockSpec((ROW_TILE, D_MODEL), xt),
        out_shape=jax.ShapeDtypeStruct(x.shape, F32),
        compiler_params=_params("parallel", "parallel"),
        name="ab_out",
    )(x, modrows, yf, yb, g, bonus, sf, sb, u, vecs, glu_w, out_w, ones)


def _attn_in_kernel(x_ref, m_ref, g_ref, w_ref, cos_ref, sin_ref, o_ref):
    h = _modnorm(x_ref[...], g_ref[...], m_ref[3:4, :], m_ref[4:5, :]).astype(BF16)
    p = _dot(h, w_ref[...])
    qk_w = ATTN_Q_W + ATTN_KV_W
    qk = p[:, :qk_w]
    half = HEAD_DIM // 2
    lane = lax.broadcasted_iota(jnp.int32, qk.shape, 1)
    first = (lane & (HEAD_DIM - 1)) < half
    partner = jnp.where(first, pltpu.roll(qk, qk_w - half, axis=1), pltpu.roll(qk, half, axis=1))
    reps = qk_w // 128
    cos = jnp.concatenate([cos_ref[...]] * reps, axis=1)
    sin = jnp.concatenate([sin_ref[...]] * reps, axis=1)
    o_ref[:, :qk_w] = qk * cos + partner * sin
    o_ref[:, qk_w:] = p[:, qk_w:]


def _attn_in_proj(x, modrows, g, w, cos_t, sin_t, n_ctx_tiles, n_lat_tiles):
    rows = x.shape[0]
    width = w.shape[1]

    def rope_tile(i):
        return (jnp.where(i < n_ctx_tiles, 0, 1 + (i - n_ctx_tiles) % n_lat_tiles), 0)

    return pl.pallas_call(
        _attn_in_kernel,
        grid=(rows // ROW_TILE,),
        in_specs=[pl.BlockSpec((ROW_TILE, D_MODEL), lambda i: (i, 0)),
                  pl.BlockSpec((None, N_MOD, D_MODEL), lambda i: (i, 0, 0)),
                  pl.BlockSpec((1, D_MODEL), lambda i: (0, 0)),
                  pl.BlockSpec(w.shape, lambda i: (0, 0)),
                  pl.BlockSpec((ROW_TILE, 128), rope_tile),
                  pl.BlockSpec((ROW_TILE, 128), rope_tile)],
        out_specs=pl.BlockSpec((ROW_TILE, width), lambda i: (i, 0)),
        out_shape=jax.ShapeDtypeStruct((rows, width), F32),
        compiler_params=_params("parallel"),
        name="attn_in_proj",
    )(x, modrows, g.reshape(1, D_MODEL), w, cos_t, sin_t)


def _rope_tables(n_lat, grid_w):
    half = HEAD_DIM // 2
    t = jnp.arange(n_lat)
    row_id = (t // grid_w).astype(F32)
    col_id = (t % grid_w).astype(F32)
    inv_freq = ROPE_BASE ** (-jnp.arange(0, half, 2, dtype=F32) / half)
    ang = jnp.concatenate([row_id[:, None] * inv_freq, col_id[:, None] * inv_freq], axis=-1)
    cos, sin = jnp.cos(ang), jnp.sin(ang)
    cos_t = jnp.concatenate([cos, cos, cos, cos], axis=-1)
    sin_t = jnp.concatenate([-sin, sin, -sin, sin], axis=-1)
    cos_t = jnp.concatenate([jnp.ones((ROW_TILE, 128), F32), cos_t], axis=0)
    sin_t = jnp.concatenate([jnp.zeros((ROW_TILE, 128), F32), sin_t], axis=0)
    return cos_t, sin_t


def _attn_kernel(sink_ref, q_ref, kp_ref, kc_ref, kn_ref, vp_ref, vc_ref, vn_ref, kx_ref, vx_ref, o_ref, *, nblk):
    i = pl.program_id(1)
    Q = ATTN_BLOCK
    G = ATTN_HEADS // ATTN_KV_HEADS
    k_all = jnp.concatenate([kp_ref[...], kc_ref[...], kn_ref[...], kx_ref[...]], axis=0).astype(BF16)
    v_all = jnp.concatenate([vp_ref[...], vc_ref[...], vn_ref[...], vx_ref[...]], axis=0).astype(BF16)
    nk = k_all.shape[0]
    qi = lax.broadcasted_iota(jnp.int32, (G * Q, nk), 0) & (Q - 1)
    cj = lax.broadcasted_iota(jnp.int32, (G * Q, nk), 1)
    mj = cj - Q
    blk = jnp.where(cj < Q, i - 1, jnp.where(cj < 2 * Q, i, i + 1))
    valid = jnp.logical_and(jnp.abs(mj - qi) <= ATTN_WINDOW, jnp.logical_and(blk >= 0, blk < nblk))
    valid = jnp.logical_or(valid, cj >= 3 * Q)
    q = q_ref[...] * (HEAD_DIM ** -0.5)
    row_head = lax.broadcasted_iota(jnp.int32, (G * Q, 1), 0) >> 7
    heads = range(ATTN_KV_HEADS)
    scores = []
    for kh in heads:
        q4 = jnp.concatenate([q[:, (kh * G + g) * HEAD_DIM:(kh * G + g + 1) * HEAD_DIM] for g in range(G)], axis=0)
        scores.append(_dot(q4.astype(BF16), k_all[:, kh * HEAD_DIM:(kh + 1) * HEAD_DIM], _NT))
    probs, dens = [], []
    for kh in heads:
        s = jnp.where(valid, scores[kh], NEG_INF)
        sink = jnp.zeros((G * Q, 1), F32)
        for g in range(G):
            sink = jnp.where(row_head == g, sink_ref[kh * G + g], sink)
        m = jnp.maximum(jnp.max(s, axis=-1, keepdims=True), sink)
        pr = jnp.exp(s - m)
        dens.append(jnp.sum(pr, axis=-1, keepdims=True) + jnp.exp(sink - m))
        probs.append(pr.astype(BF16))
    for kh in heads:
        o = _dot(probs[kh], v_all[:, kh * HEAD_DIM:(kh + 1) * HEAD_DIM]) / dens[kh]
        for g in range(G):
            hq = kh * G + g
            o_ref[:, hq * HEAD_DIM:(hq + 1) * HEAD_DIM] = o[g * Q:(g + 1) * Q]


def _attention(qkv, sink, bsz, n_lat, n_ctx):
    Q = ATTN_BLOCK
    nblk = n_lat // Q
    lat0 = bsz * n_ctx // Q
    kcol, vcol = ATTN_Q_W // ATTN_KV_W, ATTN_Q_W // ATTN_KV_W + 1

    def kv_spec(col, off):
        return pl.BlockSpec((Q, ATTN_KV_W),
                            lambda b, i: (lat0 + b * nblk + jnp.clip(i + off, 0, nblk - 1), col))

    def ctx_spec(col):
        return pl.BlockSpec((n_ctx, ATTN_KV_W), lambda b, i: (b, col))

    return pl.pallas_call(
        functools.partial(_attn_kernel, nblk=nblk),
        grid=(bsz, nblk),
        in_specs=[pl.BlockSpec(memory_space=pltpu.SMEM),
                  pl.BlockSpec((Q, ATTN_Q_W), lambda b, i: (lat0 + b * nblk + i, 0)),
                  kv_spec(kcol, -1), kv_spec(kcol, 0), kv_spec(kcol, 1),
                  kv_spec(vcol, -1), kv_spec(vcol, 0), kv_spec(vcol, 1),
                  ctx_spec(kcol), ctx_spec(vcol)],
        out_specs=pl.BlockSpec((Q, ATTN_Q_W), lambda b, i: (b * nblk + i, 0)),
        out_shape=jax.ShapeDtypeStruct((bsz * n_lat, ATTN_Q_W), F32),
        compiler_params=_params("parallel", "parallel"),
        name="window_attention",
    )(sink, qkv, qkv, qkv, qkv, qkv, qkv, qkv, qkv, qkv)


def _attn_out_kernel(x_ref, m_ref, o_ref_in, w_ref, out_ref):
    out_ref[...] = x_ref[...] + m_ref[5:6, :] * _dot(o_ref_in[...].astype(BF16), w_ref[...])


def _attn_out(x, modrows, o, w, n_ctx_tiles):
    rows = o.shape[0]
    return pl.pallas_call(
        _attn_out_kernel,
        grid=(rows // ROW_TILE,),
        in_specs=[pl.BlockSpec((ROW_TILE, D_MODEL), lambda i: (n_ctx_tiles + i, 0)),
                  pl.BlockSpec((None, N_MOD, D_MODEL), lambda i: (n_ctx_tiles + i, 0, 0)),
                  pl.BlockSpec((ROW_TILE, D_MODEL), lambda i: (i, 0)),
                  pl.BlockSpec(w.shape, lambda i: (0, 0))],
        out_specs=pl.BlockSpec((ROW_TILE, D_MODEL), lambda i: (i, 0)),
        out_shape=jax.ShapeDtypeStruct((rows, D_MODEL), F32),
        compiler_params=_params("parallel"),
        name="attn_out",
    )(x, modrows, o, w)


def kernel(x, c, ctx, c_ctx, norm_g, mod_w, mod_b, ffn_w1, ffn_w2, ab_in_w, ab_out_w, rwkv_mu, rwkv_w0, rwkv_w2, rwkv_a0, rwkv_a2, rwkv_g2, rwkv_k_k, rwkv_k_a, rwkv_r_k, rwkv_lnx_g, rwkv_lnx_b, s5_a_re, s5_a_im, s5_log_step, s5_b_re, s5_b_im, s5_c_re, s5_c_im, s5_d, s5_glu_w, s5_glu_b, attn_in_w, attn_out_w, attn_sink, final_g):
    bsz, n_lat, _ = x.shape
    n_ctx = ctx.shape[1]
    depth = mod_w.shape[0]
    grid_w = 64
    assert n_ctx == ROW_TILE and n_lat % FFN_ROW_TILE == 0 and (bsz * n_ctx) % FFN_ROW_TILE == 0 and bsz <= 4
    seq = n_ctx + n_lat
    n_ctx_tiles = bsz
    n_lat_tiles = n_lat // ROW_TILE
    W = RWKV_WIDTH

    cs = jnp.zeros((8, D_MODEL), F32).at[:bsz].set(c).at[bsz].set(c_ctx)
    mods = _mod_vectors(jax.nn.silu(cs), mod_w, mod_b).reshape(depth, 8, N_MOD, D_MODEL)
    tile_row = jnp.concatenate([jnp.full((n_ctx_tiles,), bsz, jnp.int32),
                                jnp.repeat(jnp.arange(bsz, dtype=jnp.int32), n_lat_tiles)])
    modrows = mods[:, tile_row]

    xs = x.reshape(bsz * n_lat, D_MODEL)
    w1 = ffn_w1.astype(BF16)
    w2 = ffn_w2.astype(BF16)

    for l in range(depth):
        last = l == depth - 1
        mr = modrows[l]
        xs = _ffn(xs, mr, norm_g[l, 0], w1[l, 0], w2[l, 0], final_g, mod0=0, row_tile_of_mod=0,
                  x_ctx=ctx.reshape(bsz * n_ctx, D_MODEL) if l == 0 else None)
        if l % 2 == 0:
            e = l // 2
            pa, pb = _ab_in_proj(xs, mr, norm_g[l, 1], ab_in_w[e].astype(BF16), bsz, n_lat_tiles)
            zeros = jnp.zeros((64, W), F32)
            w_lora = jnp.concatenate([
                jnp.concatenate([rwkv_w2[e, 0], rwkv_w2[e, 1], zeros, zeros, zeros], axis=1),
                jnp.concatenate([zeros, zeros, rwkv_a2[e, 0], rwkv_a2[e, 1], zeros], axis=1),
                jnp.concatenate([jnp.zeros((128, 4 * W), F32), rwkv_g2[e]], axis=1)], axis=0)
            vecs = jnp.stack([rwkv_k_k[e], rwkv_k_a[e], rwkv_r_k[e].reshape(W), rwkv_w0[e, 0], rwkv_w0[e, 1],
                              rwkv_a0[e, 0], rwkv_a0[e, 1], jnp.zeros((W,), F32)])
            ones3 = _head_ones3(W)
            r, v, a, kd, bv, lw, g, bonus = _rwkv_prep(pa, rwkv_mu[e], _pack3(w_lora), vecs, ones3)
            yf, yb = _rwkv_scan(r, v, a, kd, bv, lw, n_ctx // RWKV_CHUNK)
            lam, bmat, cmat = _s5_discretise(s5_a_re[e], s5_a_im[e], s5_log_step[e], s5_b_re[e], s5_b_im[e],
                                             s5_c_re[e], s5_c_im[e])
            sf, sb = _s5_scan(pb, lam, bmat, cmat, n_ctx // S5_STEPS)
            vecs_out = jnp.stack([rwkv_lnx_g[e], rwkv_lnx_b[e], s5_d[e], s5_glu_b[e]] + [jnp.zeros((W,), F32)] * 4)
            xs = _ab_out(xs, mr, yf, yb, g, bonus, sf, sb, pb, vecs_out, s5_glu_w[e].astype(BF16),
                         ab_out_w[e].astype(BF16), ones3, bsz, n_lat_tiles)
            rows_mod0 = 0
        else:
            o = l // 2
            cos_t, sin_t = _rope_tables(n_lat, grid_w)
            qkv = _attn_in_proj(xs, mr, norm_g[l, 1], attn_in_w[o].astype(BF16), cos_t, sin_t,
                                n_ctx_tiles, n_lat_tiles)
            att = _attention(qkv, attn_sink[o], bsz, n_lat, n_ctx)
            if last:
                xs = _attn_out(xs, mr, att, attn_out_w[o].astype(BF16), n_ctx_tiles)
                rows_mod0 = n_ctx_tiles
            else:
                raise NotImplementedError("context update after an attention layer")
        xs = _ffn(xs, mr, norm_g[l, 2], w1[l, 1], w2[l, 1], final_g, mod0=6, row_tile_of_mod=rows_mod0,
                  final=last)
    return xs.reshape(bsz, n_lat, D_MODEL)
```

```python
import functools
import math

import jax
import jax.numpy as jnp
from jax import lax
from jax.experimental import pallas as pl
from jax.experimental.pallas import tpu as pltpu

F32 = jnp.float32
BF16 = jnp.bfloat16

D_MODEL = 1024
D_FF = 2816
N_MOD = 9
HEAD_DIM = 64
RMS_EPS = 1e-6
GN_EPS = 64e-5
RWKV_WIDTH = 512
RWKV_PROJ = 1792
S5_WIDTH = 512
S5_GROUP_CH = 16
S5_GROUPS = 32
S5_STATE = 64
ATTN_HEADS = 16
ATTN_KV_HEADS = 4
ATTN_BLOCK = 128
ATTN_WINDOW = 128
ATTN_Q_W = 1024
ATTN_KV_W = 256
ROPE_BASE = 10000.0
NEG_INF = -1e30

ROW_TILE = 256
FFN_ROW_TILE = 512
FFN_F_CHUNKS = (1280, 1536)
RWKV_CHUNK = 64
S5_STEPS = 64
VMEM_LIMIT = 56 * 1024 * 1024

_NN = (((1,), (0,)), ((), ()))
_NT = (((1,), (1,)), ((), ()))


def _dot(a, b, dims=_NN):
    return lax.dot_general(a, b, dims, preferred_element_type=F32)


def _split2(a):
    hi = a.astype(BF16)
    lo = (a - hi.astype(F32)).astype(BF16)
    return hi, lo


def _split3(a):
    hi = a.astype(BF16)
    r1 = a - hi.astype(F32)
    mid = r1.astype(BF16)
    lo = (r1 - mid.astype(F32)).astype(BF16)
    return hi, mid, lo


def _mm1(a, b, dims=_NN):
    return _dot(a.astype(BF16), b.astype(BF16), dims)


def _mm3(a, b, dims=_NN):
    ah, al = _split2(a)
    bh, bl = _split2(b)
    return _dot(ah, bh, dims) + (_dot(ah, bl, dims) + _dot(al, bh, dims))


def _pack3(w):
    hi, lo = _split2(w)
    return jnp.concatenate([hi, lo, hi], axis=0)


def _dot3_packed(a, w3):
    hi, lo = _split2(a)
    return _dot(jnp.concatenate([hi, hi, lo], axis=1), w3)


def _head_sum(a, ones3):
    hi, mid, lo = _split3(a)
    return _dot(jnp.concatenate([hi, mid, lo], axis=1), ones3)


def _sigmoid(x):
    return 1.0 / (1.0 + jnp.exp(-x))


def _silu(x):
    return x * _sigmoid(x)


def _params(*sem):
    return pltpu.CompilerParams(dimension_semantics=sem, vmem_limit_bytes=VMEM_LIMIT)


def _modnorm(x, g, shift, scale):
    xn = x * lax.rsqrt(jnp.mean(x * x, axis=-1, keepdims=True) + RMS_EPS)
    return (xn * g) * (1.0 + scale) + shift


def _head_ones3(width):
    r = lax.broadcasted_iota(jnp.int32, (width, width), 0) >> 6
    c = lax.broadcasted_iota(jnp.int32, (width, width), 1) >> 6
    ones = (r == c).astype(BF16)
    return jnp.concatenate([ones, ones, ones], axis=0)


def _mod_kernel(c_ref, w_ref, b_ref, o_ref):
    o_ref[...] = _mm3(c_ref[...], w_ref[...]) + b_ref[...]


def _mod_vectors(cs, mod_w, mod_b):
    depth = mod_w.shape[0]
    n = mod_w.shape[2]
    tn = 1152
    return pl.pallas_call(
        _mod_kernel,
        grid=(depth, n // tn),
        in_specs=[pl.BlockSpec((8, D_MODEL), lambda l, j: (0, 0)),
                  pl.BlockSpec((None, D_MODEL, tn), lambda l, j: (l, 0, j)),
                  pl.BlockSpec((None, 1, tn), lambda l, j: (l, 0, j))],
        out_specs=pl.BlockSpec((None, 8, tn), lambda l, j: (l, 0, j)),
        out_shape=jax.ShapeDtypeStruct((depth, 8, n), F32),
        compiler_params=_params("parallel", "parallel"),
        name="mod_vectors",
    )(cs, mod_w, mod_b.reshape(depth, 1, n))


def _ffn_kernel(x_ref, xc_ref, att_ref, wo_ref, m_ref, g_ref, w1_ref, w2_ref, fg_ref, o_ref, *,
                mod0, final, n_ctx_tiles, attn_residual):
    x = x_ref[...]
    if n_ctx_tiles:
        x = jnp.where(pl.program_id(0) < n_ctx_tiles, xc_ref[...], x)
    if attn_residual:
        x = x + m_ref[5:6, :] * _dot(att_ref[...].astype(BF16), wo_ref[...])
    h = _modnorm(x, g_ref[...], m_ref[mod0:mod0 + 1, :], m_ref[mod0 + 1:mod0 + 2, :]).astype(BF16)
    acc = None
    f0 = 0
    for fc in FFN_F_CHUNKS:
        gate = _dot(h, w1_ref[:, f0:f0 + fc])
        up = _dot(h, w1_ref[:, D_FF + f0:D_FF + f0 + fc])
        part = _dot((_silu(gate) * up).astype(BF16), w2_ref[f0:f0 + fc, :])
        acc = part if acc is None else acc + part
        f0 += fc
    y = x + (0.5 * m_ref[mod0 + 2:mod0 + 3, :]) * acc
    if final:
        y = y * lax.rsqrt(jnp.mean(y * y, axis=-1, keepdims=True) + RMS_EPS) * fg_ref[...]
    o_ref[...] = y


def _ffn(x, modrows, g, w1, w2, lj, final_g, *, mod0, row_tile_of_mod, final=False, x_ctx=None, attn=None):
    tm = FFN_ROW_TILE
    step = tm // ROW_TILE
    nct = 0 if x_ctx is None else x_ctx.shape[0] // tm
    x_off = 0
    rows = x.shape[0] + nct * tm
    once = pl.Buffered(1)
    small = pl.BlockSpec((8, D_MODEL), lambda i: (0, 0))
    if x_ctx is None:
        x_ctx, ctx_spec = x, small
    else:
        ctx_spec = pl.BlockSpec((tm, D_MODEL), lambda i: (jnp.minimum(i, nct - 1), 0))
    if attn is None:
        att, w_out, att_spec, wo_spec = x, x, small, small
    else:
        att, w_out = attn
        rows = att.shape[0]
        x_off = row_tile_of_mod // step
        att_spec = pl.BlockSpec((tm, D_MODEL), lambda i: (i, 0))
        wo_spec = pl.BlockSpec(w_out.shape, lambda i: (0, 0), pipeline_mode=once)
    layer, which = lj
    return pl.pallas_call(
        functools.partial(_ffn_kernel, mod0=mod0, final=final, n_ctx_tiles=nct, attn_residual=attn is not None),
        grid=(rows // tm,),
        in_specs=[pl.BlockSpec((tm, D_MODEL), lambda i: (x_off + jnp.maximum(i - nct, 0), 0)),
                  ctx_spec, att_spec, wo_spec,
                  pl.BlockSpec((None, N_MOD, D_MODEL), lambda i: (row_tile_of_mod + i * step, 0, 0)),
                  pl.BlockSpec((1, D_MODEL), lambda i: (0, 0)),
                  pl.BlockSpec((None, None) + w1.shape[2:], lambda i: (layer, which, 0, 0), pipeline_mode=once),
                  pl.BlockSpec((None, None) + w2.shape[2:], lambda i: (layer, which, 0, 0), pipeline_mode=once),
                  pl.BlockSpec((1, D_MODEL), lambda i: (0, 0))],
        out_specs=pl.BlockSpec((tm, D_MODEL), lambda i: (i, 0)),
        out_shape=jax.ShapeDtypeStruct((rows, D_MODEL), F32),
        compiler_params=_params("parallel"),
        name="ffn",
    )(x, x_ctx, att, w_out, modrows, g.reshape(1, D_MODEL), w1, w2, final_g.reshape(1, D_MODEL))


def _x_tile(bsz, n_lat_tiles):
    return lambda b, k: (jnp.where(k == 0, b, bsz + b * n_lat_tiles + k - 1), 0)


def _ab_in_kernel(x_ref, m_ref, g_ref, w_ref, pa_ref, pb_ref):
    h = _modnorm(x_ref[...], g_ref[...], m_ref[3:4, :], m_ref[4:5, :]).astype(BF16)
    p = _dot(h, w_ref[...])
    pa_ref[...] = p[:, :RWKV_PROJ]
    pb_ref[...] = p[:, RWKV_PROJ:]


def _ab_in_proj(x, modrows, g, w, bsz, n_lat_tiles):
    nt = n_lat_tiles + 1
    seq = nt * ROW_TILE
    xt = _x_tile(bsz, n_lat_tiles)
    return pl.pallas_call(
        _ab_in_kernel,
        grid=(bsz, nt),
        in_specs=[pl.BlockSpec((ROW_TILE, D_MODEL), xt),
                  pl.BlockSpec((None, N_MOD, D_MODEL), lambda b, k: (xt(b, k)[0], 0, 0)),
                  pl.BlockSpec((1, D_MODEL), lambda b, k: (0, 0)),
                  pl.BlockSpec(w.shape, lambda b, k: (0, 0))],
        out_specs=[pl.BlockSpec((None, ROW_TILE, RWKV_PROJ), lambda b, k: (b, k, 0)),
                   pl.BlockSpec((None, ROW_TILE, S5_WIDTH), lambda b, k: (b, k, 0))],
        out_shape=[jax.ShapeDtypeStruct((bsz, seq, RWKV_PROJ), F32),
                   jax.ShapeDtypeStruct((bsz, seq, S5_WIDTH), F32)],
        compiler_params=_params("parallel", "parallel"),
        name="ab_in_proj",
    )(x, modrows, g.reshape(1, D_MODEL), w)


def _rwkv_prep_kernel(cur_ref, prev_ref, next_ref, mu_ref, wl_ref, vec_ref, ones_ref,
                      r_ref, v_ref, a_ref, k_ref, b_ref, lw_ref, g_ref, bonus_ref):
    k_idx = pl.program_id(1)
    nt = pl.num_programs(1)
    cur = cur_ref[...]
    rows = cur.shape[0]
    row = lax.broadcasted_iota(jnp.int32, cur.shape, 0)
    use_prev = k_idx >= 2
    use_next = jnp.logical_and(k_idx >= 1, k_idx < nt - 1)
    prev_row = jnp.where(use_prev, prev_ref[7:8, :], 0.0)
    next_row = jnp.where(use_next, next_ref[0:1, :], 0.0)
    prev = jnp.where(row == 0, prev_row, pltpu.roll(cur, 1, axis=0))
    nxt = jnp.where(row == rows - 1, next_row, pltpu.roll(cur, rows - 1, axis=0))
    p = cur + mu_ref[0:1, :] * (prev - cur) + mu_ref[1:2, :] * (nxt - cur)

    W = RWKV_WIDTH
    r, k, v = p[:, :W], p[:, W:2 * W], p[:, 2 * W:3 * W]
    lora_in = p[:, 3 * W:]
    lane = lax.broadcasted_iota(jnp.int32, lora_in.shape, 1)
    lora_act = jnp.where(lane < 64, jnp.tanh(lora_in), jnp.where(lane < 128, lora_in, _sigmoid(lora_in)))
    lo = _dot3_packed(lora_act, wl_ref[...])

    k_k, k_a, r_k = vec_ref[0:1, :], vec_ref[1:2, :], vec_ref[2:3, :]
    ones = ones_ref[...]
    kk = k * k_k
    kk = kk * lax.rsqrt(jnp.maximum(_head_sum(kk * kk, ones), 1e-12))
    r_ref[...] = r
    v_ref[...] = v
    a_ref[...] = -kk
    k_sum = jnp.zeros_like(k)
    for d in range(2):
        z = -(vec_ref[3 + d:4 + d, :] + lo[:, d * W:(d + 1) * W])
        softplus = jnp.maximum(z, 0.0) + jnp.log(1.0 + jnp.exp(-jnp.abs(z)))
        lw_ref[d] = -jnp.exp(-softplus - 0.5)
        a = _sigmoid(vec_ref[5 + d:6 + d, :] + lo[:, (2 + d) * W:(3 + d) * W])
        kd = k * (1.0 + (a - 1.0) * k_a)
        k_ref[d] = kd
        b_ref[d] = kk * a
        k_sum = k_sum + kd
    g_ref[...] = lo[:, 4 * W:]
    bonus_ref[...] = _head_sum(r * (0.5 * k_sum) * r_k, ones) * v


def _rwkv_prep(pa, mu, w_lora, vecs, ones):
    bsz, seq, _ = pa.shape
    nt = seq // ROW_TILE
    sub = ROW_TILE // 8
    last_blk = seq // 8 - 1
    W = RWKV_WIDTH
    one = jax.ShapeDtypeStruct((bsz, seq, W), F32)
    two = jax.ShapeDtypeStruct((2, bsz, seq, W), F32)
    spec1 = pl.BlockSpec((None, ROW_TILE, W), lambda b, k: (b, k, 0))
    spec2 = pl.BlockSpec((2, None, ROW_TILE, W), lambda b, k: (0, b, k, 0))
    return pl.pallas_call(
        _rwkv_prep_kernel,
        grid=(bsz, nt),
        in_specs=[pl.BlockSpec((None, ROW_TILE, RWKV_PROJ), lambda b, k: (b, k, 0)),
                  pl.BlockSpec((None, 8, RWKV_PROJ), lambda b, k: (b, jnp.maximum(k * sub - 1, 0), 0)),
                  pl.BlockSpec((None, 8, RWKV_PROJ), lambda b, k: (b, jnp.minimum((k + 1) * sub, last_blk), 0)),
                  pl.BlockSpec(mu.shape, lambda b, k: (0, 0)),
                  pl.BlockSpec(w_lora.shape, lambda b, k: (0, 0)),
                  pl.BlockSpec(vecs.shape, lambda b, k: (0, 0)),
                  pl.BlockSpec(ones.shape, lambda b, k: (0, 0))],
        out_specs=[spec1, spec1, spec1, spec2, spec2, spec2, spec1, spec1],
        out_shape=[one, one, one, two, two, two, one, one],
        compiler_params=_params("parallel", "parallel"),
        name="rwkv_prep",
    )(pa, pa, pa, mu, w_lora, vecs, ones)


RWKV_TILE_HEADS = 2
RWKV_TILE = RWKV_TILE_HEADS * HEAD_DIM
RWKV_BATCHES_PER_STEP = 4


def _rwkv_scaled(fwd, r, v, a, k, b, lw):
    C = RWKV_CHUNK
    rr = lax.broadcasted_iota(jnp.int32, (C, C), 0)
    cc = lax.broadcasted_iota(jnp.int32, (C, C), 1)
    tri = ((rr >= cc) if fwd else (rr <= cc)).astype(BF16)
    hi, mid, lo = _split3(lw)
    cum = _dot(jnp.concatenate([tri, tri, tri], axis=1), jnp.concatenate([hi, mid, lo], axis=0))
    last = cum[C - 1:C, :] if fwd else cum[0:1, :]
    einv = jnp.exp(-cum)
    to_end = jnp.exp(last - cum)
    return dict(at=a * jnp.exp(cum - lw), rt=r * jnp.exp(cum), bt=b * einv, kt=k * einv,
                b_end=b * to_end, k_end=k * to_end, p_end=jnp.exp(last), v=v)


def _rwkv_chunks(dirs, h_ref, y_refs):
    C = RWKV_CHUNK
    TW = RWKV_TILE
    tq = lax.broadcasted_iota(jnp.int32, (C, TW), 0)
    sq = lax.broadcasted_iota(jnp.int32, (C, TW), 1) & (C - 1)
    eye_cat = (tq == sq).astype(F32)
    rb = lax.broadcasted_iota(jnp.int32, (TW, TW), 0)
    cb = lax.broadcasted_iota(jnp.int32, (TW, TW), 1)
    same_head = (rb >> 6) == (cb >> 6)
    diag = rb == cb

    def stack(x):
        return jnp.where(same_head, jnp.concatenate([x] * RWKV_TILE_HEADS, axis=0), jnp.zeros((), x.dtype))

    chains = []
    for di, (fwd, op) in enumerate(dirs):
        strict = (tq > sq) if fwd else (tq < sq)
        incl = (tq >= sq) if fwd else (tq <= sq)
        for q in range(RWKV_WIDTH // TW):
            sl = slice(TW * q, TW * (q + 1))
            ch = {name: val[:, sl] for name, val in op.items()}
            ch.update(strict=strict, incl=incl, di=di, q=q, sl=sl)
            chains.append(ch)

    for ch in chains:
        lhs = jnp.concatenate([ch["at"], ch["rt"]], axis=0).astype(BF16)
        rhs = jnp.concatenate([stack(ch["bt"].astype(BF16)), stack(ch["kt"].astype(BF16))], axis=0)
        s_all = _dot(lhs, rhs, _NT)
        ch["n_ab"] = jnp.where(ch["strict"], s_all[:C, :TW], 0.0)
        ch["n_ak"] = jnp.where(ch["strict"], s_all[:C, TW:], 0.0)
        ch["n_rb"] = jnp.where(ch["incl"], s_all[C:, :TW], 0.0)
        ch["n_rk"] = jnp.where(ch["incl"], s_all[C:, TW:], 0.0)
        ch["tinv"] = eye_cat + ch["n_ab"]
        ch["pw"] = ch["n_ab"]

    levels = int(math.log2(C))
    for i in range(levels):
        for ch in chains:
            pw_b = ch["pw"].astype(BF16)
            if i == 0:
                ch["pw"] = _dot(pw_b, stack(pw_b))
                v_s = stack(ch["v"].astype(BF16))
                ch["kv"] = _dot(jnp.concatenate([ch["n_ak"], ch["n_rk"]], axis=0).astype(BF16), v_s)
            elif i == levels - 1:
                ch["tinv"] = ch["tinv"] + _dot(pw_b, stack(ch["tinv"].astype(BF16)))
            else:
                res = _dot(pw_b, jnp.concatenate([stack(pw_b), stack(ch["tinv"].astype(BF16))], axis=1))
                ch["pw"] = res[:, :TW]
                ch["tinv"] = ch["tinv"] + res[:, TW:]

    for ch in chains:
        nh, nl = _split2(ch["n_ab"])
        xh, xl = _split2(ch["tinv"])
        ch["xh"] = xh
        ch["resid"] = (eye_cat - ch["tinv"]) + _dot(jnp.concatenate([nh, nh, nl], axis=1),
                                                     jnp.concatenate([stack(xh), stack(xl), stack(xh)], axis=0))
    for ch in chains:
        ch["tinv"] = ch["tinv"] + _dot(ch["xh"], stack(ch["resid"].astype(BF16)))

    for ch in chains:
        rhs = jnp.concatenate([stack(ch["at"].astype(BF16)), stack(ch["kv"][:C].astype(BF16))], axis=1)
        ch["aw"] = _dot(ch["tinv"].astype(BF16), rhs).astype(BF16)

    for ch in chains:
        aw = ch["aw"]
        c1 = _dot(ch["n_rb"].astype(BF16), jnp.concatenate([stack(aw[:, :TW]), stack(aw[:, TW:])], axis=1))
        ch["qm"] = ch["rt"] + c1[:, :TW]
        ch["z"] = c1[:, TW:] + ch["kv"][C:]
        bk_t = jnp.concatenate([ch["b_end"], ch["k_end"]], axis=0).T.astype(BF16)
        low = jnp.concatenate([jnp.zeros((C, TW), BF16), ch["v"].astype(BF16)], axis=1)
        c2 = _dot(bk_t, jnp.concatenate([aw, low], axis=0))
        ch["m_upd"] = jnp.where(same_head, c2[:, :TW], 0.0)
        ch["g_upd"] = jnp.where(same_head, c2[:, TW:], 0.0)

    for ch in chains:
        h = h_ref[ch["di"], ch["q"]]
        d = _dot(jnp.concatenate([ch["m_upd"], ch["qm"]], axis=0).astype(BF16), h.astype(BF16))
        y_refs[ch["di"]][:, ch["sl"]] = d[TW:] + ch["z"]
        p_col = jnp.sum(jnp.where(diag, ch["p_end"], 0.0), axis=1, keepdims=True)
        h_ref[ch["di"], ch["q"]] = p_col * h + d[:TW] + ch["g_upd"]


def _rwkv_scan_kernel(rf_ref, vf_ref, af_ref, rb_ref, vb_ref, ab_ref, kf_ref, bf_ref, lwf_ref,
                      kb_ref, bb_ref, lwb_ref, yf_ref, yb_ref, h_ref, *, nb):
    @pl.when(pl.program_id(1) == 0)
    def _():
        h_ref[...] = jnp.zeros_like(h_ref)

    dirs, y_refs = [], []
    for i in range(nb):
        dirs.append((True, _rwkv_scaled(True, rf_ref[i], vf_ref[i], af_ref[i], kf_ref[i], bf_ref[i], lwf_ref[i])))
        dirs.append((False, _rwkv_scaled(False, rb_ref[i], vb_ref[i], ab_ref[i], kb_ref[i], bb_ref[i], lwb_ref[i])))
        y_refs += [yf_ref.at[i], yb_ref.at[i]]
    _rwkv_chunks(dirs, h_ref, y_refs)


def _rwkv_scan(r, v, a, k, b, lw, n_ctx_chunks):
    bsz, seq, W = r.shape
    C = RWKV_CHUNK
    nch = seq // C

    def back(j):
        return jnp.where(j < n_ctx_chunks, n_ctx_chunks - 1 - j, nch - 1 - (j - n_ctx_chunks))

    nb = math.gcd(bsz, RWKV_BATCHES_PER_STEP)
    one_f = pl.BlockSpec((nb, C, W), lambda g, j: (g, j, 0))
    one_b = pl.BlockSpec((nb, C, W), lambda g, j: (g, back(j), 0))
    two_f = pl.BlockSpec((None, nb, C, W), lambda g, j: (0, g, j, 0))
    two_b = pl.BlockSpec((None, nb, C, W), lambda g, j: (1, g, back(j), 0))
    out = jax.ShapeDtypeStruct((bsz, seq, W), F32)
    return pl.pallas_call(
        functools.partial(_rwkv_scan_kernel, nb=nb),
        grid=(bsz // nb, nch),
        in_specs=[one_f, one_f, one_f, one_b, one_b, one_b, two_f, two_f, two_f, two_b, two_b, two_b],
        out_specs=[one_f, one_b],
        out_shape=[out, out],
        scratch_shapes=[pltpu.VMEM((2 * nb, W // RWKV_TILE, RWKV_TILE, RWKV_TILE), F32)],
        compiler_params=_params("parallel", "arbitrary"),
        name="rwkv_scan",
    )(r, v, a, r, v, a, k, b, lw, k, b, lw)


def _s5_scan_kernel(uf_ref, ub_ref, lam_ref, bmat_ref, cmat_ref, yf_ref, yb_ref, bu_ref, st_ref, *, bsz):
    S = S5_STEPS
    half = 8 * S5_STATE
    n_nat = 2 * bsz * S

    @pl.when(pl.program_id(0) == 0)
    def _():
        st_ref[...] = jnp.zeros_like(st_ref)

    def perm_mask(rho, col):
        t, s = rho >> 3, rho & 7
        src = jnp.where(s < 4, s * S + t, bsz * S + (s - 4) * S + (S - 1 - t))
        return jnp.logical_and((s & 3) < bsz, col == src).astype(BF16)

    perm = perm_mask(lax.broadcasted_iota(jnp.int32, (S * 8, n_nat), 0),
                     lax.broadcasted_iota(jnp.int32, (S * 8, n_nat), 1))
    perm_t = perm_mask(lax.broadcasted_iota(jnp.int32, (n_nat, S * 8), 1),
                       lax.broadcasted_iota(jnp.int32, (n_nat, S * 8), 0))
    u_nat = jnp.concatenate([uf_ref[...].reshape(bsz * S, S5_WIDTH), ub_ref[...].reshape(bsz * S, S5_WIDTH)], axis=0)
    u = _dot(perm, u_nat.astype(BF16))
    is_fwd = (lax.broadcasted_iota(jnp.int32, (S * 8, 128), 0) & 7) < 4
    for q in range(S5_WIDTH // 128):
        uq = u[:, 128 * q:128 * (q + 1)]
        lhs = jnp.concatenate([jnp.where(is_fwd, uq, 0.0), jnp.where(is_fwd, 0.0, uq)], axis=1)
        bu_ref[:, 2 * half * q:2 * half * (q + 1)] = _dot(lhs.astype(BF16), bmat_ref[q])

    nq = S5_WIDTH // 128
    lam = lam_ref[...]

    def step(t, carry):
        rows = pl.ds(pl.multiple_of(t * 8, 8), 8)
        new = []
        for q in range(nq):
            base = 2 * half * q
            ar, ai = lam[:, base:base + half], lam[:, base + half:base + 2 * half]
            xr, xi = carry[2 * q], carry[2 * q + 1]
            nr = ar * xr - ai * xi + bu_ref[rows, base:base + half]
            ni = ar * xi + ai * xr + bu_ref[rows, base + half:base + 2 * half]
            bu_ref[rows, base:base + half] = nr
            bu_ref[rows, base + half:base + 2 * half] = ni
            new += [nr, ni]
        return tuple(new)

    init = tuple(st_ref[:, half * j:half * (j + 1)] for j in range(2 * nq))
    final = lax.fori_loop(0, S, step, init, unroll=2)
    for j in range(2 * nq):
        st_ref[:, half * j:half * (j + 1)] = final[j]

    ys = []
    for q in range(nq):
        xq = bu_ref[:, 2 * half * q:2 * half * (q + 1)]
        y2 = _dot(xq.astype(BF16), cmat_ref[q])
        ys.append(jnp.where(is_fwd, y2[:, :128], y2[:, 128:]))
    hi, lo = _split2(jnp.concatenate(ys, axis=1))
    y_nat = _dot(jnp.concatenate([perm_t, perm_t], axis=1), jnp.concatenate([hi, lo], axis=0))
    yf_ref[...] = y_nat[:bsz * S].reshape(bsz, S, S5_WIDTH)
    yb_ref[...] = y_nat[bsz * S:].reshape(bsz, S, S5_WIDTH)


def _s5_scan(u, lam, bmat, cmat, n_ctx_blocks):
    bsz, seq, _ = u.shape
    S = S5_STEPS
    nblk = seq // S
    nstate = 2 * S5_GROUPS * S5_STATE

    def back(i):
        return jnp.where(i < n_ctx_blocks, n_ctx_blocks - 1 - i, nblk - 1 - (i - n_ctx_blocks))

    spec_f = pl.BlockSpec((bsz, S, S5_WIDTH), lambda i: (0, i, 0))
    spec_b = pl.BlockSpec((bsz, S, S5_WIDTH), lambda i: (0, back(i), 0))
    out = jax.ShapeDtypeStruct(u.shape, F32)
    return pl.pallas_call(
        functools.partial(_s5_scan_kernel, bsz=bsz),
        grid=(nblk,),
        in_specs=[spec_f, spec_b,
                  pl.BlockSpec(lam.shape, lambda i: (0, 0)),
                  pl.BlockSpec(bmat.shape, lambda i: (0, 0, 0)),
                  pl.BlockSpec(cmat.shape, lambda i: (0, 0, 0))],
        out_specs=[spec_f, spec_b],
        out_shape=[out, out],
        scratch_shapes=[pltpu.VMEM((S * 8, nstate), F32), pltpu.VMEM((8, nstate), F32)],
        compiler_params=_params("arbitrary"),
        name="s5_scan",
    )(u, u, lam, bmat, cmat)


def _s5_discretise(a_re, a_im, log_step, b_re, b_im, c_re, c_im):
    lam_re = jnp.minimum(a_re, -1e-4)
    lam_im = a_im
    dt = jnp.exp(log_step)[..., None]
    mag = jnp.exp(lam_re * dt)
    ab_re, ab_im = mag * jnp.cos(lam_im * dt), mag * jnp.sin(lam_im * dt)
    den = lam_re * lam_re + lam_im * lam_im
    f_re = ((ab_re - 1.0) * lam_re + ab_im * lam_im) / den
    f_im = (ab_im * lam_re - (ab_re - 1.0) * lam_im) / den
    bb_re = f_re[..., None] * b_re - f_im[..., None] * b_im
    bb_im = f_re[..., None] * b_im + f_im[..., None] * b_re
    nq = S5_WIDTH // 128
    eye8 = jnp.eye(8, dtype=F32)

    def lanes(t):
        return t.reshape(2, nq, 8 * S5_STATE)

    lam = jnp.concatenate([lanes(ab_re), lanes(ab_im)], axis=-1).reshape(2, nq * 16 * S5_STATE)
    lam = jnp.repeat(lam, 4, axis=0)

    def in_block(t):
        t = t.reshape(2, nq, 8, S5_STATE, S5_GROUP_CH)
        return jnp.einsum('dqgpi,gh->dqgihp', t, eye8).reshape(2, nq, 128, 8 * S5_STATE)

    bmat = jnp.concatenate([in_block(bb_re), in_block(bb_im)], axis=-1)
    bmat = jnp.concatenate([bmat[0], bmat[1]], axis=1)

    def out_block(t):
        t = t.reshape(2, nq, 8, S5_GROUP_CH, S5_STATE)
        return jnp.einsum('dqgip,gh->dqgphi', t, eye8).reshape(2, nq, 8 * S5_STATE, 128)

    cmat = jnp.concatenate([out_block(c_re), -out_block(c_im)], axis=2)
    cmat = jnp.concatenate([cmat[0], cmat[1]], axis=-1)
    return lam, bmat.astype(BF16), cmat.astype(BF16)


def _ab_out_kernel(x_ref, m_ref, yf_ref, yb_ref, g_ref, bonus_ref, sf_ref, sb_ref, u_ref, vec_ref, gluw_ref, w_ref,
                   ones_ref, o_ref):
    W = RWKV_WIDTH
    ones = ones_ref[...]
    y = yf_ref[...] + yb_ref[...]
    mu = _head_sum(y, ones) * (1.0 / HEAD_DIM)
    yc = y - mu
    var = _head_sum(yc * yc, ones) * (1.0 / HEAD_DIM)
    ya = (yc * lax.rsqrt(var + GN_EPS) * vec_ref[0:1, :] + vec_ref[1:2, :] + bonus_ref[...]) * g_ref[...]

    s = (sf_ref[...] + sb_ref[...]) + vec_ref[2:3, :] * u_ref[...]
    z = 0.5 * s * (1.0 + jnp.tanh(math.sqrt(2.0 / math.pi) * (s + 0.044715 * (s * s * s))))
    yb = z * _sigmoid(_mm1(z, gluw_ref[...]) + vec_ref[3:4, :])
    out = _dot(ya.astype(BF16), w_ref[:W, :]) + _dot(yb.astype(BF16), w_ref[W:, :])
    o_ref[...] = x_ref[...] + m_ref[5:6, :] * out


def _ab_out(x, modrows, yf, yb, g, bonus, sf, sb, u, vecs, glu_w, out_w, ones, bsz, n_lat_tiles):
    nt = n_lat_tiles + 1
    W = RWKV_WIDTH
    xt = _x_tile(bsz, n_lat_tiles)
    seq_spec = pl.BlockSpec((None, ROW_TILE, W), lambda b, k: (b, k, 0))
    return pl.pallas_call(
        _ab_out_kernel,
        grid=(bsz, nt),
        in_specs=[pl.BlockSpec((ROW_TILE, D_MODEL), xt),
                  pl.BlockSpec((None, N_MOD, D_MODEL), lambda b, k: (xt(b, k)[0], 0, 0)),
                  seq_spec, seq_spec, seq_spec, seq_spec, seq_spec, seq_spec, seq_spec,
                  pl.BlockSpec(vecs.shape, lambda b, k: (0, 0)),
                  pl.BlockSpec(glu_w.shape, lambda b, k: (0, 0)),
                  pl.BlockSpec(out_w.shape, lambda b, k: (0, 0)),
                  pl.BlockSpec(ones.shape, lambda b, k: (0, 0))],
        out_specs=pl.BlockSpec((ROW_TILE, D_MODEL), xt),
        out_shape=jax.ShapeDtypeStruct(x.shape, F32),
        compiler_params=_params("parallel", "parallel"),
        name="ab_out",
    )(x, modrows, yf, yb, g, bonus, sf, sb, u, vecs, glu_w, out_w, ones)


def _attn_in_kernel(x_ref, m_ref, g_ref, w_ref, cos_ref, sin_ref, o_ref):
    h = _modnorm(x_ref[...], g_ref[...], m_ref[3:4, :], m_ref[4:5, :]).astype(BF16)
    p = _dot(h, w_ref[...])
    qk_w = ATTN_Q_W + ATTN_KV_W
    qk = p[:, :qk_w]
    half = HEAD_DIM // 2
    lane = lax.broadcasted_iota(jnp.int32, qk.shape, 1)
    first = (lane & (HEAD_DIM - 1)) < half
    partner = jnp.where(first, pltpu.roll(qk, qk_w - half, axis=1), pltpu.roll(qk, half, axis=1))
    reps = qk_w // 128
    cos = jnp.concatenate([cos_ref[...]] * reps, axis=1)
    sin = jnp.concatenate([sin_ref[...]] * reps, axis=1)
    o_ref[:, :qk_w] = qk * cos + partner * sin
    o_ref[:, qk_w:] = p[:, qk_w:]


def _attn_in_proj(x, modrows, g, w, cos_t, sin_t, n_ctx_tiles, n_lat_tiles):
    rows = x.shape[0]
    width = w.shape[1]

    def rope_tile(i):
        return (jnp.where(i < n_ctx_tiles, 0, 1 + (i - n_ctx_tiles) % n_lat_tiles), 0)

    return pl.pallas_call(
        _attn_in_kernel,
        grid=(rows // ROW_TILE,),
        in_specs=[pl.BlockSpec((ROW_TILE, D_MODEL), lambda i: (i, 0)),
                  pl.BlockSpec((None, N_MOD, D_MODEL), lambda i: (i, 0, 0)),
                  pl.BlockSpec((1, D_MODEL), lambda i: (0, 0)),
                  pl.BlockSpec(w.shape, lambda i: (0, 0)),
                  pl.BlockSpec((ROW_TILE, 128), rope_tile),
                  pl.BlockSpec((ROW_TILE, 128), rope_tile)],
        out_specs=pl.BlockSpec((ROW_TILE, width), lambda i: (i, 0)),
        out_shape=jax.ShapeDtypeStruct((rows, width), F32),
        compiler_params=_params("parallel"),
        name="attn_in_proj",
    )(x, modrows, g.reshape(1, D_MODEL), w, cos_t, sin_t)


def _rope_tables(n_lat, grid_w):
    half = HEAD_DIM // 2
    t = jnp.arange(n_lat)
    row_id = (t // grid_w).astype(F32)
    col_id = (t % grid_w).astype(F32)
    inv_freq = ROPE_BASE ** (-jnp.arange(0, half, 2, dtype=F32) / half)
    ang = jnp.concatenate([row_id[:, None] * inv_freq, col_id[:, None] * inv_freq], axis=-1)
    cos, sin = jnp.cos(ang), jnp.sin(ang)
    cos_t = jnp.concatenate([cos, cos, cos, cos], axis=-1)
    sin_t = jnp.concatenate([-sin, sin, -sin, sin], axis=-1)
    cos_t = jnp.concatenate([jnp.ones((ROW_TILE, 128), F32), cos_t], axis=0)
    sin_t = jnp.concatenate([jnp.zeros((ROW_TILE, 128), F32), sin_t], axis=0)
    return cos_t, sin_t


def _attn_kernel(sink_ref, q_ref, kp_ref, kc_ref, kn_ref, vp_ref, vc_ref, vn_ref, kx_ref, vx_ref, o_ref, *, nblk):
    i = pl.program_id(1)
    Q = ATTN_BLOCK
    G = ATTN_HEADS // ATTN_KV_HEADS
    k_all = jnp.concatenate([kp_ref[...], kc_ref[...], kn_ref[...], kx_ref[...]], axis=0).astype(BF16)
    v_all = jnp.concatenate([vp_ref[...], vc_ref[...], vn_ref[...], vx_ref[...]], axis=0).astype(BF16)
    nk = k_all.shape[0]
    qi = lax.broadcasted_iota(jnp.int32, (G * Q, nk), 0) & (Q - 1)
    cj = lax.broadcasted_iota(jnp.int32, (G * Q, nk), 1)
    mj = cj - Q
    blk = jnp.where(cj < Q, i - 1, jnp.where(cj < 2 * Q, i, i + 1))
    valid = jnp.logical_and(jnp.abs(mj - qi) <= ATTN_WINDOW, jnp.logical_and(blk >= 0, blk < nblk))
    valid = jnp.logical_or(valid, cj >= 3 * Q)
    q = q_ref[...] * (HEAD_DIM ** -0.5)
    row_head = lax.broadcasted_iota(jnp.int32, (G * Q, 1), 0) >> 7
    def scores(kh):
        q4 = jnp.concatenate([q[:, (kh * G + g) * HEAD_DIM:(kh * G + g + 1) * HEAD_DIM] for g in range(G)], axis=0)
        return _dot(q4.astype(BF16), k_all[:, kh * HEAD_DIM:(kh + 1) * HEAD_DIM], _NT)

    nxt = scores(0)
    for kh in range(ATTN_KV_HEADS):
        cur = nxt
        if kh + 1 < ATTN_KV_HEADS:
            nxt = scores(kh + 1)
        s = jnp.where(valid, cur, NEG_INF)
        sink = jnp.zeros((G * Q, 1), F32)
        for g in range(G):
            sink = jnp.where(row_head == g, sink_ref[kh * G + g], sink)
        m = jnp.maximum(jnp.max(s, axis=-1, keepdims=True), sink)
        pr = jnp.exp(s - m)
        den = jnp.sum(pr, axis=-1, keepdims=True) + jnp.exp(sink - m)
        o = _dot(pr.astype(BF16), v_all[:, kh * HEAD_DIM:(kh + 1) * HEAD_DIM]) / den
        for g in range(G):
            hq = kh * G + g
            o_ref[:, hq * HEAD_DIM:(hq + 1) * HEAD_DIM] = o[g * Q:(g + 1) * Q]


def _attention(qkv, sink, bsz, n_lat, n_ctx):
    Q = ATTN_BLOCK
    nblk = n_lat // Q
    lat0 = bsz * n_ctx // Q
    kcol, vcol = ATTN_Q_W // ATTN_KV_W, ATTN_Q_W // ATTN_KV_W + 1

    def kv_spec(col, off):
        return pl.BlockSpec((Q, ATTN_KV_W),
                            lambda b, i: (lat0 + b * nblk + jnp.clip(i + off, 0, nblk - 1), col))

    def ctx_spec(col):
        return pl.BlockSpec((n_ctx, ATTN_KV_W), lambda b, i: (b, col))

    return pl.pallas_call(
        functools.partial(_attn_kernel, nblk=nblk),
        grid=(bsz, nblk),
        in_specs=[pl.BlockSpec(memory_space=pltpu.SMEM),
                  pl.BlockSpec((Q, ATTN_Q_W), lambda b, i: (lat0 + b * nblk + i, 0)),
                  kv_spec(kcol, -1), kv_spec(kcol, 0), kv_spec(kcol, 1),
                  kv_spec(vcol, -1), kv_spec(vcol, 0), kv_spec(vcol, 1),
                  ctx_spec(kcol), ctx_spec(vcol)],
        out_specs=pl.BlockSpec((Q, ATTN_Q_W), lambda b, i: (b * nblk + i, 0)),
        out_shape=jax.ShapeDtypeStruct((bsz * n_lat, ATTN_Q_W), F32),
        compiler_params=_params("parallel", "parallel"),
        name="window_attention",
    )(sink, qkv, qkv, qkv, qkv, qkv, qkv, qkv, qkv, qkv)


def kernel(x, c, ctx, c_ctx, norm_g, mod_w, mod_b, ffn_w1, ffn_w2, ab_in_w, ab_out_w, rwkv_mu, rwkv_w0, rwkv_w2, rwkv_a0, rwkv_a2, rwkv_g2, rwkv_k_k, rwkv_k_a, rwkv_r_k, rwkv_lnx_g, rwkv_lnx_b, s5_a_re, s5_a_im, s5_log_step, s5_b_re, s5_b_im, s5_c_re, s5_c_im, s5_d, s5_glu_w, s5_glu_b, attn_in_w, attn_out_w, attn_sink, final_g):
    bsz, n_lat, _ = x.shape
    n_ctx = ctx.shape[1]
    depth = mod_w.shape[0]
    grid_w = 64
    assert n_ctx == ROW_TILE and n_lat % FFN_ROW_TILE == 0 and (bsz * n_ctx) % FFN_ROW_TILE == 0 and bsz <= 4
    seq = n_ctx + n_lat
    n_ctx_tiles = bsz
    n_lat_tiles = n_lat // ROW_TILE
    W = RWKV_WIDTH

    cs = jnp.zeros((8, D_MODEL), F32).at[:bsz].set(c).at[bsz].set(c_ctx)
    mods = _mod_vectors(jax.nn.silu(cs), mod_w, mod_b).reshape(depth, 8, N_MOD, D_MODEL)
    tile_row = jnp.concatenate([jnp.full((n_ctx_tiles,), bsz, jnp.int32),
                                jnp.repeat(jnp.arange(bsz, dtype=jnp.int32), n_lat_tiles)])
    modrows = mods[:, tile_row]

    xs = x.reshape(bsz * n_lat, D_MODEL)
    w1 = ffn_w1.astype(BF16)
    w2 = ffn_w2.astype(BF16)

    for l in range(depth):
        last = l == depth - 1
        mr = modrows[l]
        xs = _ffn(xs, mr, norm_g[l, 0], w1, w2, (l, 0), final_g, mod0=0, row_tile_of_mod=0,
                  x_ctx=ctx.reshape(bsz * n_ctx, D_MODEL) if l == 0 else None)
        if l % 2 == 0:
            e = l // 2
            pa, pb = _ab_in_proj(xs, mr, norm_g[l, 1], ab_in_w[e].astype(BF16), bsz, n_lat_tiles)
            zeros = jnp.zeros((64, W), F32)
            w_lora = jnp.concatenate([
                jnp.concatenate([rwkv_w2[e, 0], rwkv_w2[e, 1], zeros, zeros, zeros], axis=1),
                jnp.concatenate([zeros, zeros, rwkv_a2[e, 0], rwkv_a2[e, 1], zeros], axis=1),
                jnp.concatenate([jnp.zeros((128, 4 * W), F32), rwkv_g2[e]], axis=1)], axis=0)
            vecs = jnp.stack([rwkv_k_k[e], rwkv_k_a[e], rwkv_r_k[e].reshape(W), rwkv_w0[e, 0], rwkv_w0[e, 1],
                              rwkv_a0[e, 0], rwkv_a0[e, 1], jnp.zeros((W,), F32)])
            ones3 = _head_ones3(W)
            r, v, a, kd, bv, lw, g, bonus = _rwkv_prep(pa, rwkv_mu[e], _pack3(w_lora), vecs, ones3)
            yf, yb = _rwkv_scan(r, v, a, kd, bv, lw, n_ctx // RWKV_CHUNK)
            lam, bmat, cmat = _s5_discretise(s5_a_re[e], s5_a_im[e], s5_log_step[e], s5_b_re[e], s5_b_im[e],
                                             s5_c_re[e], s5_c_im[e])
            sf, sb = _s5_scan(pb, lam, bmat, cmat, n_ctx // S5_STEPS)
            vecs_out = jnp.stack([rwkv_lnx_g[e], rwkv_lnx_b[e], s5_d[e], s5_glu_b[e]] + [jnp.zeros((W,), F32)] * 4)
            xs = _ab_out(xs, mr, yf, yb, g, bonus, sf, sb, pb, vecs_out, s5_glu_w[e].astype(BF16),
                         ab_out_w[e].astype(BF16), ones3, bsz, n_lat_tiles)
            rows_mod0, attn = 0, None
        else:
            o = l // 2
            cos_t, sin_t = _rope_tables(n_lat, grid_w)
            qkv = _attn_in_proj(xs, mr, norm_g[l, 1], attn_in_w[o].astype(BF16), cos_t, sin_t,
                                n_ctx_tiles, n_lat_tiles)
            att = _attention(qkv, attn_sink[o], bsz, n_lat, n_ctx)
            if not last:
                raise NotImplementedError("context update after an attention layer")
            rows_mod0, attn = n_ctx_tiles, (att, attn_out_w[o].astype(BF16))
        xs = _ffn(xs, mr, norm_g[l, 2], w1, w2, (l, 1), final_g, mod0=6, row_tile_of_mod=rows_mod0,
                  final=last, attn=attn)
    return xs.reshape(bsz, n_lat, D_MODEL)
```

```python
import functools
import math

import jax
import jax.numpy as jnp
from jax import lax
from jax.experimental import pallas as pl
from jax.experimental.pallas import tpu as pltpu

F32 = jnp.float32
BF16 = jnp.bfloat16

D_MODEL = 1024
D_FF = 2816
N_MOD = 9
HEAD_DIM = 64
RMS_EPS = 1e-6
GN_EPS = 64e-5
RWKV_WIDTH = 512
RWKV_PROJ = 1792
S5_WIDTH = 512
S5_GROUP_CH = 16
S5_GROUPS = 32
S5_STATE = 64
ATTN_HEADS = 16
ATTN_KV_HEADS = 4
ATTN_BLOCK = 128
ATTN_WINDOW = 128
ATTN_Q_W = 1024
ATTN_KV_W = 256
ROPE_BASE = 10000.0
NEG_INF = -1e30

ROW_TILE = 256
FFN_ROW_TILE = 512
FFN_F_CHUNKS = (1280, 1536)
RWKV_CHUNK = 64
S5_STEPS = 64
VMEM_LIMIT = 56 * 1024 * 1024

_NN = (((1,), (0,)), ((), ()))
_NT = (((1,), (1,)), ((), ()))


def _dot(a, b, dims=_NN):
    return lax.dot_general(a, b, dims, preferred_element_type=F32)


def _split2(a):
    hi = a.astype(BF16)
    lo = (a - hi.astype(F32)).astype(BF16)
    return hi, lo


def _split3(a):
    hi = a.astype(BF16)
    r1 = a - hi.astype(F32)
    mid = r1.astype(BF16)
    lo = (r1 - mid.astype(F32)).astype(BF16)
    return hi, mid, lo


def _mm1(a, b, dims=_NN):
    return _dot(a.astype(BF16), b.astype(BF16), dims)


def _mm3(a, b, dims=_NN):
    ah, al = _split2(a)
    bh, bl = _split2(b)
    return _dot(ah, bh, dims) + (_dot(ah, bl, dims) + _dot(al, bh, dims))


def _pack3(w):
    hi, lo = _split2(w)
    return jnp.concatenate([hi, lo, hi], axis=0)


def _dot3_packed(a, w3):
    hi, lo = _split2(a)
    return _dot(jnp.concatenate([hi, hi, lo], axis=1), w3)


def _head_sum(a, ones3):
    hi, mid, lo = _split3(a)
    return _dot(jnp.concatenate([hi, mid, lo], axis=1), ones3)


def _sigmoid(x):
    return 1.0 / (1.0 + jnp.exp(-x))


def _silu(x):
    return x * _sigmoid(x)


def _params(*sem):
    return pltpu.CompilerParams(dimension_semantics=sem, vmem_limit_bytes=VMEM_LIMIT)


def _modnorm(x, g, shift, scale):
    xn = x * lax.rsqrt(jnp.mean(x * x, axis=-1, keepdims=True) + RMS_EPS)
    return (xn * g) * (1.0 + scale) + shift


def _head_ones3(width):
    r = lax.broadcasted_iota(jnp.int32, (width, width), 0) >> 6
    c = lax.broadcasted_iota(jnp.int32, (width, width), 1) >> 6
    ones = (r == c).astype(BF16)
    return jnp.concatenate([ones, ones, ones], axis=0)


def _mod_kernel(c_ref, w_ref, b_ref, o_ref):
    o_ref[...] = _mm3(c_ref[...], w_ref[...]) + b_ref[...]


def _mod_vectors(cs, mod_w, mod_b):
    depth = mod_w.shape[0]
    n = mod_w.shape[2]
    tn = 1152
    return pl.pallas_call(
        _mod_kernel,
        grid=(depth, n // tn),
        in_specs=[pl.BlockSpec((8, D_MODEL), lambda l, j: (0, 0)),
                  pl.BlockSpec((None, D_MODEL, tn), lambda l, j: (l, 0, j)),
                  pl.BlockSpec((None, 1, tn), lambda l, j: (l, 0, j))],
        out_specs=pl.BlockSpec((None, 8, tn), lambda l, j: (l, 0, j)),
        out_shape=jax.ShapeDtypeStruct((depth, 8, n), F32),
        compiler_params=_params("parallel", "parallel"),
        name="mod_vectors",
    )(cs, mod_w, mod_b.reshape(depth, 1, n))


def _ffn_kernel(x_ref, xc_ref, att_ref, wo_ref, m_ref, g_ref, w1_ref, w2_ref, fg_ref, o_ref, *,
                mod0, final, n_ctx_tiles, attn_residual):
    x = x_ref[...]
    if n_ctx_tiles:
        x = jnp.where(pl.program_id(0) < n_ctx_tiles, xc_ref[...], x)
    if attn_residual:
        x = x + m_ref[5:6, :] * _dot(att_ref[...].astype(BF16), wo_ref[...])
    h = _modnorm(x, g_ref[...], m_ref[mod0:mod0 + 1, :], m_ref[mod0 + 1:mod0 + 2, :]).astype(BF16)
    acc = None
    f0 = 0
    for fc in FFN_F_CHUNKS:
        gate = _dot(h, w1_ref[:, f0:f0 + fc])
        up = _dot(h, w1_ref[:, D_FF + f0:D_FF + f0 + fc])
        part = _dot((_silu(gate) * up).astype(BF16), w2_ref[f0:f0 + fc, :])
        acc = part if acc is None else acc + part
        f0 += fc
    y = x + (0.5 * m_ref[mod0 + 2:mod0 + 3, :]) * acc
    if final:
        y = y * lax.rsqrt(jnp.mean(y * y, axis=-1, keepdims=True) + RMS_EPS) * fg_ref[...]
    o_ref[...] = y


def _ffn(x, modrows, g, w1, w2, lj, final_g, *, mod0, row_tile_of_mod, final=False, x_ctx=None, attn=None):
    tm = FFN_ROW_TILE
    step = tm // ROW_TILE
    nct = 0 if x_ctx is None else x_ctx.shape[0] // tm
    x_off = 0
    rows = x.shape[0] + nct * tm
    once = pl.Buffered(1)
    small = pl.BlockSpec((8, D_MODEL), lambda i: (0, 0))
    if x_ctx is None:
        x_ctx, ctx_spec = x, small
    else:
        ctx_spec = pl.BlockSpec((tm, D_MODEL), lambda i: (jnp.minimum(i, nct - 1), 0))
    if attn is None:
        att, w_out, att_spec, wo_spec = x, x, small, small
    else:
        att, w_out = attn
        rows = att.shape[0]
        x_off = row_tile_of_mod // step
        att_spec = pl.BlockSpec((tm, D_MODEL), lambda i: (i, 0))
        wo_spec = pl.BlockSpec(w_out.shape, lambda i: (0, 0), pipeline_mode=once)
    layer, which = lj
    return pl.pallas_call(
        functools.partial(_ffn_kernel, mod0=mod0, final=final, n_ctx_tiles=nct, attn_residual=attn is not None),
        grid=(rows // tm,),
        in_specs=[pl.BlockSpec((tm, D_MODEL), lambda i: (x_off + jnp.maximum(i - nct, 0), 0)),
                  ctx_spec, att_spec, wo_spec,
                  pl.BlockSpec((None, N_MOD, D_MODEL), lambda i: (row_tile_of_mod + i * step, 0, 0)),
                  pl.BlockSpec((1, D_MODEL), lambda i: (0, 0)),
                  pl.BlockSpec((None, None) + w1.shape[2:], lambda i: (layer, which, 0, 0), pipeline_mode=once),
                  pl.BlockSpec((None, None) + w2.shape[2:], lambda i: (layer, which, 0, 0), pipeline_mode=once),
                  pl.BlockSpec((1, D_MODEL), lambda i: (0, 0))],
        out_specs=pl.BlockSpec((tm, D_MODEL), lambda i: (i, 0)),
        out_shape=jax.ShapeDtypeStruct((rows, D_MODEL), F32),
        compiler_params=_params("parallel"),
        name="ffn",
    )(x, x_ctx, att, w_out, modrows, g.reshape(1, D_MODEL), w1, w2, final_g.reshape(1, D_MODEL))


def _x_tile(bsz, n_lat_tiles):
    return lambda b, k: (jnp.where(k == 0, b, bsz + b * n_lat_tiles + k - 1), 0)


def _ab_in_kernel(x_ref, xp_ref, xn_ref, m_ref, g_ref, w_ref, mu_ref, wl_ref, vec_ref, ones_ref,
                  pb_ref, r_ref, v_ref, a_ref, k_ref, b_ref, lw_ref, g_out_ref, bonus_ref):
    k_idx = pl.program_id(1)
    nt = pl.num_programs(1)
    xa = jnp.concatenate([xp_ref[...], x_ref[...], xn_ref[...]], axis=0)
    h = _modnorm(xa, g_ref[...], m_ref[3:4, :], m_ref[4:5, :]).astype(BF16)
    p_all = _dot(h, w_ref[...])
    rows = x_ref.shape[0]
    pb_ref[...] = p_all[8:8 + rows, RWKV_PROJ:]
    cur = p_all[8:8 + rows, :RWKV_PROJ]
    row = lax.broadcasted_iota(jnp.int32, cur.shape, 0)
    use_prev = k_idx >= 2
    use_next = jnp.logical_and(k_idx >= 1, k_idx < nt - 1)
    prev_row = jnp.where(use_prev, p_all[7:8, :RWKV_PROJ], 0.0)
    next_row = jnp.where(use_next, p_all[8 + rows:9 + rows, :RWKV_PROJ], 0.0)
    prev = jnp.where(row == 0, prev_row, pltpu.roll(cur, 1, axis=0))
    nxt = jnp.where(row == rows - 1, next_row, pltpu.roll(cur, rows - 1, axis=0))
    p = cur + mu_ref[0:1, :] * (prev - cur) + mu_ref[1:2, :] * (nxt - cur)

    W = RWKV_WIDTH
    r, k, v = p[:, :W], p[:, W:2 * W], p[:, 2 * W:3 * W]
    lora_in = p[:, 3 * W:]
    lane = lax.broadcasted_iota(jnp.int32, lora_in.shape, 1)
    lora_act = jnp.where(lane < 64, jnp.tanh(lora_in), jnp.where(lane < 128, lora_in, _sigmoid(lora_in)))
    lo = _dot3_packed(lora_act, wl_ref[...])

    k_k, k_a, r_k = vec_ref[0:1, :], vec_ref[1:2, :], vec_ref[2:3, :]
    ones = ones_ref[...]
    kk = k * k_k
    kk = kk * lax.rsqrt(jnp.maximum(_head_sum(kk * kk, ones), 1e-12))
    r_ref[...] = r
    v_ref[...] = v
    a_ref[...] = -kk
    k_sum = jnp.zeros_like(k)
    for d in range(2):
        z = -(vec_ref[3 + d:4 + d, :] + lo[:, d * W:(d + 1) * W])
        softplus = jnp.maximum(z, 0.0) + jnp.log(1.0 + jnp.exp(-jnp.abs(z)))
        lw_ref[d] = -jnp.exp(-softplus - 0.5)
        a = _sigmoid(vec_ref[5 + d:6 + d, :] + lo[:, (2 + d) * W:(3 + d) * W])
        kd = k * (1.0 + (a - 1.0) * k_a)
        k_ref[d] = kd
        b_ref[d] = kk * a
        k_sum = k_sum + kd
    g_out_ref[...] = lo[:, 4 * W:]
    bonus_ref[...] = _head_sum(r * (0.5 * k_sum) * r_k, ones) * v


def _ab_in_proj(x, modrows, g, w, mu, w_lora, vecs, ones, bsz, n_lat_tiles):
    nt = n_lat_tiles + 1
    seq = nt * ROW_TILE
    sub = ROW_TILE // 8
    last_blk = x.shape[0] // 8 - 1
    xt = _x_tile(bsz, n_lat_tiles)
    W = RWKV_WIDTH
    one = jax.ShapeDtypeStruct((bsz, seq, W), F32)
    two = jax.ShapeDtypeStruct((2, bsz, seq, W), F32)
    spec1 = pl.BlockSpec((None, ROW_TILE, W), lambda b, k: (b, k, 0))
    spec2 = pl.BlockSpec((2, None, ROW_TILE, W), lambda b, k: (0, b, k, 0))
    const = lambda b, k: (0, 0)
    return pl.pallas_call(
        _ab_in_kernel,
        grid=(bsz, nt),
        in_specs=[pl.BlockSpec((ROW_TILE, D_MODEL), xt),
                  pl.BlockSpec((8, D_MODEL), lambda b, k: (jnp.maximum(xt(b, k)[0] * sub - 1, 0), 0)),
                  pl.BlockSpec((8, D_MODEL), lambda b, k: (jnp.minimum((xt(b, k)[0] + 1) * sub, last_blk), 0)),
                  pl.BlockSpec((None, N_MOD, D_MODEL), lambda b, k: (xt(b, k)[0], 0, 0)),
                  pl.BlockSpec((1, D_MODEL), const),
                  pl.BlockSpec(w.shape, const),
                  pl.BlockSpec(mu.shape, const),
                  pl.BlockSpec(w_lora.shape, const),
                  pl.BlockSpec(vecs.shape, const),
                  pl.BlockSpec(ones.shape, const)],
        out_specs=[spec1, spec1, spec1, spec1, spec2, spec2, spec2, spec1, spec1],
        out_shape=[one, one, one, one, two, two, two, one, one],
        compiler_params=_params("parallel", "parallel"),
        name="ab_in_proj",
    )(x, x, x, modrows, g.reshape(1, D_MODEL), w, mu, w_lora, vecs, ones)


RWKV_TILE_HEADS = 2
RWKV_TILE = RWKV_TILE_HEADS * HEAD_DIM
RWKV_BATCHES_PER_STEP = 4


def _rwkv_scaled(fwd, r, v, a, k, b, lw):
    C = RWKV_CHUNK
    rr = lax.broadcasted_iota(jnp.int32, (C, C), 0)
    cc = lax.broadcasted_iota(jnp.int32, (C, C), 1)
    tri = ((rr >= cc) if fwd else (rr <= cc)).astype(BF16)
    hi, mid, lo = _split3(lw)
    cum = _dot(jnp.concatenate([tri, tri, tri], axis=1), jnp.concatenate([hi, mid, lo], axis=0))
    last = cum[C - 1:C, :] if fwd else cum[0:1, :]
    einv = jnp.exp(-cum)
    to_end = jnp.exp(last - cum)
    return dict(at=a * jnp.exp(cum - lw), rt=r * jnp.exp(cum), bt=b * einv, kt=k * einv,
                b_end=b * to_end, k_end=k * to_end, p_end=jnp.exp(last), v=v)


def _rwkv_chunks(dirs, h_ref, y_refs):
    C = RWKV_CHUNK
    TW = RWKV_TILE
    tq = lax.broadcasted_iota(jnp.int32, (C, TW), 0)
    sq = lax.broadcasted_iota(jnp.int32, (C, TW), 1) & (C - 1)
    eye_cat = (tq == sq).astype(F32)
    rb = lax.broadcasted_iota(jnp.int32, (TW, TW), 0)
    cb = lax.broadcasted_iota(jnp.int32, (TW, TW), 1)
    same_head = (rb >> 6) == (cb >> 6)
    diag = rb == cb

    def stack(x):
        return jnp.where(same_head, jnp.concatenate([x] * RWKV_TILE_HEADS, axis=0), jnp.zeros((), x.dtype))

    chains = []
    for di, (fwd, op) in enumerate(dirs):
        strict = (tq > sq) if fwd else (tq < sq)
        incl = (tq >= sq) if fwd else (tq <= sq)
        for q in range(RWKV_WIDTH // TW):
            sl = slice(TW * q, TW * (q + 1))
            ch = {name: val[:, sl] for name, val in op.items()}
            ch.update(strict=strict, incl=incl, di=di, q=q, sl=sl)
            chains.append(ch)

    for ch in chains:
        lhs = jnp.concatenate([ch["at"], ch["rt"]], axis=0).astype(BF16)
        rhs = jnp.concatenate([stack(ch["bt"].astype(BF16)), stack(ch["kt"].astype(BF16))], axis=0)
        s_all = _dot(lhs, rhs, _NT)
        ch["n_ab"] = jnp.where(ch["strict"], s_all[:C, :TW], 0.0)
        ch["n_ak"] = jnp.where(ch["strict"], s_all[:C, TW:], 0.0)
        ch["n_rb"] = jnp.where(ch["incl"], s_all[C:, :TW], 0.0)
        ch["n_rk"] = jnp.where(ch["incl"], s_all[C:, TW:], 0.0)
        ch["tinv"] = eye_cat + ch["n_ab"]
        ch["pw"] = ch["n_ab"]

    levels = int(math.log2(C))
    for i in range(levels):
        for ch in chains:
            pw_b = ch["pw"].astype(BF16)
            if i == 0:
                ch["pw"] = _dot(pw_b, stack(pw_b))
                v_s = stack(ch["v"].astype(BF16))
                ch["kv"] = _dot(jnp.concatenate([ch["n_ak"], ch["n_rk"]], axis=0).astype(BF16), v_s)
            elif i == levels - 1:
                ch["tinv"] = ch["tinv"] + _dot(pw_b, stack(ch["tinv"].astype(BF16)))
            else:
                res = _dot(pw_b, jnp.concatenate([stack(pw_b), stack(ch["tinv"].astype(BF16))], axis=1))
                ch["pw"] = res[:, :TW]
                ch["tinv"] = ch["tinv"] + res[:, TW:]

    for ch in chains:
        nh, nl = _split2(ch["n_ab"])
        xh, xl = _split2(ch["tinv"])
        ch["xh"] = xh
        ch["resid"] = (eye_cat - ch["tinv"]) + _dot(jnp.concatenate([nh, nh, nl], axis=1),
                                                     jnp.concatenate([stack(xh), stack(xl), stack(xh)], axis=0))
    for ch in chains:
        ch["tinv"] = ch["tinv"] + _dot(ch["xh"], stack(ch["resid"].astype(BF16)))

    for ch in chains:
        rhs = jnp.concatenate([stack(ch["at"].astype(BF16)), stack(ch["kv"][:C].astype(BF16))], axis=1)
        ch["aw"] = _dot(ch["tinv"].astype(BF16), rhs).astype(BF16)

    for ch in chains:
        aw = ch["aw"]
        c1 = _dot(ch["n_rb"].astype(BF16), jnp.concatenate([stack(aw[:, :TW]), stack(aw[:, TW:])], axis=1))
        ch["qm"] = ch["rt"] + c1[:, :TW]
        ch["z"] = c1[:, TW:] + ch["kv"][C:]
        bk_t = jnp.concatenate([ch["b_end"], ch["k_end"]], axis=0).T.astype(BF16)
        low = jnp.concatenate([jnp.zeros((C, TW), BF16), ch["v"].astype(BF16)], axis=1)
        c2 = _dot(bk_t, jnp.concatenate([aw, low], axis=0))
        ch["m_upd"] = jnp.where(same_head, c2[:, :TW], 0.0)
        ch["g_upd"] = jnp.where(same_head, c2[:, TW:], 0.0)

    for ch in chains:
        h = h_ref[ch["di"], ch["q"]]
        d = _dot(jnp.concatenate([ch["m_upd"], ch["qm"]], axis=0).astype(BF16), h.astype(BF16))
        y_refs[ch["di"]][:, ch["sl"]] = d[TW:] + ch["z"]
        p_col = jnp.sum(jnp.where(diag, ch["p_end"], 0.0), axis=1, keepdims=True)
        h_ref[ch["di"], ch["q"]] = p_col * h + d[:TW] + ch["g_upd"]


def _rwkv_scan_kernel(rf_ref, vf_ref, af_ref, rb_ref, vb_ref, ab_ref, kf_ref, bf_ref, lwf_ref,
                      kb_ref, bb_ref, lwb_ref, yf_ref, yb_ref, h_ref, *, nb):
    @pl.when(pl.program_id(1) == 0)
    def _():
        h_ref[...] = jnp.zeros_like(h_ref)

    dirs, y_refs = [], []
    for i in range(nb):
        dirs.append((True, _rwkv_scaled(True, rf_ref[i], vf_ref[i], af_ref[i], kf_ref[i], bf_ref[i], lwf_ref[i])))
        dirs.append((False, _rwkv_scaled(False, rb_ref[i], vb_ref[i], ab_ref[i], kb_ref[i], bb_ref[i], lwb_ref[i])))
        y_refs += [yf_ref.at[i], yb_ref.at[i]]
    _rwkv_chunks(dirs, h_ref, y_refs)


def _rwkv_scan(r, v, a, k, b, lw, n_ctx_chunks):
    bsz, seq, W = r.shape
    C = RWKV_CHUNK
    nch = seq // C

    def back(j):
        return jnp.where(j < n_ctx_chunks, n_ctx_chunks - 1 - j, nch - 1 - (j - n_ctx_chunks))

    nb = math.gcd(bsz, RWKV_BATCHES_PER_STEP)
    one_f = pl.BlockSpec((nb, C, W), lambda g, j: (g, j, 0))
    one_b = pl.BlockSpec((nb, C, W), lambda g, j: (g, back(j), 0))
    two_f = pl.BlockSpec((None, nb, C, W), lambda g, j: (0, g, j, 0))
    two_b = pl.BlockSpec((None, nb, C, W), lambda g, j: (1, g, back(j), 0))
    out = jax.ShapeDtypeStruct((bsz, seq, W), F32)
    return pl.pallas_call(
        functools.partial(_rwkv_scan_kernel, nb=nb),
        grid=(bsz // nb, nch),
        in_specs=[one_f, one_f, one_f, one_b, one_b, one_b, two_f, two_f, two_f, two_b, two_b, two_b],
        out_specs=[one_f, one_b],
        out_shape=[out, out],
        scratch_shapes=[pltpu.VMEM((2 * nb, W // RWKV_TILE, RWKV_TILE, RWKV_TILE), F32)],
        compiler_params=_params("parallel", "arbitrary"),
        name="rwkv_scan",
    )(r, v, a, r, v, a, k, b, lw, k, b, lw)


def _s5_scan_kernel(uf_ref, ub_ref, lam_ref, bmat_ref, cmat_ref, yf_ref, yb_ref, bu_ref, st_ref, *, bsz):
    S = S5_STEPS
    half = 8 * S5_STATE
    n_nat = 2 * bsz * S

    @pl.when(pl.program_id(0) == 0)
    def _():
        st_ref[...] = jnp.zeros_like(st_ref)

    def perm_mask(rho, col):
        t, s = rho >> 3, rho & 7
        src = jnp.where(s < 4, s * S + t, bsz * S + (s - 4) * S + (S - 1 - t))
        return jnp.logical_and((s & 3) < bsz, col == src).astype(BF16)

    perm = perm_mask(lax.broadcasted_iota(jnp.int32, (S * 8, n_nat), 0),
                     lax.broadcasted_iota(jnp.int32, (S * 8, n_nat), 1))
    perm_t = perm_mask(lax.broadcasted_iota(jnp.int32, (n_nat, S * 8), 1),
                       lax.broadcasted_iota(jnp.int32, (n_nat, S * 8), 0))
    u_nat = jnp.concatenate([uf_ref[...].reshape(bsz * S, S5_WIDTH), ub_ref[...].reshape(bsz * S, S5_WIDTH)], axis=0)
    u = _dot(perm, u_nat.astype(BF16))
    is_fwd = (lax.broadcasted_iota(jnp.int32, (S * 8, 128), 0) & 7) < 4
    nq = S5_WIDTH // 128
    ys = []
    for q in range(nq):
        uq = u[:, 128 * q:128 * (q + 1)]
        lhs = jnp.concatenate([jnp.where(is_fwd, uq, 0.0), jnp.where(is_fwd, 0.0, uq)], axis=1)
        bu_ref[:, 2 * half * q:2 * half * (q + 1)] = _dot(lhs.astype(BF16), bmat_ref[q])
    for q in range(nq):
        base = 2 * half * q
        re, im = slice(base, base + half), slice(base + half, base + 2 * half)
        ar, ai = lam_ref[:, re], lam_ref[:, im]
        xr, xi = st_ref[:, re], st_ref[:, im]
        for t in range(S):
            rows = slice(8 * t, 8 * t + 8)
            xr, xi = (ar * xr - ai * xi + bu_ref[rows, re], ar * xi + ai * xr + bu_ref[rows, im])
            bu_ref[rows, re] = xr
            bu_ref[rows, im] = xi
        st_ref[:, re] = xr
        st_ref[:, im] = xi
        y2 = _dot(bu_ref[:, base:base + 2 * half].astype(BF16), cmat_ref[q])
        ys.append(jnp.where(is_fwd, y2[:, :128], y2[:, 128:]))
    hi, lo = _split2(jnp.concatenate(ys, axis=1))
    y_nat = _dot(jnp.concatenate([perm_t, perm_t], axis=1), jnp.concatenate([hi, lo], axis=0))
    yf_ref[...] = y_nat[:bsz * S].reshape(bsz, S, S5_WIDTH)
    yb_ref[...] = y_nat[bsz * S:].reshape(bsz, S, S5_WIDTH)


def _s5_scan(u, lam, bmat, cmat, n_ctx_blocks):
    bsz, seq, _ = u.shape
    S = S5_STEPS
    nblk = seq // S
    nstate = 2 * S5_GROUPS * S5_STATE

    def back(i):
        return jnp.where(i < n_ctx_blocks, n_ctx_blocks - 1 - i, nblk - 1 - (i - n_ctx_blocks))

    spec_f = pl.BlockSpec((bsz, S, S5_WIDTH), lambda i: (0, i, 0))
    spec_b = pl.BlockSpec((bsz, S, S5_WIDTH), lambda i: (0, back(i), 0))
    out = jax.ShapeDtypeStruct(u.shape, F32)
    return pl.pallas_call(
        functools.partial(_s5_scan_kernel, bsz=bsz),
        grid=(nblk,),
        in_specs=[spec_f, spec_b,
                  pl.BlockSpec(lam.shape, lambda i: (0, 0)),
                  pl.BlockSpec(bmat.shape, lambda i: (0, 0, 0)),
                  pl.BlockSpec(cmat.shape, lambda i: (0, 0, 0))],
        out_specs=[spec_f, spec_b],
        out_shape=[out, out],
        scratch_shapes=[pltpu.VMEM((S * 8, nstate), F32), pltpu.VMEM((8, nstate), F32)],
        compiler_params=_params("arbitrary"),
        name="s5_scan",
    )(u, u, lam, bmat, cmat)


def _s5_discretise(a_re, a_im, log_step, b_re, b_im, c_re, c_im):
    lam_re = jnp.minimum(a_re, -1e-4)
    lam_im = a_im
    dt = jnp.exp(log_step)[..., None]
    mag = jnp.exp(lam_re * dt)
    ab_re, ab_im = mag * jnp.cos(lam_im * dt), mag * jnp.sin(lam_im * dt)
    den = lam_re * lam_re + lam_im * lam_im
    f_re = ((ab_re - 1.0) * lam_re + ab_im * lam_im) / den
    f_im = (ab_im * lam_re - (ab_re - 1.0) * lam_im) / den
    bb_re = f_re[..., None] * b_re - f_im[..., None] * b_im
    bb_im = f_re[..., None] * b_im + f_im[..., None] * b_re
    nq = S5_WIDTH // 128
    eye8 = jnp.eye(8, dtype=F32)

    def lanes(t):
        return t.reshape(2, nq, 8 * S5_STATE)

    lam = jnp.concatenate([lanes(ab_re), lanes(ab_im)], axis=-1).reshape(2, nq * 16 * S5_STATE)
    lam = jnp.repeat(lam, 4, axis=0)

    def in_block(t):
        t = t.reshape(2, nq, 8, S5_STATE, S5_GROUP_CH)
        return jnp.einsum('dqgpi,gh->dqgihp', t, eye8).reshape(2, nq, 128, 8 * S5_STATE)

    bmat = jnp.concatenate([in_block(bb_re), in_block(bb_im)], axis=-1)
    bmat = jnp.concatenate([bmat[0], bmat[1]], axis=1)

    def out_block(t):
        t = t.reshape(2, nq, 8, S5_GROUP_CH, S5_STATE)
        return jnp.einsum('dqgip,gh->dqgphi', t, eye8).reshape(2, nq, 8 * S5_STATE, 128)

    cmat = jnp.concatenate([out_block(c_re), -out_block(c_im)], axis=2)
    cmat = jnp.concatenate([cmat[0], cmat[1]], axis=-1)
    return lam, bmat.astype(BF16), cmat.astype(BF16)


def _ab_out_kernel(x_ref, m_ref, yf_ref, yb_ref, g_ref, bonus_ref, sf_ref, sb_ref, u_ref, vec_ref, gluw_ref, w_ref,
                   ones_ref, o_ref):
    W = RWKV_WIDTH
    ones = ones_ref[...]
    y = yf_ref[...] + yb_ref[...]
    mu = _head_sum(y, ones) * (1.0 / HEAD_DIM)
    yc = y - mu
    var = _head_sum(yc * yc, ones) * (1.0 / HEAD_DIM)
    ya = (yc * lax.rsqrt(var + GN_EPS) * vec_ref[0:1, :] + vec_ref[1:2, :] + bonus_ref[...]) * g_ref[...]

    s = (sf_ref[...] + sb_ref[...]) + vec_ref[2:3, :] * u_ref[...]
    z = 0.5 * s * (1.0 + jnp.tanh(math.sqrt(2.0 / math.pi) * (s + 0.044715 * (s * s * s))))
    yb = z * _sigmoid(_mm1(z, gluw_ref[...]) + vec_ref[3:4, :])
    out = _dot(ya.astype(BF16), w_ref[:W, :]) + _dot(yb.astype(BF16), w_ref[W:, :])
    o_ref[...] = x_ref[...] + m_ref[5:6, :] * out


def _ab_out(x, modrows, yf, yb, g, bonus, sf, sb, u, vecs, glu_w, out_w, ones, bsz, n_lat_tiles):
    nt = n_lat_tiles + 1
    W = RWKV_WIDTH
    xt = _x_tile(bsz, n_lat_tiles)
    seq_spec = pl.BlockSpec((None, ROW_TILE, W), lambda b, k: (b, k, 0))
    return pl.pallas_call(
        _ab_out_kernel,
        grid=(bsz, nt),
        in_specs=[pl.BlockSpec((ROW_TILE, D_MODEL), xt),
                  pl.BlockSpec((None, N_MOD, D_MODEL), lambda b, k: (xt(b, k)[0], 0, 0)),
                  seq_spec, seq_spec, seq_spec, seq_spec, seq_spec, seq_spec, seq_spec,
                  pl.BlockSpec(vecs.shape, lambda b, k: (0, 0)),
                  pl.BlockSpec(glu_w.shape, lambda b, k: (0, 0)),
                  pl.BlockSpec(out_w.shape, lambda b, k: (0, 0)),
                  pl.BlockSpec(ones.shape, lambda b, k: (0, 0))],
        out_specs=pl.BlockSpec((ROW_TILE, D_MODEL), xt),
        out_shape=jax.ShapeDtypeStruct(x.shape, F32),
        compiler_params=_params("parallel", "parallel"),
        name="ab_out",
    )(x, modrows, yf, yb, g, bonus, sf, sb, u, vecs, glu_w, out_w, ones)


def _attn_in_kernel(x_ref, m_ref, g_ref, w_ref, cos_ref, sin_ref, o_ref):
    h = _modnorm(x_ref[...], g_ref[...], m_ref[3:4, :], m_ref[4:5, :]).astype(BF16)
    p = _dot(h, w_ref[...])
    qk_w = ATTN_Q_W + ATTN_KV_W
    qk = p[:, :qk_w]
    half = HEAD_DIM // 2
    lane = lax.broadcasted_iota(jnp.int32, qk.shape, 1)
    first = (lane & (HEAD_DIM - 1)) < half
    partner = jnp.where(first, pltpu.roll(qk, qk_w - half, axis=1), pltpu.roll(qk, half, axis=1))
    reps = qk_w // 128
    cos = jnp.concatenate([cos_ref[...]] * reps, axis=1)
    sin = jnp.concatenate([sin_ref[...]] * reps, axis=1)
    o_ref[:, :qk_w] = qk * cos + partner * sin
    o_ref[:, qk_w:] = p[:, qk_w:]


def _attn_in_proj(x, modrows, g, w, cos_t, sin_t, n_ctx_tiles, n_lat_tiles):
    rows = x.shape[0]
    width = w.shape[1]

    def rope_tile(i):
        return (jnp.where(i < n_ctx_tiles, 0, 1 + (i - n_ctx_tiles) % n_lat_tiles), 0)

    return pl.pallas_call(
        _attn_in_kernel,
        grid=(rows // ROW_TILE,),
        in_specs=[pl.BlockSpec((ROW_TILE, D_MODEL), lambda i: (i, 0)),
                  pl.BlockSpec((None, N_MOD, D_MODEL), lambda i: (i, 0, 0)),
                  pl.BlockSpec((1, D_MODEL), lambda i: (0, 0)),
                  pl.BlockSpec(w.shape, lambda i: (0, 0)),
                  pl.BlockSpec((ROW_TILE, 128), rope_tile),
                  pl.BlockSpec((ROW_TILE, 128), rope_tile)],
        out_specs=pl.BlockSpec((ROW_TILE, width), lambda i: (i, 0)),
        out_shape=jax.ShapeDtypeStruct((rows, width), F32),
        compiler_params=_params("parallel"),
        name="attn_in_proj",
    )(x, modrows, g.reshape(1, D_MODEL), w, cos_t, sin_t)


def _rope_tables(n_lat, grid_w):
    half = HEAD_DIM // 2
    t = jnp.arange(n_lat)
    row_id = (t // grid_w).astype(F32)
    col_id = (t % grid_w).astype(F32)
    inv_freq = ROPE_BASE ** (-jnp.arange(0, half, 2, dtype=F32) / half)
    ang = jnp.concatenate([row_id[:, None] * inv_freq, col_id[:, None] * inv_freq], axis=-1)
    cos, sin = jnp.cos(ang), jnp.sin(ang)
    cos_t = jnp.concatenate([cos, cos, cos, cos], axis=-1)
    sin_t = jnp.concatenate([-sin, sin, -sin, sin], axis=-1)
    cos_t = jnp.concatenate([jnp.ones((ROW_TILE, 128), F32), cos_t], axis=0)
    sin_t = jnp.concatenate([jnp.zeros((ROW_TILE, 128), F32), sin_t], axis=0)
    return cos_t, sin_t


def _attn_kernel(sink_ref, q_ref, kp_ref, kc_ref, kn_ref, vp_ref, vc_ref, vn_ref, kx_ref, vx_ref, o_ref, *, nblk):
    i = pl.program_id(1)
    Q = ATTN_BLOCK
    G = ATTN_HEADS // ATTN_KV_HEADS
    k_all = jnp.concatenate([kp_ref[...], kc_ref[...], kn_ref[...], kx_ref[...]], axis=0).astype(BF16)
    v_all = jnp.concatenate([vp_ref[...], vc_ref[...], vn_ref[...], vx_ref[...]], axis=0).astype(BF16)
    nk = k_all.shape[0]
    qi = lax.broadcasted_iota(jnp.int32, (G * Q, nk), 0) & (Q - 1)
    cj = lax.broadcasted_iota(jnp.int32, (G * Q, nk), 1)
    mj = cj - Q
    blk = jnp.where(cj < Q, i - 1, jnp.where(cj < 2 * Q, i, i + 1))
    valid = jnp.logical_and(jnp.abs(mj - qi) <= ATTN_WINDOW, jnp.logical_and(blk >= 0, blk < nblk))
    valid = jnp.logical_or(valid, cj >= 3 * Q)
    q = q_ref[...] * (HEAD_DIM ** -0.5)
    row_head = lax.broadcasted_iota(jnp.int32, (G * Q, 1), 0) >> 7
    def scores(kh):
        q4 = jnp.concatenate([q[:, (kh * G + g) * HEAD_DIM:(kh * G + g + 1) * HEAD_DIM] for g in range(G)], axis=0)
        return _dot(q4.astype(BF16), k_all[:, kh * HEAD_DIM:(kh + 1) * HEAD_DIM], _NT)

    nxt = scores(0)
    for kh in range(ATTN_KV_HEADS):
        cur = nxt
        if kh + 1 < ATTN_KV_HEADS:
            nxt = scores(kh + 1)
        s = jnp.where(valid, cur, NEG_INF)
        sink = jnp.zeros((G * Q, 1), F32)
        for g in range(G):
            sink = jnp.where(row_head == g, sink_ref[kh * G + g], sink)
        m = jnp.maximum(jnp.max(s, axis=-1, keepdims=True), sink)
        pr = jnp.exp(s - m)
        den = jnp.sum(pr, axis=-1, keepdims=True) + jnp.exp(sink - m)
        o = _dot(pr.astype(BF16), v_all[:, kh * HEAD_DIM:(kh + 1) * HEAD_DIM]) / den
        for g in range(G):
            hq = kh * G + g
            o_ref[:, hq * HEAD_DIM:(hq + 1) * HEAD_DIM] = o[g * Q:(g + 1) * Q]


def _attention(qkv, sink, bsz, n_lat, n_ctx):
    Q = ATTN_BLOCK
    nblk = n_lat // Q
    lat0 = bsz * n_ctx // Q
    kcol, vcol = ATTN_Q_W // ATTN_KV_W, ATTN_Q_W // ATTN_KV_W + 1

    def kv_spec(col, off):
        return pl.BlockSpec((Q, ATTN_KV_W),
                            lambda b, i: (lat0 + b * nblk + jnp.clip(i + off, 0, nblk - 1), col))

    def ctx_spec(col):
        return pl.BlockSpec((n_ctx, ATTN_KV_W), lambda b, i: (b, col))

    return pl.pallas_call(
        functools.partial(_attn_kernel, nblk=nblk),
        grid=(bsz, nblk),
        in_specs=[pl.BlockSpec(memory_space=pltpu.SMEM),
                  pl.BlockSpec((Q, ATTN_Q_W), lambda b, i: (lat0 + b * nblk + i, 0)),
                  kv_spec(kcol, -1), kv_spec(kcol, 0), kv_spec(kcol, 1),
                  kv_spec(vcol, -1), kv_spec(vcol, 0), kv_spec(vcol, 1),
                  ctx_spec(kcol), ctx_spec(vcol)],
        out_specs=pl.BlockSpec((Q, ATTN_Q_W), lambda b, i: (b * nblk + i, 0)),
        out_shape=jax.ShapeDtypeStruct((bsz * n_lat, ATTN_Q_W), F32),
        compiler_params=_params("parallel", "parallel"),
        name="window_attention",
    )(sink, qkv, qkv, qkv, qkv, qkv, qkv, qkv, qkv, qkv)


def kernel(x, c, ctx, c_ctx, norm_g, mod_w, mod_b, ffn_w1, ffn_w2, ab_in_w, ab_out_w, rwkv_mu, rwkv_w0, rwkv_w2, rwkv_a0, rwkv_a2, rwkv_g2, rwkv_k_k, rwkv_k_a, rwkv_r_k, rwkv_lnx_g, rwkv_lnx_b, s5_a_re, s5_a_im, s5_log_step, s5_b_re, s5_b_im, s5_c_re, s5_c_im, s5_d, s5_glu_w, s5_glu_b, attn_in_w, attn_out_w, attn_sink, final_g):
    bsz, n_lat, _ = x.shape
    n_ctx = ctx.shape[1]
    depth = mod_w.shape[0]
    grid_w = 64
    assert n_ctx == ROW_TILE and n_lat % FFN_ROW_TILE == 0 and (bsz * n_ctx) % FFN_ROW_TILE == 0 and bsz <= 4
    seq = n_ctx + n_lat
    n_ctx_tiles = bsz
    n_lat_tiles = n_lat // ROW_TILE
    W = RWKV_WIDTH

    cs = jnp.zeros((8, D_MODEL), F32).at[:bsz].set(c).at[bsz].set(c_ctx)
    mods = _mod_vectors(jax.nn.silu(cs), mod_w, mod_b).reshape(depth, 8, N_MOD, D_MODEL)
    tile_row = jnp.concatenate([jnp.full((n_ctx_tiles,), bsz, jnp.int32),
                                jnp.repeat(jnp.arange(bsz, dtype=jnp.int32), n_lat_tiles)])
    modrows = mods[:, tile_row]

    xs = x.reshape(bsz * n_lat, D_MODEL)
    w1 = ffn_w1.astype(BF16)
    w2 = ffn_w2.astype(BF16)

    for l in range(depth):
        last = l == depth - 1
        mr = modrows[l]
        xs = _ffn(xs, mr, norm_g[l, 0], w1, w2, (l, 0), final_g, mod0=0, row_tile_of_mod=0,
                  x_ctx=ctx.reshape(bsz * n_ctx, D_MODEL) if l == 0 else None)
        if l % 2 == 0:
            e = l // 2
            zeros = jnp.zeros((64, W), F32)
            w_lora = jnp.concatenate([
                jnp.concatenate([rwkv_w2[e, 0], rwkv_w2[e, 1], zeros, zeros, zeros], axis=1),
                jnp.concatenate([zeros, zeros, rwkv_a2[e, 0], rwkv_a2[e, 1], zeros], axis=1),
                jnp.concatenate([jnp.zeros((128, 4 * W), F32), rwkv_g2[e]], axis=1)], axis=0)
            vecs = jnp.stack([rwkv_k_k[e], rwkv_k_a[e], rwkv_r_k[e].reshape(W), rwkv_w0[e, 0], rwkv_w0[e, 1],
                              rwkv_a0[e, 0], rwkv_a0[e, 1], jnp.zeros((W,), F32)])
            ones3 = _head_ones3(W)
            pb, r, v, a, kd, bv, lw, g, bonus = _ab_in_proj(xs, mr, norm_g[l, 1], ab_in_w[e].astype(BF16), rwkv_mu[e],
                                                            _pack3(w_lora), vecs, ones3, bsz, n_lat_tiles)
            yf, yb = _rwkv_scan(r, v, a, kd, bv, lw, n_ctx // RWKV_CHUNK)
            lam, bmat, cmat = _s5_discretise(s5_a_re[e], s5_a_im[e], s5_log_step[e], s5_b_re[e], s5_b_im[e],
                                             s5_c_re[e], s5_c_im[e])
            sf, sb = _s5_scan(pb, lam, bmat, cmat, n_ctx // S5_STEPS)
            vecs_out = jnp.stack([rwkv_lnx_g[e], rwkv_lnx_b[e], s5_d[e], s5_glu_b[e]] + [jnp.zeros((W,), F32)] * 4)
            xs = _ab_out(xs, mr, yf, yb, g, bonus, sf, sb, pb, vecs_out, s5_glu_w[e].astype(BF16),
                         ab_out_w[e].astype(BF16), ones3, bsz, n_lat_tiles)
            rows_mod0, attn = 0, None
        else:
            o = l // 2
            cos_t, sin_t = _rope_tables(n_lat, grid_w)
            qkv = _attn_in_proj(xs, mr, norm_g[l, 1], attn_in_w[o].astype(BF16), cos_t, sin_t,
                                n_ctx_tiles, n_lat_tiles)
            att = _attention(qkv, attn_sink[o], bsz, n_lat, n_ctx)
            if not last:
                raise NotImplementedError("context update after an attention layer")
            rows_mod0, attn = n_ctx_tiles, (att, attn_out_w[o].astype(BF16))
        xs = _ffn(xs, mr, norm_g[l, 2], w1, w2, (l, 1), final_g, mod0=6, row_tile_of_mod=rows_mod0,
                  final=last, attn=attn)
    return xs.reshape(bsz, n_lat, D_MODEL)
```

```python
import functools
import math

import jax
import jax.numpy as jnp
from jax import lax
from jax.experimental import pallas as pl
from jax.experimental.pallas import tpu as pltpu

F32 = jnp.float32
BF16 = jnp.bfloat16

D_MODEL = 1024
D_FF = 2816
N_MOD = 9
HEAD_DIM = 64
RMS_EPS = 1e-6
GN_EPS = 64e-5
RWKV_WIDTH = 512
RWKV_PROJ = 1792
S5_WIDTH = 512
S5_GROUP_CH = 16
S5_GROUPS = 32
S5_STATE = 64
ATTN_HEADS = 16
ATTN_KV_HEADS = 4
ATTN_BLOCK = 128
ATTN_WINDOW = 128
ATTN_Q_W = 1024
ATTN_KV_W = 256
ROPE_BASE = 10000.0
NEG_INF = -1e30

ROW_TILE = 256
FFN_ROW_TILE = 512
FFN_F_CHUNKS = (1280, 1536)
RWKV_CHUNK = 64
S5_STEPS = 64
VMEM_LIMIT = 56 * 1024 * 1024

_NN = (((1,), (0,)), ((), ()))
_NT = (((1,), (1,)), ((), ()))


def _dot(a, b, dims=_NN):
    return lax.dot_general(a, b, dims, preferred_element_type=F32)


def _split2(a):
    hi = a.astype(BF16)
    lo = (a - hi.astype(F32)).astype(BF16)
    return hi, lo


def _split3(a):
    hi = a.astype(BF16)
    r1 = a - hi.astype(F32)
    mid = r1.astype(BF16)
    lo = (r1 - mid.astype(F32)).astype(BF16)
    return hi, mid, lo


def _mm1(a, b, dims=_NN):
    return _dot(a.astype(BF16), b.astype(BF16), dims)


def _mm3(a, b, dims=_NN):
    ah, al = _split2(a)
    bh, bl = _split2(b)
    return _dot(ah, bh, dims) + (_dot(ah, bl, dims) + _dot(al, bh, dims))


def _pack3(w):
    hi, lo = _split2(w)
    return jnp.concatenate([hi, lo, hi], axis=0)


def _dot3_packed(a, w3):
    hi, lo = _split2(a)
    return _dot(jnp.concatenate([hi, hi, lo], axis=1), w3)


def _head_sum(a, ones3):
    hi, mid, lo = _split3(a)
    return _dot(jnp.concatenate([hi, mid, lo], axis=1), ones3)


def _sigmoid(x):
    return 1.0 / (1.0 + jnp.exp(-x))


def _silu(x):
    return x * _sigmoid(x)


def _params(*sem):
    return pltpu.CompilerParams(dimension_semantics=sem, vmem_limit_bytes=VMEM_LIMIT)


def _modnorm(x, g, shift, scale):
    xn = x * lax.rsqrt(jnp.mean(x * x, axis=-1, keepdims=True) + RMS_EPS)
    return (xn * g) * (1.0 + scale) + shift


def _head_ones3(width):
    r = lax.broadcasted_iota(jnp.int32, (width, width), 0) >> 6
    c = lax.broadcasted_iota(jnp.int32, (width, width), 1) >> 6
    ones = (r == c).astype(BF16)
    return jnp.concatenate([ones, ones, ones], axis=0)


def _mod_kernel(c_ref, w_ref, b_ref, o_ref):
    o_ref[...] = _mm3(c_ref[...], w_ref[...]) + b_ref[...]


def _mod_vectors(cs, mod_w, mod_b):
    depth = mod_w.shape[0]
    n = mod_w.shape[2]
    tn = 1152
    return pl.pallas_call(
        _mod_kernel,
        grid=(depth, n // tn),
        in_specs=[pl.BlockSpec((8, D_MODEL), lambda l, j: (0, 0)),
                  pl.BlockSpec((None, D_MODEL, tn), lambda l, j: (l, 0, j)),
                  pl.BlockSpec((None, 1, tn), lambda l, j: (l, 0, j))],
        out_specs=pl.BlockSpec((None, 8, tn), lambda l, j: (l, 0, j)),
        out_shape=jax.ShapeDtypeStruct((depth, 8, n), F32),
        compiler_params=_params("parallel", "parallel"),
        name="mod_vectors",
    )(cs, mod_w, mod_b.reshape(depth, 1, n))


def _ffn_kernel(x_ref, xc_ref, att_ref, wo_ref, m_ref, g_ref, w1_ref, w2_ref, fg_ref, *rest,
                mod0, final, n_ctx_tiles, attn_residual, qkv_proj):
    if qkv_proj:
        gq_ref, wq_ref, cos_ref, sin_ref, o_ref, qkv_ref = rest
    else:
        (o_ref,) = rest
    x = x_ref[...]
    if n_ctx_tiles:
        x = jnp.where(pl.program_id(0) < n_ctx_tiles, xc_ref[...], x)
    if attn_residual:
        x = x + m_ref[5:6, :] * _dot(att_ref[...].astype(BF16), wo_ref[...])
    h = _modnorm(x, g_ref[...], m_ref[mod0:mod0 + 1, :], m_ref[mod0 + 1:mod0 + 2, :]).astype(BF16)
    acc = None
    f0 = 0
    for fc in FFN_F_CHUNKS:
        gate = _dot(h, w1_ref[:, f0:f0 + fc])
        up = _dot(h, w1_ref[:, D_FF + f0:D_FF + f0 + fc])
        part = _dot((_silu(gate) * up).astype(BF16), w2_ref[f0:f0 + fc, :])
        acc = part if acc is None else acc + part
        f0 += fc
    y = x + (0.5 * m_ref[mod0 + 2:mod0 + 3, :]) * acc
    if final:
        y = y * lax.rsqrt(jnp.mean(y * y, axis=-1, keepdims=True) + RMS_EPS) * fg_ref[...]
    o_ref[...] = y
    if qkv_proj:
        _qkv_rope(y, m_ref, gq_ref, wq_ref, cos_ref, sin_ref, qkv_ref)


def _ffn(x, modrows, g, w1, w2, lj, final_g, *, mod0, row_tile_of_mod, final=False, x_ctx=None, attn=None,
         qkv=None):
    tm = FFN_ROW_TILE
    step = tm // ROW_TILE
    nct = 0 if x_ctx is None else x_ctx.shape[0] // tm
    x_off = 0
    rows = x.shape[0] + nct * tm
    once = pl.Buffered(1)
    small = pl.BlockSpec((8, D_MODEL), lambda i: (0, 0))
    if x_ctx is None:
        x_ctx, ctx_spec = x, small
    else:
        ctx_spec = pl.BlockSpec((tm, D_MODEL), lambda i: (jnp.minimum(i, nct - 1), 0))
    if attn is None:
        att, w_out, att_spec, wo_spec = x, x, small, small
    else:
        att, w_out = attn
        rows = att.shape[0]
        x_off = row_tile_of_mod // step
        att_spec = pl.BlockSpec((tm, D_MODEL), lambda i: (i, 0))
        wo_spec = pl.BlockSpec(w_out.shape, lambda i: (0, 0), pipeline_mode=once)
    layer, which = lj
    operands = [x, x_ctx, att, w_out, modrows, g.reshape(1, D_MODEL), w1, w2, final_g.reshape(1, D_MODEL)]
    in_specs = [pl.BlockSpec((tm, D_MODEL), lambda i: (x_off + jnp.maximum(i - nct, 0), 0)),
                ctx_spec, att_spec, wo_spec,
                pl.BlockSpec((None, N_MOD, D_MODEL), lambda i: (row_tile_of_mod + i * step, 0, 0)),
                pl.BlockSpec((1, D_MODEL), lambda i: (0, 0)),
                pl.BlockSpec((None, None) + w1.shape[2:], lambda i: (layer, which, 0, 0), pipeline_mode=once),
                pl.BlockSpec((None, None) + w2.shape[2:], lambda i: (layer, which, 0, 0), pipeline_mode=once),
                pl.BlockSpec((1, D_MODEL), lambda i: (0, 0))]
    out_specs = [pl.BlockSpec((tm, D_MODEL), lambda i: (i, 0))]
    out_shape = [jax.ShapeDtypeStruct((rows, D_MODEL), F32)]
    if qkv is not None:
        g_mix, w_in, cos_t, sin_t, rope_tile = qkv
        operands += [g_mix.reshape(1, D_MODEL), w_in, cos_t, sin_t]
        in_specs += [pl.BlockSpec((1, D_MODEL), lambda i: (0, 0)),
                     pl.BlockSpec(w_in.shape, lambda i: (0, 0), pipeline_mode=once),
                     pl.BlockSpec((tm, 128), rope_tile), pl.BlockSpec((tm, 128), rope_tile)]
        out_specs.append(pl.BlockSpec((tm, w_in.shape[1]), lambda i: (i, 0)))
        out_shape.append(jax.ShapeDtypeStruct((rows, w_in.shape[1]), F32))
    out = pl.pallas_call(
        functools.partial(_ffn_kernel, mod0=mod0, final=final, n_ctx_tiles=nct, attn_residual=attn is not None,
                          qkv_proj=qkv is not None),
        grid=(rows // tm,),
        in_specs=in_specs,
        out_specs=out_specs,
        out_shape=out_shape,
        compiler_params=_params("parallel"),
        name="ffn",
    )(*operands)
    return out if qkv is not None else out[0]


def _x_tile(bsz, n_lat_tiles):
    return lambda b, k: (jnp.where(k == 0, b, bsz + b * n_lat_tiles + k - 1), 0)


def _ab_in_kernel(x_ref, xp_ref, xn_ref, m_ref, g_ref, w_ref, mu_ref, wl_ref, vec_ref, ones_ref,
                  pb_ref, r_ref, v_ref, a_ref, k_ref, b_ref, lw_ref, g_out_ref, bonus_ref):
    k_idx = pl.program_id(1)
    nt = pl.num_programs(1)
    xa = jnp.concatenate([xp_ref[...], x_ref[...], xn_ref[...]], axis=0)
    h = _modnorm(xa, g_ref[...], m_ref[3:4, :], m_ref[4:5, :]).astype(BF16)
    p_all = _dot(h, w_ref[...])
    rows = x_ref.shape[0]
    pb_ref[...] = p_all[8:8 + rows, RWKV_PROJ:]
    cur = p_all[8:8 + rows, :RWKV_PROJ]
    row = lax.broadcasted_iota(jnp.int32, cur.shape, 0)
    use_prev = k_idx >= 2
    use_next = jnp.logical_and(k_idx >= 1, k_idx < nt - 1)
    prev_row = jnp.where(use_prev, p_all[7:8, :RWKV_PROJ], 0.0)
    next_row = jnp.where(use_next, p_all[8 + rows:9 + rows, :RWKV_PROJ], 0.0)
    prev = jnp.where(row == 0, prev_row, pltpu.roll(cur, 1, axis=0))
    nxt = jnp.where(row == rows - 1, next_row, pltpu.roll(cur, rows - 1, axis=0))
    p = cur + mu_ref[0:1, :] * (prev - cur) + mu_ref[1:2, :] * (nxt - cur)

    W = RWKV_WIDTH
    r, k, v = p[:, :W], p[:, W:2 * W], p[:, 2 * W:3 * W]
    lo_w = _dot3_packed(jnp.tanh(p[:, 3 * W:3 * W + 64]), wl_ref[0:192, :])
    lo_a = _dot3_packed(p[:, 3 * W + 64:3 * W + 128], wl_ref[192:384, :])
    gate = _dot3_packed(_sigmoid(p[:, 3 * W + 128:]), wl_ref[384:768, 0:W])

    k_k, k_a, r_k = vec_ref[0:1, :], vec_ref[1:2, :], vec_ref[2:3, :]
    ones = ones_ref[...]
    kk = k * k_k
    kk = kk * lax.rsqrt(jnp.maximum(_head_sum(kk * kk, ones), 1e-12))
    r_ref[...] = r
    v_ref[...] = v
    a_ref[...] = -kk
    k_sum = jnp.zeros_like(k)
    for d in range(2):
        z = -(vec_ref[3 + d:4 + d, :] + lo_w[:, d * W:(d + 1) * W])
        softplus = jnp.maximum(z, 0.0) + jnp.log(1.0 + jnp.exp(-jnp.abs(z)))
        lw_ref[d] = -jnp.exp(-softplus - 0.5)
        a = _sigmoid(vec_ref[5 + d:6 + d, :] + lo_a[:, d * W:(d + 1) * W])
        kd = k * (1.0 + (a - 1.0) * k_a)
        k_ref[d] = kd
        b_ref[d] = kk * a
        k_sum = k_sum + kd
    g_out_ref[...] = gate
    bonus_ref[...] = _head_sum(r * (0.5 * k_sum) * r_k, ones) * v


def _ab_in_proj(x, modrows, g, w, mu, w_lora, vecs, ones, bsz, n_lat_tiles):
    nt = n_lat_tiles + 1
    seq = nt * ROW_TILE
    sub = ROW_TILE // 8
    last_blk = x.shape[0] // 8 - 1
    xt = _x_tile(bsz, n_lat_tiles)
    W = RWKV_WIDTH
    one = jax.ShapeDtypeStruct((bsz, seq, W), F32)
    two = jax.ShapeDtypeStruct((2, bsz, seq, W), F32)
    spec1 = pl.BlockSpec((None, ROW_TILE, W), lambda b, k: (b, k, 0))
    spec2 = pl.BlockSpec((2, None, ROW_TILE, W), lambda b, k: (0, b, k, 0))
    const = lambda b, k: (0, 0)
    return pl.pallas_call(
        _ab_in_kernel,
        grid=(bsz, nt),
        in_specs=[pl.BlockSpec((ROW_TILE, D_MODEL), xt),
                  pl.BlockSpec((8, D_MODEL), lambda b, k: (jnp.maximum(xt(b, k)[0] * sub - 1, 0), 0)),
                  pl.BlockSpec((8, D_MODEL), lambda b, k: (jnp.minimum((xt(b, k)[0] + 1) * sub, last_blk), 0)),
                  pl.BlockSpec((None, N_MOD, D_MODEL), lambda b, k: (xt(b, k)[0], 0, 0)),
                  pl.BlockSpec((1, D_MODEL), const),
                  pl.BlockSpec(w.shape, const),
                  pl.BlockSpec(mu.shape, const),
                  pl.BlockSpec(w_lora.shape, const),
                  pl.BlockSpec(vecs.shape, const),
                  pl.BlockSpec(ones.shape, const)],
        out_specs=[spec1, spec1, spec1, spec1, spec2, spec2, spec2, spec1, spec1],
        out_shape=[one, one, one, one, two, two, two, one, one],
        compiler_params=_params("parallel", "parallel"),
        name="ab_in_proj",
    )(x, x, x, modrows, g.reshape(1, D_MODEL), w, mu, w_lora, vecs, ones)


RWKV_TILE_HEADS = 2
RWKV_TILE = RWKV_TILE_HEADS * HEAD_DIM
RWKV_BATCHES_PER_STEP = 4


def _rwkv_scaled(fwd, r, v, a, k, b, lw):
    C = RWKV_CHUNK
    rr = lax.broadcasted_iota(jnp.int32, (C, C), 0)
    cc = lax.broadcasted_iota(jnp.int32, (C, C), 1)
    tri = ((rr >= cc) if fwd else (rr <= cc)).astype(BF16)
    hi, mid, lo = _split3(lw)
    cum = _dot(jnp.concatenate([tri, tri, tri], axis=1), jnp.concatenate([hi, mid, lo], axis=0))
    last = cum[C - 1:C, :] if fwd else cum[0:1, :]
    einv = jnp.exp(-cum)
    to_end = jnp.exp(last - cum)
    return dict(at=a * jnp.exp(cum - lw), rt=r * jnp.exp(cum), bt=b * einv, kt=k * einv,
                b_end=b * to_end, k_end=k * to_end, p_end=jnp.exp(last), v=v)


def _rwkv_chunks(dirs, h_ref, y_refs):
    C = RWKV_CHUNK
    TW = RWKV_TILE
    tq = lax.broadcasted_iota(jnp.int32, (C, TW), 0)
    sq = lax.broadcasted_iota(jnp.int32, (C, TW), 1) & (C - 1)
    eye_cat = (tq == sq).astype(F32)
    rb = lax.broadcasted_iota(jnp.int32, (TW, TW), 0)
    cb = lax.broadcasted_iota(jnp.int32, (TW, TW), 1)
    same_head = (rb >> 6) == (cb >> 6)
    diag = rb == cb

    def stack(x):
        return jnp.where(same_head, jnp.concatenate([x] * RWKV_TILE_HEADS, axis=0), jnp.zeros((), x.dtype))

    chains = []
    for di, (fwd, op) in enumerate(dirs):
        strict = (tq > sq) if fwd else (tq < sq)
        incl = (tq >= sq) if fwd else (tq <= sq)
        for q in range(RWKV_WIDTH // TW):
            sl = slice(TW * q, TW * (q + 1))
            ch = {name: val[:, sl] for name, val in op.items()}
            ch.update(strict=strict, incl=incl, di=di, q=q, sl=sl)
            chains.append(ch)

    for ch in chains:
        lhs = jnp.concatenate([ch["at"], ch["rt"]], axis=0).astype(BF16)
        rhs = jnp.concatenate([stack(ch["bt"].astype(BF16)), stack(ch["kt"].astype(BF16))], axis=0)
        s_all = _dot(lhs, rhs, _NT)
        ch["n_ab"] = jnp.where(ch["strict"], s_all[:C, :TW], 0.0)
        ch["n_ak"] = jnp.where(ch["strict"], s_all[:C, TW:], 0.0)
        ch["n_rb"] = jnp.where(ch["incl"], s_all[C:, :TW], 0.0)
        ch["n_rk"] = jnp.where(ch["incl"], s_all[C:, TW:], 0.0)
        ch["tinv"] = eye_cat + ch["n_ab"]
        ch["pw"] = ch["n_ab"]

    levels = int(math.log2(C))
    for i in range(levels):
        for ch in chains:
            pw_b = ch["pw"].astype(BF16)
            if i == 0:
                ch["pw"] = _dot(pw_b, stack(pw_b))
                v_s = stack(ch["v"].astype(BF16))
                ch["kv"] = _dot(jnp.concatenate([ch["n_ak"], ch["n_rk"]], axis=0).astype(BF16), v_s)
            elif i == levels - 1:
                ch["tinv"] = ch["tinv"] + _dot(pw_b, stack(ch["tinv"].astype(BF16)))
            else:
                res = _dot(pw_b, jnp.concatenate([stack(pw_b), stack(ch["tinv"].astype(BF16))], axis=1))
                ch["pw"] = res[:, :TW]
                ch["tinv"] = ch["tinv"] + res[:, TW:]

    for ch in chains:
        nh, nl = _split2(ch["n_ab"])
        xh, xl = _split2(ch["tinv"])
        ch["xh"] = xh
        ch["resid"] = (eye_cat - ch["tinv"]) + _dot(jnp.concatenate([nh, nh, nl], axis=1),
                                                     jnp.concatenate([stack(xh), stack(xl), stack(xh)], axis=0))
    for ch in chains:
        ch["tinv"] = ch["tinv"] + _dot(ch["xh"], stack(ch["resid"].astype(BF16)))

    for ch in chains:
        rhs = jnp.concatenate([stack(ch["at"].astype(BF16)), stack(ch["kv"][:C].astype(BF16))], axis=1)
        ch["aw"] = _dot(ch["tinv"].astype(BF16), rhs).astype(BF16)

    for ch in chains:
        aw = ch["aw"]
        c1 = _dot(ch["n_rb"].astype(BF16), jnp.concatenate([stack(aw[:, :TW]), stack(aw[:, TW:])], axis=1))
        ch["qm"] = ch["rt"] + c1[:, :TW]
        ch["z"] = c1[:, TW:] + ch["kv"][C:]
        bk_t = jnp.concatenate([ch["b_end"], ch["k_end"]], axis=0).T.astype(BF16)
        low = jnp.concatenate([jnp.zeros((C, TW), BF16), ch["v"].astype(BF16)], axis=1)
        c2 = _dot(bk_t, jnp.concatenate([aw, low], axis=0))
        ch["m_upd"] = jnp.where(same_head, c2[:, :TW], 0.0)
        ch["g_upd"] = jnp.where(same_head, c2[:, TW:], 0.0)

    for ch in chains:
        h = h_ref[ch["di"], ch["q"]]
        d = _dot(jnp.concatenate([ch["m_upd"], ch["qm"]], axis=0).astype(BF16), h.astype(BF16))
        y_refs[ch["di"]][:, ch["sl"]] = d[TW:] + ch["z"]
        p_col = jnp.sum(jnp.where(diag, ch["p_end"], 0.0), axis=1, keepdims=True)
        h_ref[ch["di"], ch["q"]] = p_col * h + d[:TW] + ch["g_upd"]


def _rwkv_scan_kernel(rf_ref, vf_ref, af_ref, rb_ref, vb_ref, ab_ref, kf_ref, bf_ref, lwf_ref,
                      kb_ref, bb_ref, lwb_ref, yf_ref, yb_ref, h_ref, *, nb):
    @pl.when(pl.program_id(1) == 0)
    def _():
        h_ref[...] = jnp.zeros_like(h_ref)

    dirs, y_refs = [], []
    for i in range(nb):
        dirs.append((True, _rwkv_scaled(True, rf_ref[i], vf_ref[i], af_ref[i], kf_ref[i], bf_ref[i], lwf_ref[i])))
        dirs.append((False, _rwkv_scaled(False, rb_ref[i], vb_ref[i], ab_ref[i], kb_ref[i], bb_ref[i], lwb_ref[i])))
        y_refs += [yf_ref.at[i], yb_ref.at[i]]
    _rwkv_chunks(dirs, h_ref, y_refs)


def _rwkv_scan(r, v, a, k, b, lw, n_ctx_chunks):
    bsz, seq, W = r.shape
    C = RWKV_CHUNK
    nch = seq // C

    def back(j):
        return jnp.where(j < n_ctx_chunks, n_ctx_chunks - 1 - j, nch - 1 - (j - n_ctx_chunks))

    nb = math.gcd(bsz, RWKV_BATCHES_PER_STEP)
    one_f = pl.BlockSpec((nb, C, W), lambda g, j: (g, j, 0))
    one_b = pl.BlockSpec((nb, C, W), lambda g, j: (g, back(j), 0))
    two_f = pl.BlockSpec((None, nb, C, W), lambda g, j: (0, g, j, 0))
    two_b = pl.BlockSpec((None, nb, C, W), lambda g, j: (1, g, back(j), 0))
    out = jax.ShapeDtypeStruct((bsz, seq, W), F32)
    return pl.pallas_call(
        functools.partial(_rwkv_scan_kernel, nb=nb),
        grid=(bsz // nb, nch),
        in_specs=[one_f, one_f, one_f, one_b, one_b, one_b, two_f, two_f, two_f, two_b, two_b, two_b],
        out_specs=[one_f, one_b],
        out_shape=[out, out],
        scratch_shapes=[pltpu.VMEM((2 * nb, W // RWKV_TILE, RWKV_TILE, RWKV_TILE), F32)],
        compiler_params=_params("parallel", "arbitrary"),
        name="rwkv_scan",
    )(r, v, a, r, v, a, k, b, lw, k, b, lw)


def _s5_scan_kernel(uf_ref, ub_ref, lam_ref, bmat_ref, cmat_ref, yf_ref, yb_ref, bu_ref, st_ref, *, bsz):
    S = S5_STEPS
    half = 8 * S5_STATE
    n_nat = 2 * bsz * S

    @pl.when(pl.program_id(0) == 0)
    def _():
        st_ref[...] = jnp.zeros_like(st_ref)

    def perm_mask(rho, col):
        t, s = rho >> 3, rho & 7
        src = jnp.where(s < 4, s * S + t, bsz * S + (s - 4) * S + (S - 1 - t))
        return jnp.logical_and((s & 3) < bsz, col == src).astype(BF16)

    perm = perm_mask(lax.broadcasted_iota(jnp.int32, (S * 8, n_nat), 0),
                     lax.broadcasted_iota(jnp.int32, (S * 8, n_nat), 1))
    perm_t = perm_mask(lax.broadcasted_iota(jnp.int32, (n_nat, S * 8), 1),
                       lax.broadcasted_iota(jnp.int32, (n_nat, S * 8), 0))
    u_nat = jnp.concatenate([uf_ref[...].reshape(bsz * S, S5_WIDTH), ub_ref[...].reshape(bsz * S, S5_WIDTH)], axis=0)
    u = _dot(perm, u_nat.astype(BF16))
    is_fwd = (lax.broadcasted_iota(jnp.int32, (S * 8, 128), 0) & 7) < 4
    nq = S5_WIDTH // 128
    ys = []
    for q in range(nq):
        uq = u[:, 128 * q:128 * (q + 1)]
        lhs = jnp.concatenate([jnp.where(is_fwd, uq, 0.0), jnp.where(is_fwd, 0.0, uq)], axis=1)
        bu_ref[:, 2 * half * q:2 * half * (q + 1)] = _dot(lhs.astype(BF16), bmat_ref[q])
    for q in range(nq):
        base = 2 * half * q
        re, im = slice(base, base + half), slice(base + half, base + 2 * half)
        ar, ai = lam_ref[:, re], lam_ref[:, im]
        xr, xi = st_ref[:, re], st_ref[:, im]
        for t in range(S):
            rows = slice(8 * t, 8 * t + 8)
            xr, xi = (ar * xr - ai * xi + bu_ref[rows, re], ar * xi + ai * xr + bu_ref[rows, im])
            bu_ref[rows, re] = xr
            bu_ref[rows, im] = xi
        st_ref[:, re] = xr
        st_ref[:, im] = xi
        y2 = _dot(bu_ref[:, base:base + 2 * half].astype(BF16), cmat_ref[q])
        ys.append(jnp.where(is_fwd, y2[:, :128], y2[:, 128:]))
    hi, lo = _split2(jnp.concatenate(ys, axis=1))
    y_nat = _dot(jnp.concatenate([perm_t, perm_t], axis=1), jnp.concatenate([hi, lo], axis=0))
    yf_ref[...] = y_nat[:bsz * S].reshape(bsz, S, S5_WIDTH)
    yb_ref[...] = y_nat[bsz * S:].reshape(bsz, S, S5_WIDTH)


def _s5_scan(u, lam, bmat, cmat, n_ctx_blocks):
    bsz, seq, _ = u.shape
    S = S5_STEPS
    nblk = seq // S
    nstate = 2 * S5_GROUPS * S5_STATE

    def back(i):
        return jnp.where(i < n_ctx_blocks, n_ctx_blocks - 1 - i, nblk - 1 - (i - n_ctx_blocks))

    spec_f = pl.BlockSpec((bsz, S, S5_WIDTH), lambda i: (0, i, 0))
    spec_b = pl.BlockSpec((bsz, S, S5_WIDTH), lambda i: (0, back(i), 0))
    out = jax.ShapeDtypeStruct(u.shape, F32)
    return pl.pallas_call(
        functools.partial(_s5_scan_kernel, bsz=bsz),
        grid=(nblk,),
        in_specs=[spec_f, spec_b,
                  pl.BlockSpec(lam.shape, lambda i: (0, 0)),
                  pl.BlockSpec(bmat.shape, lambda i: (0, 0, 0)),
                  pl.BlockSpec(cmat.shape, lambda i: (0, 0, 0))],
        out_specs=[spec_f, spec_b],
        out_shape=[out, out],
        scratch_shapes=[pltpu.VMEM((S * 8, nstate), F32), pltpu.VMEM((8, nstate), F32)],
        compiler_params=_params("arbitrary"),
        name="s5_scan",
    )(u, u, lam, bmat, cmat)


def _s5_discretise(a_re, a_im, log_step, b_re, b_im, c_re, c_im):
    lam_re = jnp.minimum(a_re, -1e-4)
    lam_im = a_im
    dt = jnp.exp(log_step)[..., None]
    mag = jnp.exp(lam_re * dt)
    ab_re, ab_im = mag * jnp.cos(lam_im * dt), mag * jnp.sin(lam_im * dt)
    den = lam_re * lam_re + lam_im * lam_im
    f_re = ((ab_re - 1.0) * lam_re + ab_im * lam_im) / den
    f_im = (ab_im * lam_re - (ab_re - 1.0) * lam_im) / den
    bb_re = f_re[..., None] * b_re - f_im[..., None] * b_im
    bb_im = f_re[..., None] * b_im + f_im[..., None] * b_re
    nq = S5_WIDTH // 128
    eye8 = jnp.eye(8, dtype=F32)

    def lanes(t):
        return t.reshape(2, nq, 8 * S5_STATE)

    lam = jnp.concatenate([lanes(ab_re), lanes(ab_im)], axis=-1).reshape(2, nq * 16 * S5_STATE)
    lam = jnp.repeat(lam, 4, axis=0)

    def in_block(t):
        t = t.reshape(2, nq, 8, S5_STATE, S5_GROUP_CH)
        return jnp.einsum('dqgpi,gh->dqgihp', t, eye8).reshape(2, nq, 128, 8 * S5_STATE)

    bmat = jnp.concatenate([in_block(bb_re), in_block(bb_im)], axis=-1)
    bmat = jnp.concatenate([bmat[0], bmat[1]], axis=1)

    def out_block(t):
        t = t.reshape(2, nq, 8, S5_GROUP_CH, S5_STATE)
        return jnp.einsum('dqgip,gh->dqgphi', t, eye8).reshape(2, nq, 8 * S5_STATE, 128)

    cmat = jnp.concatenate([out_block(c_re), -out_block(c_im)], axis=2)
    cmat = jnp.concatenate([cmat[0], cmat[1]], axis=-1)
    return lam, bmat.astype(BF16), cmat.astype(BF16)


def _ab_out_kernel(x_ref, m_ref, yf_ref, yb_ref, g_ref, bonus_ref, sf_ref, sb_ref, u_ref, vec_ref, gluw_ref, w_ref,
                   ones_ref, o_ref):
    W = RWKV_WIDTH
    ones = ones_ref[...]
    y = yf_ref[...] + yb_ref[...]
    mu = _head_sum(y, ones) * (1.0 / HEAD_DIM)
    yc = y - mu
    var = _head_sum(yc * yc, ones) * (1.0 / HEAD_DIM)
    ya = (yc * lax.rsqrt(var + GN_EPS) * vec_ref[0:1, :] + vec_ref[1:2, :] + bonus_ref[...]) * g_ref[...]

    s = (sf_ref[...] + sb_ref[...]) + vec_ref[2:3, :] * u_ref[...]
    z = 0.5 * s * (1.0 + jnp.tanh(math.sqrt(2.0 / math.pi) * (s + 0.044715 * (s * s * s))))
    yb = z * _sigmoid(_mm1(z, gluw_ref[...]) + vec_ref[3:4, :])
    out = _dot(ya.astype(BF16), w_ref[:W, :]) + _dot(yb.astype(BF16), w_ref[W:, :])
    o_ref[...] = x_ref[...] + m_ref[5:6, :] * out


def _ab_out(x, modrows, yf, yb, g, bonus, sf, sb, u, vecs, glu_w, out_w, ones, bsz, n_lat_tiles):
    nt = n_lat_tiles + 1
    W = RWKV_WIDTH
    xt = _x_tile(bsz, n_lat_tiles)
    seq_spec = pl.BlockSpec((None, ROW_TILE, W), lambda b, k: (b, k, 0))
    return pl.pallas_call(
        _ab_out_kernel,
        grid=(bsz, nt),
        in_specs=[pl.BlockSpec((ROW_TILE, D_MODEL), xt),
                  pl.BlockSpec((None, N_MOD, D_MODEL), lambda b, k: (xt(b, k)[0], 0, 0)),
                  seq_spec, seq_spec, seq_spec, seq_spec, seq_spec, seq_spec, seq_spec,
                  pl.BlockSpec(vecs.shape, lambda b, k: (0, 0)),
                  pl.BlockSpec(glu_w.shape, lambda b, k: (0, 0)),
                  pl.BlockSpec(out_w.shape, lambda b, k: (0, 0)),
                  pl.BlockSpec(ones.shape, lambda b, k: (0, 0))],
        out_specs=pl.BlockSpec((ROW_TILE, D_MODEL), xt),
        out_shape=jax.ShapeDtypeStruct(x.shape, F32),
        compiler_params=_params("parallel", "parallel"),
        name="ab_out",
    )(x, modrows, yf, yb, g, bonus, sf, sb, u, vecs, glu_w, out_w, ones)


def _qkv_rope(x, m_ref, g_ref, w_ref, cos_ref, sin_ref, o_ref):
    h = _modnorm(x, g_ref[...], m_ref[3:4, :], m_ref[4:5, :]).astype(BF16)
    p = _dot(h, w_ref[...])
    qk_w = ATTN_Q_W + ATTN_KV_W
    qk = p[:, :qk_w]
    half = HEAD_DIM // 2
    lane = lax.broadcasted_iota(jnp.int32, qk.shape, 1)
    first = (lane & (HEAD_DIM - 1)) < half
    partner = jnp.where(first, pltpu.roll(qk, qk_w - half, axis=1), pltpu.roll(qk, half, axis=1))
    reps = qk_w // 128
    cos = jnp.concatenate([cos_ref[...]] * reps, axis=1)
    sin = jnp.concatenate([sin_ref[...]] * reps, axis=1)
    o_ref[:, :qk_w] = qk * cos + partner * sin
    o_ref[:, qk_w:] = p[:, qk_w:]


def _rope_tables(n_lat, grid_w):
    half = HEAD_DIM // 2
    t = jnp.arange(n_lat)
    row_id = (t // grid_w).astype(F32)
    col_id = (t % grid_w).astype(F32)
    inv_freq = ROPE_BASE ** (-jnp.arange(0, half, 2, dtype=F32) / half)
    ang = jnp.concatenate([row_id[:, None] * inv_freq, col_id[:, None] * inv_freq], axis=-1)
    cos, sin = jnp.cos(ang), jnp.sin(ang)
    cos_t = jnp.concatenate([cos, cos, cos, cos], axis=-1)
    sin_t = jnp.concatenate([-sin, sin, -sin, sin], axis=-1)
    cos_t = jnp.concatenate([jnp.ones((FFN_ROW_TILE, 128), F32), cos_t], axis=0)
    sin_t = jnp.concatenate([jnp.zeros((FFN_ROW_TILE, 128), F32), sin_t], axis=0)
    return cos_t, sin_t


def _attn_kernel(sink_ref, q_ref, kp_ref, kc_ref, kn_ref, vp_ref, vc_ref, vn_ref, kx_ref, vx_ref, o_ref, *, nblk):
    i = pl.program_id(1)
    Q = ATTN_BLOCK
    G = ATTN_HEADS // ATTN_KV_HEADS
    k_all = jnp.concatenate([kp_ref[...], kc_ref[...], kn_ref[...], kx_ref[...]], axis=0).astype(BF16)
    v_all = jnp.concatenate([vp_ref[...], vc_ref[...], vn_ref[...], vx_ref[...]], axis=0).astype(BF16)
    nk = k_all.shape[0]
    qi = lax.broadcasted_iota(jnp.int32, (G * Q, nk), 0) & (Q - 1)
    cj = lax.broadcasted_iota(jnp.int32, (G * Q, nk), 1)
    mj = cj - Q
    blk = jnp.where(cj < Q, i - 1, jnp.where(cj < 2 * Q, i, i + 1))
    valid = jnp.logical_and(jnp.abs(mj - qi) <= ATTN_WINDOW, jnp.logical_and(blk >= 0, blk < nblk))
    valid = jnp.logical_or(valid, cj >= 3 * Q)
    q = q_ref[...] * (HEAD_DIM ** -0.5)
    row_head = lax.broadcasted_iota(jnp.int32, (G * Q, 1), 0) >> 7
    def scores(kh):
        q4 = jnp.concatenate([q[:, (kh * G + g) * HEAD_DIM:(kh * G + g + 1) * HEAD_DIM] for g in range(G)], axis=0)
        return _dot(q4.astype(BF16), k_all[:, kh * HEAD_DIM:(kh + 1) * HEAD_DIM], _NT)

    nxt = scores(0)
    for kh in range(ATTN_KV_HEADS):
        cur = nxt
        if kh + 1 < ATTN_KV_HEADS:
            nxt = scores(kh + 1)
        s = jnp.where(valid, cur, NEG_INF)
        sink = jnp.zeros((G * Q, 1), F32)
        for g in range(G):
            sink = jnp.where(row_head == g, sink_ref[kh * G + g], sink)
        m = jnp.maximum(jnp.max(s, axis=-1, keepdims=True), sink)
        pr = jnp.exp(s - m)
        den = jnp.sum(pr, axis=-1, keepdims=True) + jnp.exp(sink - m)
        o = _dot(pr.astype(BF16), v_all[:, kh * HEAD_DIM:(kh + 1) * HEAD_DIM]) / den
        for g in range(G):
            hq = kh * G + g
            o_ref[:, hq * HEAD_DIM:(hq + 1) * HEAD_DIM] = o[g * Q:(g + 1) * Q]


def _attention(qkv, sink, bsz, n_lat, n_ctx):
    Q = ATTN_BLOCK
    nblk = n_lat // Q
    lat0 = bsz * n_ctx // Q
    kcol, vcol = ATTN_Q_W // ATTN_KV_W, ATTN_Q_W // ATTN_KV_W + 1

    def kv_spec(col, off):
        return pl.BlockSpec((Q, ATTN_KV_W),
                            lambda b, i: (lat0 + b * nblk + jnp.clip(i + off, 0, nblk - 1), col))

    def ctx_spec(col):
        return pl.BlockSpec((n_ctx, ATTN_KV_W), lambda b, i: (b, col))

    return pl.pallas_call(
        functools.partial(_attn_kernel, nblk=nblk),
        grid=(bsz, nblk),
        in_specs=[pl.BlockSpec(memory_space=pltpu.SMEM),
                  pl.BlockSpec((Q, ATTN_Q_W), lambda b, i: (lat0 + b * nblk + i, 0)),
                  kv_spec(kcol, -1), kv_spec(kcol, 0), kv_spec(kcol, 1),
                  kv_spec(vcol, -1), kv_spec(vcol, 0), kv_spec(vcol, 1),
                  ctx_spec(kcol), ctx_spec(vcol)],
        out_specs=pl.BlockSpec((Q, ATTN_Q_W), lambda b, i: (b * nblk + i, 0)),
        out_shape=jax.ShapeDtypeStruct((bsz * n_lat, ATTN_Q_W), F32),
        compiler_params=_params("parallel", "parallel"),
        name="window_attention",
    )(sink, qkv, qkv, qkv, qkv, qkv, qkv, qkv, qkv, qkv)


def kernel(x, c, ctx, c_ctx, norm_g, mod_w, mod_b, ffn_w1, ffn_w2, ab_in_w, ab_out_w, rwkv_mu, rwkv_w0, rwkv_w2, rwkv_a0, rwkv_a2, rwkv_g2, rwkv_k_k, rwkv_k_a, rwkv_r_k, rwkv_lnx_g, rwkv_lnx_b, s5_a_re, s5_a_im, s5_log_step, s5_b_re, s5_b_im, s5_c_re, s5_c_im, s5_d, s5_glu_w, s5_glu_b, attn_in_w, attn_out_w, attn_sink, final_g):
    bsz, n_lat, _ = x.shape
    n_ctx = ctx.shape[1]
    depth = mod_w.shape[0]
    grid_w = 64
    assert n_ctx == ROW_TILE and n_lat % FFN_ROW_TILE == 0 and (bsz * n_ctx) % FFN_ROW_TILE == 0 and bsz <= 4
    seq = n_ctx + n_lat
    n_ctx_tiles = bsz
    n_lat_tiles = n_lat // ROW_TILE
    W = RWKV_WIDTH

    cs = jnp.zeros((8, D_MODEL), F32).at[:bsz].set(c).at[bsz].set(c_ctx)
    mods = _mod_vectors(jax.nn.silu(cs), mod_w, mod_b).reshape(depth, 8, N_MOD, D_MODEL)
    tile_row = jnp.concatenate([jnp.full((n_ctx_tiles,), bsz, jnp.int32),
                                jnp.repeat(jnp.arange(bsz, dtype=jnp.int32), n_lat_tiles)])
    modrows = mods[:, tile_row]

    xs = x.reshape(bsz * n_lat, D_MODEL)
    w1 = ffn_w1.astype(BF16)
    w2 = ffn_w2.astype(BF16)

    for l in range(depth):
        last = l == depth - 1
        mr = modrows[l]
        qkv_args = None
        if l % 2 == 1:
            cos_t, sin_t = _rope_tables(n_lat, grid_w)
            nct, nlt = bsz * n_ctx // FFN_ROW_TILE, n_lat // FFN_ROW_TILE

            def rope_tile(i):
                return (jnp.where(i < nct, 0, 1 + (i - nct) % nlt), 0)

            qkv_args = (norm_g[l, 1], attn_in_w[l // 2].astype(BF16), cos_t, sin_t, rope_tile)
        xs = _ffn(xs, mr, norm_g[l, 0], w1, w2, (l, 0), final_g, mod0=0, row_tile_of_mod=0,
                  x_ctx=ctx.reshape(bsz * n_ctx, D_MODEL) if l == 0 else None, qkv=qkv_args)
        if l % 2 == 0:
            e = l // 2
            w_lora = jnp.concatenate([
                _pack3(jnp.concatenate([rwkv_w2[e, 0], rwkv_w2[e, 1]], axis=1)),
                _pack3(jnp.concatenate([rwkv_a2[e, 0], rwkv_a2[e, 1]], axis=1)),
                _pack3(jnp.concatenate([rwkv_g2[e], jnp.zeros_like(rwkv_g2[e])], axis=1))], axis=0)
            vecs = jnp.stack([rwkv_k_k[e], rwkv_k_a[e], rwkv_r_k[e].reshape(W), rwkv_w0[e, 0], rwkv_w0[e, 1],
                              rwkv_a0[e, 0], rwkv_a0[e, 1], jnp.zeros((W,), F32)])
            ones3 = _head_ones3(W)
            pb, r, v, a, kd, bv, lw, g, bonus = _ab_in_proj(xs, mr, norm_g[l, 1], ab_in_w[e].astype(BF16), rwkv_mu[e],
                                                            w_lora, vecs, ones3, bsz, n_lat_tiles)
            yf, yb = _rwkv_scan(r, v, a, kd, bv, lw, n_ctx // RWKV_CHUNK)
            lam, bmat, cmat = _s5_discretise(s5_a_re[e], s5_a_im[e], s5_log_step[e], s5_b_re[e], s5_b_im[e],
                                             s5_c_re[e], s5_c_im[e])
            sf, sb = _s5_scan(pb, lam, bmat, cmat, n_ctx // S5_STEPS)
            vecs_out = jnp.stack([rwkv_lnx_g[e], rwkv_lnx_b[e], s5_d[e], s5_glu_b[e]] + [jnp.zeros((W,), F32)] * 4)
            xs = _ab_out(xs, mr, yf, yb, g, bonus, sf, sb, pb, vecs_out, s5_glu_w[e].astype(BF16),
                         ab_out_w[e].astype(BF16), ones3, bsz, n_lat_tiles)
            rows_mod0, attn = 0, None
        else:
            o = l // 2
            xs, qkv = xs
            att = _attention(qkv, attn_sink[o], bsz, n_lat, n_ctx)
            if not last:
                raise NotImplementedError("context update after an attention layer")
            rows_mod0, attn = n_ctx_tiles, (att, attn_out_w[o].astype(BF16))
        xs = _ffn(xs, mr, norm_g[l, 2], w1, w2, (l, 1), final_g, mod0=6, row_tile_of_mod=rows_mod0,
                  final=last, attn=attn)
    return xs.reshape(bsz, n_lat, D_MODEL)
```

```python
import functools
import math

import jax
import jax.numpy as jnp
from jax import lax
from jax.experimental import pallas as pl
from jax.experimental.pallas import tpu as pltpu

F32 = jnp.float32
BF16 = jnp.bfloat16

D_MODEL = 1024
D_FF = 2816
N_MOD = 9
HEAD_DIM = 64
RMS_EPS = 1e-6
GN_EPS = 64e-5
RWKV_WIDTH = 512
RWKV_PROJ = 1792
S5_WIDTH = 512
S5_GROUP_CH = 16
S5_GROUPS = 32
S5_STATE = 64
ATTN_HEADS = 16
ATTN_KV_HEADS = 4
ATTN_BLOCK = 128
ATTN_WINDOW = 128
ATTN_Q_W = 1024
ATTN_KV_W = 256
ROPE_BASE = 10000.0
NEG_INF = -1e30

ROW_TILE = 256
FFN_ROW_TILE = 512
FFN_F_CHUNKS = (1280, 1536)
RWKV_CHUNK = 64
S5_STEPS = 64
VMEM_LIMIT = 56 * 1024 * 1024

_NN = (((1,), (0,)), ((), ()))
_NT = (((1,), (1,)), ((), ()))


def _dot(a, b, dims=_NN):
    return lax.dot_general(a, b, dims, preferred_element_type=F32)


def _split2(a):
    hi = a.astype(BF16)
    lo = (a - hi.astype(F32)).astype(BF16)
    return hi, lo


def _split3(a):
    hi = a.astype(BF16)
    r1 = a - hi.astype(F32)
    mid = r1.astype(BF16)
    lo = (r1 - mid.astype(F32)).astype(BF16)
    return hi, mid, lo


def _mm1(a, b, dims=_NN):
    return _dot(a.astype(BF16), b.astype(BF16), dims)


def _mm3(a, b, dims=_NN):
    ah, al = _split2(a)
    bh, bl = _split2(b)
    return _dot(ah, bh, dims) + (_dot(ah, bl, dims) + _dot(al, bh, dims))


def _pack3(w):
    hi, lo = _split2(w)
    return jnp.concatenate([hi, lo, hi], axis=0)


def _dot3_packed(a, w3):
    hi, lo = _split2(a)
    return _dot(jnp.concatenate([hi, hi, lo], axis=1), w3)


def _head_sum(a, ones3):
    hi, mid, lo = _split3(a)
    return _dot(jnp.concatenate([hi, mid, lo], axis=1), ones3)


def _sigmoid(x):
    return 1.0 / (1.0 + jnp.exp(-x))


def _silu(x):
    return x * _sigmoid(x)


def _params(*sem):
    return pltpu.CompilerParams(dimension_semantics=sem, vmem_limit_bytes=VMEM_LIMIT)


def _modnorm(x, g, shift, scale):
    xn = x * lax.rsqrt(jnp.mean(x * x, axis=-1, keepdims=True) + RMS_EPS)
    return (xn * g) * (1.0 + scale) + shift


def _head_ones3(width):
    r = lax.broadcasted_iota(jnp.int32, (width, width), 0) >> 6
    c = lax.broadcasted_iota(jnp.int32, (width, width), 1) >> 6
    ones = (r == c).astype(BF16)
    return jnp.concatenate([ones, ones, ones], axis=0)


def _mod_kernel(c_ref, w_ref, b_ref, o_ref):
    o_ref[...] = _mm3(c_ref[...], w_ref[...]) + b_ref[...]


def _mod_vectors(cs, mod_w, mod_b):
    depth = mod_w.shape[0]
    n = mod_w.shape[2]
    tn = 2304
    return pl.pallas_call(
        _mod_kernel,
        grid=(depth, n // tn),
        in_specs=[pl.BlockSpec((8, D_MODEL), lambda l, j: (0, 0)),
                  pl.BlockSpec((None, D_MODEL, tn), lambda l, j: (l, 0, j)),
                  pl.BlockSpec((None, 1, tn), lambda l, j: (l, 0, j))],
        out_specs=pl.BlockSpec((None, 8, tn), lambda l, j: (l, 0, j)),
        out_shape=jax.ShapeDtypeStruct((depth, 8, n), F32),
        compiler_params=_params("parallel", "parallel"),
        name="mod_vectors",
    )(cs, mod_w, mod_b.reshape(depth, 1, n))


def _ffn_kernel(x_ref, xc_ref, att_ref, wo_ref, m_ref, g_ref, w1_ref, w2_ref, fg_ref, *rest,
                mod0, final, n_ctx_tiles, attn_residual, qkv_proj):
    if qkv_proj:
        gq_ref, wq_ref, cos_ref, sin_ref, o_ref, qkv_ref = rest
    else:
        (o_ref,) = rest
    x = x_ref[...]
    if n_ctx_tiles:
        x = jnp.where(pl.program_id(0) < n_ctx_tiles, xc_ref[...], x)
    if attn_residual:
        x = x + m_ref[5:6, :] * _dot(att_ref[...].astype(BF16), wo_ref[...])
    h = _modnorm(x, g_ref[...], m_ref[mod0:mod0 + 1, :], m_ref[mod0 + 1:mod0 + 2, :]).astype(BF16)
    acc = None
    f0 = 0
    for fc in FFN_F_CHUNKS:
        gate = _dot(h, w1_ref[:, f0:f0 + fc])
        up = _dot(h, w1_ref[:, D_FF + f0:D_FF + f0 + fc])
        part = _dot((_silu(gate) * up).astype(BF16), w2_ref[f0:f0 + fc, :])
        acc = part if acc is None else acc + part
        f0 += fc
    y = x + (0.5 * m_ref[mod0 + 2:mod0 + 3, :]) * acc
    if final:
        y = y * lax.rsqrt(jnp.mean(y * y, axis=-1, keepdims=True) + RMS_EPS) * fg_ref[...]
    o_ref[...] = y
    if qkv_proj:
        _qkv_rope(y, m_ref, gq_ref, wq_ref, cos_ref, sin_ref, qkv_ref)


def _ffn(x, modrows, g, w1, w2, lj, final_g, *, mod0, row_tile_of_mod, final=False, x_ctx=None, attn=None,
         qkv=None):
    tm = FFN_ROW_TILE
    step = tm // ROW_TILE
    nct = 0 if x_ctx is None else x_ctx.shape[0] // tm
    x_off = 0
    rows = x.shape[0] + nct * tm
    once = pl.Buffered(1)
    small = pl.BlockSpec((8, D_MODEL), lambda i: (0, 0))
    if x_ctx is None:
        x_ctx, ctx_spec = x, small
    else:
        ctx_spec = pl.BlockSpec((tm, D_MODEL), lambda i: (jnp.minimum(i, nct - 1), 0))
    if attn is None:
        att, w_out, att_spec, wo_spec = x, x, small, small
    else:
        att, w_out = attn
        rows = att.shape[0]
        x_off = row_tile_of_mod // step
        att_spec = pl.BlockSpec((tm, D_MODEL), lambda i: (i, 0))
        wo_spec = pl.BlockSpec(w_out.shape, lambda i: (0, 0), pipeline_mode=once)
    layer, which = lj
    operands = [x, x_ctx, att, w_out, modrows, g.reshape(1, D_MODEL), w1, w2, final_g.reshape(1, D_MODEL)]
    in_specs = [pl.BlockSpec((tm, D_MODEL), lambda i: (x_off + jnp.maximum(i - nct, 0), 0)),
                ctx_spec, att_spec, wo_spec,
                pl.BlockSpec((None, N_MOD, D_MODEL), lambda i: (row_tile_of_mod + i * step, 0, 0)),
                pl.BlockSpec((1, D_MODEL), lambda i: (0, 0)),
                pl.BlockSpec((None, None) + w1.shape[2:], lambda i: (layer, which, 0, 0), pipeline_mode=once),
                pl.BlockSpec((None, None) + w2.shape[2:], lambda i: (layer, which, 0, 0), pipeline_mode=once),
                pl.BlockSpec((1, D_MODEL), lambda i: (0, 0))]
    out_specs = [pl.BlockSpec((tm, D_MODEL), lambda i: (i, 0))]
    out_shape = [jax.ShapeDtypeStruct((rows, D_MODEL), F32)]
    if qkv is not None:
        g_mix, w_in, cos_t, sin_t, rope_tile = qkv
        operands += [g_mix.reshape(1, D_MODEL), w_in, cos_t, sin_t]
        in_specs += [pl.BlockSpec((1, D_MODEL), lambda i: (0, 0)),
                     pl.BlockSpec(w_in.shape, lambda i: (0, 0), pipeline_mode=once),
                     pl.BlockSpec((tm, 128), rope_tile), pl.BlockSpec((tm, 128), rope_tile)]
        out_specs.append(pl.BlockSpec((tm, w_in.shape[1]), lambda i: (i, 0)))
        out_shape.append(jax.ShapeDtypeStruct((rows, w_in.shape[1]), BF16))
    out = pl.pallas_call(
        functools.partial(_ffn_kernel, mod0=mod0, final=final, n_ctx_tiles=nct, attn_residual=attn is not None,
                          qkv_proj=qkv is not None),
        grid=(rows // tm,),
        in_specs=in_specs,
        out_specs=out_specs,
        out_shape=out_shape,
        compiler_params=_params("parallel"),
        name="ffn",
    )(*operands)
    return out if qkv is not None else out[0]


def _x_tile(bsz, n_lat_tiles):
    return lambda b, k: (jnp.where(k == 0, b, bsz + b * n_lat_tiles + k - 1), 0)


def _ab_in_kernel(x_ref, xp_ref, xn_ref, m_ref, g_ref, w_ref, mu_ref, wl_ref, vec_ref, ones_ref,
                  pb_ref, r_ref, v_ref, a_ref, k_ref, b_ref, lw_ref, g_out_ref, bonus_ref):
    k_idx = pl.program_id(1)
    nt = pl.num_programs(1)
    xa = jnp.concatenate([xp_ref[...], x_ref[...], xn_ref[...]], axis=0)
    h = _modnorm(xa, g_ref[...], m_ref[3:4, :], m_ref[4:5, :]).astype(BF16)
    p_all = _dot(h, w_ref[...])
    rows = x_ref.shape[0]
    pb_ref[...] = p_all[8:8 + rows, RWKV_PROJ:]
    cur = p_all[8:8 + rows, :RWKV_PROJ]
    row = lax.broadcasted_iota(jnp.int32, cur.shape, 0)
    use_prev = k_idx >= 2
    use_next = jnp.logical_and(k_idx >= 1, k_idx < nt - 1)
    prev_row = jnp.where(use_prev, p_all[7:8, :RWKV_PROJ], 0.0)
    next_row = jnp.where(use_next, p_all[8 + rows:9 + rows, :RWKV_PROJ], 0.0)
    prev = jnp.where(row == 0, prev_row, pltpu.roll(cur, 1, axis=0))
    nxt = jnp.where(row == rows - 1, next_row, pltpu.roll(cur, rows - 1, axis=0))
    p = cur + mu_ref[0:1, :] * (prev - cur) + mu_ref[1:2, :] * (nxt - cur)

    W = RWKV_WIDTH
    k_k, k_a, r_k = vec_ref[0:1, :], vec_ref[1:2, :], vec_ref[2:3, :]
    ones = ones_ref[...]
    halves = [slice(0, rows // 2), slice(rows // 2, rows)]
    stage1 = []
    for hs in halves:
        ph = p[hs]
        kk = ph[:, W:2 * W] * k_k
        stage1.append(dict(
            kk=kk,
            lo_w=_dot3_packed(jnp.tanh(ph[:, 3 * W:3 * W + 64]), wl_ref[0:192, :]),
            lo_a=_dot3_packed(ph[:, 3 * W + 64:3 * W + 128], wl_ref[192:384, :]),
            gate=_dot3_packed(_sigmoid(ph[:, 3 * W + 128:]), wl_ref[384:768, 0:W]),
            norm2=_head_sum(kk * kk, ones)))
    for hs, s in zip(halves, stage1):
        ph = p[hs]
        r, k, v = ph[:, :W], ph[:, W:2 * W], ph[:, 2 * W:3 * W]
        kk = s["kk"] * lax.rsqrt(jnp.maximum(s["norm2"], 1e-12))
        r_ref[hs, :] = r
        v_ref[hs, :] = v
        a_ref[hs, :] = -kk
        k_sum = jnp.zeros_like(k)
        for d in range(2):
            z = -(vec_ref[3 + d:4 + d, :] + s["lo_w"][:, d * W:(d + 1) * W])
            softplus = jnp.maximum(z, 0.0) + jnp.log(1.0 + jnp.exp(-jnp.abs(z)))
            lw_ref[d, hs, :] = -jnp.exp(-softplus - 0.5)
            a = _sigmoid(vec_ref[5 + d:6 + d, :] + s["lo_a"][:, d * W:(d + 1) * W])
            kd = k * (1.0 + (a - 1.0) * k_a)
            k_ref[d, hs, :] = kd
            b_ref[d, hs, :] = kk * a
            k_sum = k_sum + kd
        g_out_ref[hs, :] = s["gate"]
        bonus_ref[hs, :] = _head_sum(r * (0.5 * k_sum) * r_k, ones) * v


def _ab_in_proj(x, modrows, g, w, mu, w_lora, vecs, ones, bsz, n_lat_tiles):
    nt = n_lat_tiles + 1
    seq = nt * ROW_TILE
    sub = ROW_TILE // 8
    last_blk = x.shape[0] // 8 - 1
    xt = _x_tile(bsz, n_lat_tiles)
    W = RWKV_WIDTH
    one = jax.ShapeDtypeStruct((bsz, seq, W), F32)
    two = jax.ShapeDtypeStruct((2, bsz, seq, W), F32)
    spec1 = pl.BlockSpec((None, ROW_TILE, W), lambda b, k: (b, k, 0))
    spec2 = pl.BlockSpec((2, None, ROW_TILE, W), lambda b, k: (0, b, k, 0))
    const = lambda b, k: (0, 0)
    return pl.pallas_call(
        _ab_in_kernel,
        grid=(bsz, nt),
        in_specs=[pl.BlockSpec((ROW_TILE, D_MODEL), xt),
                  pl.BlockSpec((8, D_MODEL), lambda b, k: (jnp.maximum(xt(b, k)[0] * sub - 1, 0), 0)),
                  pl.BlockSpec((8, D_MODEL), lambda b, k: (jnp.minimum((xt(b, k)[0] + 1) * sub, last_blk), 0)),
                  pl.BlockSpec((None, N_MOD, D_MODEL), lambda b, k: (xt(b, k)[0], 0, 0)),
                  pl.BlockSpec((1, D_MODEL), const),
                  pl.BlockSpec(w.shape, const),
                  pl.BlockSpec(mu.shape, const),
                  pl.BlockSpec(w_lora.shape, const),
                  pl.BlockSpec(vecs.shape, const),
                  pl.BlockSpec(ones.shape, const)],
        out_specs=[spec1, spec1, spec1, spec1, spec2, spec2, spec2, spec1, spec1],
        out_shape=[one, one, one, one, two, two, two, one, one],
        compiler_params=_params("parallel", "parallel"),
        name="ab_in_proj",
    )(x, x, x, modrows, g.reshape(1, D_MODEL), w, mu, w_lora, vecs, ones)


RWKV_TILE_HEADS = 2
RWKV_TILE = RWKV_TILE_HEADS * HEAD_DIM
RWKV_BATCHES_PER_STEP = 4


def _rwkv_scaled(fwd, r, v, a, k, b, lw):
    C = RWKV_CHUNK
    rr = lax.broadcasted_iota(jnp.int32, (C, C), 0)
    cc = lax.broadcasted_iota(jnp.int32, (C, C), 1)
    tri = ((rr >= cc) if fwd else (rr <= cc)).astype(BF16)
    hi, mid, lo = _split3(lw)
    cum = _dot(jnp.concatenate([tri, tri, tri], axis=1), jnp.concatenate([hi, mid, lo], axis=0))
    last = cum[C - 1:C, :] if fwd else cum[0:1, :]
    einv = jnp.exp(-cum)
    to_end = jnp.exp(last - cum)
    return dict(at=a * jnp.exp(cum - lw), rt=r * jnp.exp(cum), bt=b * einv, kt=k * einv,
                b_end=b * to_end, k_end=k * to_end, p_end=jnp.exp(last), v=v)


def _rwkv_chunks(dirs, h_ref, y_refs):
    C = RWKV_CHUNK
    TW = RWKV_TILE
    tq = lax.broadcasted_iota(jnp.int32, (C, TW), 0)
    sq = lax.broadcasted_iota(jnp.int32, (C, TW), 1) & (C - 1)
    eye_cat = (tq == sq).astype(F32)
    rb = lax.broadcasted_iota(jnp.int32, (TW, TW), 0)
    cb = lax.broadcasted_iota(jnp.int32, (TW, TW), 1)
    same_head = (rb >> 6) == (cb >> 6)
    diag = rb == cb

    def stack(x):
        return jnp.where(same_head, jnp.concatenate([x] * RWKV_TILE_HEADS, axis=0), jnp.zeros((), x.dtype))

    chains = []
    for di, (fwd, op) in enumerate(dirs):
        strict = (tq > sq) if fwd else (tq < sq)
        incl = (tq >= sq) if fwd else (tq <= sq)
        for q in range(RWKV_WIDTH // TW):
            sl = slice(TW * q, TW * (q + 1))
            ch = {name: val[:, sl] for name, val in op.items()}
            ch.update(strict=strict, incl=incl, di=di, q=q, sl=sl)
            chains.append(ch)

    for ch in chains:
        lhs = jnp.concatenate([ch["at"], ch["rt"]], axis=0).astype(BF16)
        rhs = jnp.concatenate([stack(ch["bt"].astype(BF16)), stack(ch["kt"].astype(BF16))], axis=0)
        s_all = _dot(lhs, rhs, _NT)
        ch["n_ab"] = jnp.where(ch["strict"], s_all[:C, :TW], 0.0)
        ch["n_ak"] = jnp.where(ch["strict"], s_all[:C, TW:], 0.0)
        ch["n_rb"] = jnp.where(ch["incl"], s_all[C:, :TW], 0.0)
        ch["n_rk"] = jnp.where(ch["incl"], s_all[C:, TW:], 0.0)
        ch["tinv"] = eye_cat + ch["n_ab"]
        ch["pw"] = ch["n_ab"]

    levels = int(math.log2(C))
    for i in range(levels):
        for ch in chains:
            pw_b = ch["pw"].astype(BF16)
            if i == 0:
                ch["pw"] = _dot(pw_b, stack(pw_b))
                v_s = stack(ch["v"].astype(BF16))
                ch["kv"] = _dot(jnp.concatenate([ch["n_ak"], ch["n_rk"]], axis=0).astype(BF16), v_s)
            elif i == levels - 1:
                ch["tinv"] = ch["tinv"] + _dot(pw_b, stack(ch["tinv"].astype(BF16)))
            else:
                res = _dot(pw_b, jnp.concatenate([stack(pw_b), stack(ch["tinv"].astype(BF16))], axis=1))
                ch["pw"] = res[:, :TW]
                ch["tinv"] = ch["tinv"] + res[:, TW:]

    for ch in chains:
        nh, nl = _split2(ch["n_ab"])
        xh, xl = _split2(ch["tinv"])
        ch["xh"] = xh
        ch["resid"] = (eye_cat - ch["tinv"]) + _dot(jnp.concatenate([nh, nh, nl], axis=1),
                                                     jnp.concatenate([stack(xh), stack(xl), stack(xh)], axis=0))
    for ch in chains:
        ch["tinv"] = ch["tinv"] + _dot(ch["xh"], stack(ch["resid"].astype(BF16)))

    for ch in chains:
        rhs = jnp.concatenate([stack(ch["at"].astype(BF16)), stack(ch["kv"][:C].astype(BF16))], axis=1)
        ch["aw"] = _dot(ch["tinv"].astype(BF16), rhs).astype(BF16)

    for ch in chains:
        aw = ch["aw"]
        c1 = _dot(ch["n_rb"].astype(BF16), jnp.concatenate([stack(aw[:, :TW]), stack(aw[:, TW:])], axis=1))
        ch["qm"] = ch["rt"] + c1[:, :TW]
        ch["z"] = c1[:, TW:] + ch["kv"][C:]
        bk_t = jnp.concatenate([ch["b_end"], ch["k_end"]], axis=0).T.astype(BF16)
        low = jnp.concatenate([jnp.zeros((C, TW), BF16), ch["v"].astype(BF16)], axis=1)
        c2 = _dot(bk_t, jnp.concatenate([aw, low], axis=0))
        ch["m_upd"] = jnp.where(same_head, c2[:, :TW], 0.0)
        ch["g_upd"] = jnp.where(same_head, c2[:, TW:], 0.0)

    for ch in chains:
        h = h_ref[ch["di"], ch["q"]]
        d = _dot(jnp.concatenate([ch["m_upd"], ch["qm"]], axis=0).astype(BF16), h.astype(BF16))
        y_refs[ch["di"]][:, ch["sl"]] = d[TW:] + ch["z"]
        p_col = jnp.sum(jnp.where(diag, ch["p_end"], 0.0), axis=1, keepdims=True)
        h_ref[ch["di"], ch["q"]] = p_col * h + d[:TW] + ch["g_upd"]


def _rwkv_scan_kernel(rf_ref, vf_ref, af_ref, rb_ref, vb_ref, ab_ref, kf_ref, bf_ref, lwf_ref,
                      kb_ref, bb_ref, lwb_ref, yf_ref, yb_ref, h_ref, *, nb):
    @pl.when(pl.program_id(1) == 0)
    def _():
        h_ref[...] = jnp.zeros_like(h_ref)

    dirs, y_refs = [], []
    for i in range(nb):
        dirs.append((True, _rwkv_scaled(True, rf_ref[i], vf_ref[i], af_ref[i], kf_ref[i], bf_ref[i], lwf_ref[i])))
        dirs.append((False, _rwkv_scaled(False, rb_ref[i], vb_ref[i], ab_ref[i], kb_ref[i], bb_ref[i], lwb_ref[i])))
        y_refs += [yf_ref.at[i], yb_ref.at[i]]
    _rwkv_chunks(dirs, h_ref, y_refs)


def _rwkv_scan(r, v, a, k, b, lw, n_ctx_chunks):
    bsz, seq, W = r.shape
    C = RWKV_CHUNK
    nch = seq // C

    def back(j):
        return jnp.where(j < n_ctx_chunks, n_ctx_chunks - 1 - j, nch - 1 - (j - n_ctx_chunks))

    nb = math.gcd(bsz, RWKV_BATCHES_PER_STEP)
    one_f = pl.BlockSpec((nb, C, W), lambda g, j: (g, j, 0))
    one_b = pl.BlockSpec((nb, C, W), lambda g, j: (g, back(j), 0))
    two_f = pl.BlockSpec((None, nb, C, W), lambda g, j: (0, g, j, 0))
    two_b = pl.BlockSpec((None, nb, C, W), lambda g, j: (1, g, back(j), 0))
    out = jax.ShapeDtypeStruct((bsz, seq, W), F32)
    return pl.pallas_call(
        functools.partial(_rwkv_scan_kernel, nb=nb),
        grid=(bsz // nb, nch),
        in_specs=[one_f, one_f, one_f, one_b, one_b, one_b, two_f, two_f, two_f, two_b, two_b, two_b],
        out_specs=[one_f, one_b],
        out_shape=[out, out],
        scratch_shapes=[pltpu.VMEM((2 * nb, W // RWKV_TILE, RWKV_TILE, RWKV_TILE), F32)],
        compiler_params=_params("parallel", "arbitrary"),
        name="rwkv_scan",
    )(r, v, a, r, v, a, k, b, lw, k, b, lw)


def _s5_scan_kernel(uf_ref, ub_ref, lam_ref, bmat_ref, cmat_ref, yf_ref, yb_ref, bu_ref, st_ref, *, bsz):
    S = S5_STEPS
    half = 8 * S5_STATE
    n_nat = 2 * bsz * S

    @pl.when(pl.program_id(0) == 0)
    def _():
        st_ref[...] = jnp.zeros_like(st_ref)

    def perm_mask(rho, col):
        t, s = rho >> 3, rho & 7
        src = jnp.where(s < 4, s * S + t, bsz * S + (s - 4) * S + (S - 1 - t))
        return jnp.logical_and((s & 3) < bsz, col == src).astype(BF16)

    perm = perm_mask(lax.broadcasted_iota(jnp.int32, (S * 8, n_nat), 0),
                     lax.broadcasted_iota(jnp.int32, (S * 8, n_nat), 1))
    perm_t = perm_mask(lax.broadcasted_iota(jnp.int32, (n_nat, S * 8), 1),
                       lax.broadcasted_iota(jnp.int32, (n_nat, S * 8), 0))
    u_nat = jnp.concatenate([uf_ref[...].reshape(bsz * S, S5_WIDTH), ub_ref[...].reshape(bsz * S, S5_WIDTH)], axis=0)
    u = _dot(perm, u_nat.astype(BF16))
    is_fwd = (lax.broadcasted_iota(jnp.int32, (S * 8, 128), 0) & 7) < 4
    nq = S5_WIDTH // 128
    ys = []
    for q in range(nq):
        uq = u[:, 128 * q:128 * (q + 1)]
        lhs = jnp.concatenate([jnp.where(is_fwd, uq, 0.0), jnp.where(is_fwd, 0.0, uq)], axis=1)
        bu_ref[:, 2 * half * q:2 * half * (q + 1)] = _dot(lhs.astype(BF16), bmat_ref[q])
    for q in range(nq):
        base = 2 * half * q
        re, im = slice(base, base + half), slice(base + half, base + 2 * half)
        ar, ai = lam_ref[:, re], lam_ref[:, im]
        xr, xi = st_ref[:, re], st_ref[:, im]
        for t in range(S):
            rows = slice(8 * t, 8 * t + 8)
            xr, xi = (ar * xr - ai * xi + bu_ref[rows, re], ar * xi + ai * xr + bu_ref[rows, im])
            bu_ref[rows, re] = xr
            bu_ref[rows, im] = xi
        st_ref[:, re] = xr
        st_ref[:, im] = xi
        y2 = _dot(bu_ref[:, base:base + 2 * half].astype(BF16), cmat_ref[q])
        ys.append(jnp.where(is_fwd, y2[:, :128], y2[:, 128:]))
    hi, lo = _split2(jnp.concatenate(ys, axis=1))
    y_nat = _dot(jnp.concatenate([perm_t, perm_t], axis=1), jnp.concatenate([hi, lo], axis=0))
    yf_ref[...] = y_nat[:bsz * S].reshape(bsz, S, S5_WIDTH)
    yb_ref[...] = y_nat[bsz * S:].reshape(bsz, S, S5_WIDTH)


def _s5_scan(u, lam, bmat, cmat, n_ctx_blocks):
    bsz, seq, _ = u.shape
    S = S5_STEPS
    nblk = seq // S
    nstate = 2 * S5_GROUPS * S5_STATE

    def back(i):
        return jnp.where(i < n_ctx_blocks, n_ctx_blocks - 1 - i, nblk - 1 - (i - n_ctx_blocks))

    spec_f = pl.BlockSpec((bsz, S, S5_WIDTH), lambda i: (0, i, 0))
    spec_b = pl.BlockSpec((bsz, S, S5_WIDTH), lambda i: (0, back(i), 0))
    out = jax.ShapeDtypeStruct(u.shape, F32)
    return pl.pallas_call(
        functools.partial(_s5_scan_kernel, bsz=bsz),
        grid=(nblk,),
        in_specs=[spec_f, spec_b,
                  pl.BlockSpec(lam.shape, lambda i: (0, 0)),
                  pl.BlockSpec(bmat.shape, lambda i: (0, 0, 0)),
                  pl.BlockSpec(cmat.shape, lambda i: (0, 0, 0))],
        out_specs=[spec_f, spec_b],
        out_shape=[out, out],
        scratch_shapes=[pltpu.VMEM((S * 8, nstate), F32), pltpu.VMEM((8, nstate), F32)],
        compiler_params=_params("arbitrary"),
        name="s5_scan",
    )(u, u, lam, bmat, cmat)


def _s5_discretise(a_re, a_im, log_step, b_re, b_im, c_re, c_im):
    lam_re = jnp.minimum(a_re, -1e-4)
    lam_im = a_im
    dt = jnp.exp(log_step)[..., None]
    mag = jnp.exp(lam_re * dt)
    ab_re, ab_im = mag * jnp.cos(lam_im * dt), mag * jnp.sin(lam_im * dt)
    den = lam_re * lam_re + lam_im * lam_im
    f_re = ((ab_re - 1.0) * lam_re + ab_im * lam_im) / den
    f_im = (ab_im * lam_re - (ab_re - 1.0) * lam_im) / den
    bb_re = f_re[..., None] * b_re - f_im[..., None] * b_im
    bb_im = f_re[..., None] * b_im + f_im[..., None] * b_re
    nq = S5_WIDTH // 128
    eye8 = jnp.eye(8, dtype=F32)

    def lanes(t):
        return t.reshape(2, nq, 8 * S5_STATE)

    lam = jnp.concatenate([lanes(ab_re), lanes(ab_im)], axis=-1).reshape(2, nq * 16 * S5_STATE)
    lam = jnp.repeat(lam, 4, axis=0)

    def in_block(t):
        t = t.reshape(2, nq, 8, S5_STATE, S5_GROUP_CH)
        return jnp.einsum('dqgpi,gh->dqgihp', t, eye8).reshape(2, nq, 128, 8 * S5_STATE)

    bmat = jnp.concatenate([in_block(bb_re), in_block(bb_im)], axis=-1)
    bmat = jnp.concatenate([bmat[0], bmat[1]], axis=1)

    def out_block(t):
        t = t.reshape(2, nq, 8, S5_GROUP_CH, S5_STATE)
        return jnp.einsum('dqgip,gh->dqgphi', t, eye8).reshape(2, nq, 8 * S5_STATE, 128)

    cmat = jnp.concatenate([out_block(c_re), -out_block(c_im)], axis=2)
    cmat = jnp.concatenate([cmat[0], cmat[1]], axis=-1)
    return lam, bmat.astype(BF16), cmat.astype(BF16)


def _ab_out_kernel(x_ref, m_ref, yf_ref, yb_ref, g_ref, bonus_ref, sf_ref, sb_ref, u_ref, vec_ref, gluw_ref, w_ref,
                   ones_ref, o_ref):
    W = RWKV_WIDTH
    ones = ones_ref[...]
    y = yf_ref[...] + yb_ref[...]
    mu = _head_sum(y, ones) * (1.0 / HEAD_DIM)
    yc = y - mu
    var = _head_sum(yc * yc, ones) * (1.0 / HEAD_DIM)
    ya = (yc * lax.rsqrt(var + GN_EPS) * vec_ref[0:1, :] + vec_ref[1:2, :] + bonus_ref[...]) * g_ref[...]

    s = (sf_ref[...] + sb_ref[...]) + vec_ref[2:3, :] * u_ref[...]
    z = 0.5 * s * (1.0 + jnp.tanh(math.sqrt(2.0 / math.pi) * (s + 0.044715 * (s * s * s))))
    yb = z * _sigmoid(_mm1(z, gluw_ref[...]) + vec_ref[3:4, :])
    out = _dot(ya.astype(BF16), w_ref[:W, :]) + _dot(yb.astype(BF16), w_ref[W:, :])
    o_ref[...] = x_ref[...] + m_ref[5:6, :] * out


def _ab_out(x, modrows, yf, yb, g, bonus, sf, sb, u, vecs, glu_w, out_w, ones, bsz, n_lat_tiles):
    nt = n_lat_tiles + 1
    W = RWKV_WIDTH
    xt = _x_tile(bsz, n_lat_tiles)
    seq_spec = pl.BlockSpec((None, ROW_TILE, W), lambda b, k: (b, k, 0))
    return pl.pallas_call(
        _ab_out_kernel,
        grid=(bsz, nt),
        in_specs=[pl.BlockSpec((ROW_TILE, D_MODEL), xt),
                  pl.BlockSpec((None, N_MOD, D_MODEL), lambda b, k: (xt(b, k)[0], 0, 0)),
                  seq_spec, seq_spec, seq_spec, seq_spec, seq_spec, seq_spec, seq_spec,
                  pl.BlockSpec(vecs.shape, lambda b, k: (0, 0)),
                  pl.BlockSpec(glu_w.shape, lambda b, k: (0, 0)),
                  pl.BlockSpec(out_w.shape, lambda b, k: (0, 0)),
                  pl.BlockSpec(ones.shape, lambda b, k: (0, 0))],
        out_specs=pl.BlockSpec((ROW_TILE, D_MODEL), xt),
        out_shape=jax.ShapeDtypeStruct(x.shape, F32),
        compiler_params=_params("parallel", "parallel"),
        name="ab_out",
    )(x, modrows, yf, yb, g, bonus, sf, sb, u, vecs, glu_w, out_w, ones)


def _qkv_rope(x, m_ref, g_ref, w_ref, cos_ref, sin_ref, o_ref):
    h = _modnorm(x, g_ref[...], m_ref[3:4, :], m_ref[4:5, :]).astype(BF16)
    p = _dot(h, w_ref[...])
    qk_w = ATTN_Q_W + ATTN_KV_W
    qk = p[:, :qk_w]
    half = HEAD_DIM // 2
    lane = lax.broadcasted_iota(jnp.int32, qk.shape, 1)
    first = (lane & (HEAD_DIM - 1)) < half
    partner = jnp.where(first, pltpu.roll(qk, qk_w - half, axis=1), pltpu.roll(qk, half, axis=1))
    reps = qk_w // 128
    cos = jnp.concatenate([cos_ref[...]] * reps, axis=1)
    sin = jnp.concatenate([sin_ref[...]] * reps, axis=1)
    o_ref[:, :qk_w] = (qk * cos + partner * sin).astype(o_ref.dtype)
    o_ref[:, qk_w:] = p[:, qk_w:].astype(o_ref.dtype)


def _rope_tables(n_lat, grid_w):
    half = HEAD_DIM // 2
    t = jnp.arange(n_lat)
    row_id = (t // grid_w).astype(F32)
    col_id = (t % grid_w).astype(F32)
    inv_freq = ROPE_BASE ** (-jnp.arange(0, half, 2, dtype=F32) / half)
    ang = jnp.concatenate([row_id[:, None] * inv_freq, col_id[:, None] * inv_freq], axis=-1)
    cos, sin = jnp.cos(ang), jnp.sin(ang)
    cos_t = jnp.concatenate([cos, cos, cos, cos], axis=-1)
    sin_t = jnp.concatenate([-sin, sin, -sin, sin], axis=-1)
    cos_t = jnp.concatenate([jnp.ones((FFN_ROW_TILE, 128), F32), cos_t], axis=0)
    sin_t = jnp.concatenate([jnp.zeros((FFN_ROW_TILE, 128), F32), sin_t], axis=0)
    return cos_t, sin_t


def _attn_kernel(sink_ref, q_ref, kp_ref, kc_ref, kn_ref, vp_ref, vc_ref, vn_ref, kx_ref, vx_ref, o_ref, *, nblk):
    i = pl.program_id(1)
    Q = ATTN_BLOCK
    G = ATTN_HEADS // ATTN_KV_HEADS
    k_all = jnp.concatenate([kp_ref[...], kc_ref[...], kn_ref[...], kx_ref[...]], axis=0)
    v_all = jnp.concatenate([vp_ref[...], vc_ref[...], vn_ref[...], vx_ref[...]], axis=0)
    nk = k_all.shape[0]
    qi = lax.broadcasted_iota(jnp.int32, (G * Q, nk), 0) & (Q - 1)
    cj = lax.broadcasted_iota(jnp.int32, (G * Q, nk), 1)
    mj = cj - Q
    blk = jnp.where(cj < Q, i - 1, jnp.where(cj < 2 * Q, i, i + 1))
    valid = jnp.logical_and(jnp.abs(mj - qi) <= ATTN_WINDOW, jnp.logical_and(blk >= 0, blk < nblk))
    valid = jnp.logical_or(valid, cj >= 3 * Q)
    q = q_ref[...] * (HEAD_DIM ** -0.5)
    row_head = lax.broadcasted_iota(jnp.int32, (G * Q, 1), 0) >> 7
    def scores(kh):
        q4 = jnp.concatenate([q[:, (kh * G + g) * HEAD_DIM:(kh * G + g + 1) * HEAD_DIM] for g in range(G)], axis=0)
        return _dot(q4, k_all[:, kh * HEAD_DIM:(kh + 1) * HEAD_DIM], _NT)

    nxt = scores(0)
    for kh in range(ATTN_KV_HEADS):
        cur = nxt
        if kh + 1 < ATTN_KV_HEADS:
            nxt = scores(kh + 1)
        s = jnp.where(valid, cur, NEG_INF)
        sink = jnp.zeros((G * Q, 1), F32)
        for g in range(G):
            sink = jnp.where(row_head == g, sink_ref[kh * G + g], sink)
        m = jnp.maximum(jnp.max(s, axis=-1, keepdims=True), sink)
        pr = jnp.exp(s - m)
        den = jnp.sum(pr, axis=-1, keepdims=True) + jnp.exp(sink - m)
        o = _dot(pr.astype(BF16), v_all[:, kh * HEAD_DIM:(kh + 1) * HEAD_DIM]) / den
        for g in range(G):
            hq = kh * G + g
            o_ref[:, hq * HEAD_DIM:(hq + 1) * HEAD_DIM] = o[g * Q:(g + 1) * Q]


def _attention(qkv, sink, bsz, n_lat, n_ctx):
    Q = ATTN_BLOCK
    nblk = n_lat // Q
    lat0 = bsz * n_ctx // Q
    kcol, vcol = ATTN_Q_W // ATTN_KV_W, ATTN_Q_W // ATTN_KV_W + 1

    def kv_spec(col, off):
        return pl.BlockSpec((Q, ATTN_KV_W),
                            lambda b, i: (lat0 + b * nblk + jnp.clip(i + off, 0, nblk - 1), col))

    def ctx_spec(col):
        return pl.BlockSpec((n_ctx, ATTN_KV_W), lambda b, i: (b, col))

    return pl.pallas_call(
        functools.partial(_attn_kernel, nblk=nblk),
        grid=(bsz, nblk),
        in_specs=[pl.BlockSpec(memory_space=pltpu.SMEM),
                  pl.BlockSpec((Q, ATTN_Q_W), lambda b, i: (lat0 + b * nblk + i, 0)),
                  kv_spec(kcol, -1), kv_spec(kcol, 0), kv_spec(kcol, 1),
                  kv_spec(vcol, -1), kv_spec(vcol, 0), kv_spec(vcol, 1),
                  ctx_spec(kcol), ctx_spec(vcol)],
        out_specs=pl.BlockSpec((Q, ATTN_Q_W), lambda b, i: (b * nblk + i, 0)),
        out_shape=jax.ShapeDtypeStruct((bsz * n_lat, ATTN_Q_W), F32),
        compiler_params=_params("parallel", "parallel"),
        name="window_attention",
    )(sink, qkv, qkv, qkv, qkv, qkv, qkv, qkv, qkv, qkv)


def kernel(x, c, ctx, c_ctx, norm_g, mod_w, mod_b, ffn_w1, ffn_w2, ab_in_w, ab_out_w, rwkv_mu, rwkv_w0, rwkv_w2, rwkv_a0, rwkv_a2, rwkv_g2, rwkv_k_k, rwkv_k_a, rwkv_r_k, rwkv_lnx_g, rwkv_lnx_b, s5_a_re, s5_a_im, s5_log_step, s5_b_re, s5_b_im, s5_c_re, s5_c_im, s5_d, s5_glu_w, s5_glu_b, attn_in_w, attn_out_w, attn_sink, final_g):
    bsz, n_lat, _ = x.shape
    n_ctx = ctx.shape[1]
    depth = mod_w.shape[0]
    grid_w = 64
    assert n_ctx == ROW_TILE and n_lat % FFN_ROW_TILE == 0 and (bsz * n_ctx) % FFN_ROW_TILE == 0 and bsz <= 4
    seq = n_ctx + n_lat
    n_ctx_tiles = bsz
    n_lat_tiles = n_lat // ROW_TILE
    W = RWKV_WIDTH

    cs = jnp.zeros((8, D_MODEL), F32).at[:bsz].set(c).at[bsz].set(c_ctx)
    mods = _mod_vectors(jax.nn.silu(cs), mod_w, mod_b).reshape(depth, 8, N_MOD, D_MODEL)
    tile_row = jnp.concatenate([jnp.full((n_ctx_tiles,), bsz, jnp.int32),
                                jnp.repeat(jnp.arange(bsz, dtype=jnp.int32), n_lat_tiles)])
    modrows = mods[:, tile_row]

    xs = x.reshape(bsz * n_lat, D_MODEL)
    w1 = ffn_w1.astype(BF16)
    w2 = ffn_w2.astype(BF16)

    for l in range(depth):
        last = l == depth - 1
        mr = modrows[l]
        qkv_args = None
        if l % 2 == 1:
            cos_t, sin_t = _rope_tables(n_lat, grid_w)
            nct, nlt = bsz * n_ctx // FFN_ROW_TILE, n_lat // FFN_ROW_TILE

            def rope_tile(i):
                return (jnp.where(i < nct, 0, 1 + (i - nct) % nlt), 0)

            qkv_args = (norm_g[l, 1], attn_in_w[l // 2].astype(BF16), cos_t, sin_t, rope_tile)
        xs = _ffn(xs, mr, norm_g[l, 0], w1, w2, (l, 0), final_g, mod0=0, row_tile_of_mod=0,
                  x_ctx=ctx.reshape(bsz * n_ctx, D_MODEL) if l == 0 else None, qkv=qkv_args)
        if l % 2 == 0:
            e = l // 2
            w_lora = jnp.concatenate([
                _pack3(jnp.concatenate([rwkv_w2[e, 0], rwkv_w2[e, 1]], axis=1)),
                _pack3(jnp.concatenate([rwkv_a2[e, 0], rwkv_a2[e, 1]], axis=1)),
                _pack3(jnp.concatenate([rwkv_g2[e], jnp.zeros_like(rwkv_g2[e])], axis=1))], axis=0)
            vecs = jnp.stack([rwkv_k_k[e], rwkv_k_a[e], rwkv_r_k[e].reshape(W), rwkv_w0[e, 0], rwkv_w0[e, 1],
                              rwkv_a0[e, 0], rwkv_a0[e, 1], jnp.zeros((W,), F32)])
            ones3 = _head_ones3(W)
            pb, r, v, a, kd, bv, lw, g, bonus = _ab_in_proj(xs, mr, norm_g[l, 1], ab_in_w[e].astype(BF16), rwkv_mu[e],
                                                            w_lora, vecs, ones3, bsz, n_lat_tiles)
            yf, yb = _rwkv_scan(r, v, a, kd, bv, lw, n_ctx // RWKV_CHUNK)
            lam, bmat, cmat = _s5_discretise(s5_a_re[e], s5_a_im[e], s5_log_step[e], s5_b_re[e], s5_b_im[e],
                                             s5_c_re[e], s5_c_im[e])
            sf, sb = _s5_scan(pb, lam, bmat, cmat, n_ctx // S5_STEPS)
            vecs_out = jnp.stack([rwkv_lnx_g[e], rwkv_lnx_b[e], s5_d[e], s5_glu_b[e]] + [jnp.zeros((W,), F32)] * 4)
            xs = _ab_out(xs, mr, yf, yb, g, bonus, sf, sb, pb, vecs_out, s5_glu_w[e].astype(BF16),
                         ab_out_w[e].astype(BF16), ones3, bsz, n_lat_tiles)
            rows_mod0, attn = 0, None
        else:
            o = l // 2
            xs, qkv = xs
            att = _attention(qkv, attn_sink[o], bsz, n_lat, n_ctx)
            if not last:
                raise NotImplementedError("context update after an attention layer")
            rows_mod0, attn = n_ctx_tiles, (att, attn_out_w[o].astype(BF16))
        xs = _ffn(xs, mr, norm_g[l, 2], w1, w2, (l, 1), final_g, mod0=6, row_tile_of_mod=rows_mod0,
                  final=last, attn=attn)
    return xs.reshape(bsz, n_lat, D_MODEL)
```

```python
import functools
import math

import jax
import jax.numpy as jnp
from jax import lax
from jax.experimental import pallas as pl
from jax.experimental.pallas import tpu as pltpu

F32 = jnp.float32
BF16 = jnp.bfloat16

D_MODEL = 1024
D_FF = 2816
N_MOD = 9
HEAD_DIM = 64
RMS_EPS = 1e-6
GN_EPS = 64e-5
RWKV_WIDTH = 512
RWKV_PROJ = 1792
S5_WIDTH = 512
S5_GROUP_CH = 16
S5_GROUPS = 32
S5_STATE = 64
ATTN_HEADS = 16
ATTN_KV_HEADS = 4
ATTN_BLOCK = 128
ATTN_WINDOW = 128
ATTN_Q_W = 1024
ATTN_KV_W = 256
ROPE_BASE = 10000.0
NEG_INF = -1e30

ROW_TILE = 256
FFN_ROW_TILE = 512
FFN_F_CHUNKS = (1280, 1536)
RWKV_CHUNK = 64
S5_STEPS = 64
VMEM_LIMIT = 56 * 1024 * 1024

_NN = (((1,), (0,)), ((), ()))
_NT = (((1,), (1,)), ((), ()))


def _dot(a, b, dims=_NN):
    return lax.dot_general(a, b, dims, preferred_element_type=F32)


def _split2(a):
    hi = a.astype(BF16)
    lo = (a - hi.astype(F32)).astype(BF16)
    return hi, lo


def _split3(a):
    hi = a.astype(BF16)
    r1 = a - hi.astype(F32)
    mid = r1.astype(BF16)
    lo = (r1 - mid.astype(F32)).astype(BF16)
    return hi, mid, lo


def _mm1(a, b, dims=_NN):
    return _dot(a.astype(BF16), b.astype(BF16), dims)


def _mm3(a, b, dims=_NN):
    ah, al = _split2(a)
    bh, bl = _split2(b)
    return _dot(ah, bh, dims) + (_dot(ah, bl, dims) + _dot(al, bh, dims))


def _pack3(w):
    hi, lo = _split2(w)
    return jnp.concatenate([hi, lo, hi], axis=0)


def _dot3_packed(a, w3):
    hi, lo = _split2(a)
    return _dot(jnp.concatenate([hi, hi, lo], axis=1), w3)


def _head_sum(a, ones3):
    hi, mid, lo = _split3(a)
    return _dot(jnp.concatenate([hi, mid, lo], axis=1), ones3)


def _sigmoid(x):
    return 1.0 / (1.0 + jnp.exp(-x))


def _silu(x):
    return x * _sigmoid(x)


def _params(*sem):
    return pltpu.CompilerParams(dimension_semantics=sem, vmem_limit_bytes=VMEM_LIMIT)


def _modnorm(x, g, shift, scale):
    xn = x * lax.rsqrt(jnp.mean(x * x, axis=-1, keepdims=True) + RMS_EPS)
    return (xn * g) * (1.0 + scale) + shift


def _head_ones3(width):
    r = lax.broadcasted_iota(jnp.int32, (width, width), 0) >> 6
    c = lax.broadcasted_iota(jnp.int32, (width, width), 1) >> 6
    ones = (r == c).astype(BF16)
    return jnp.concatenate([ones, ones, ones], axis=0)


def _mod_kernel(c_ref, w_ref, b_ref, o_ref):
    o_ref[...] = _mm3(c_ref[...], w_ref[...]) + b_ref[...]


def _mod_vectors(cs, mod_w, mod_b):
    depth = mod_w.shape[0]
    n = mod_w.shape[2]
    tn = 2304
    return pl.pallas_call(
        _mod_kernel,
        grid=(depth, n // tn),
        in_specs=[pl.BlockSpec((8, D_MODEL), lambda l, j: (0, 0)),
                  pl.BlockSpec((None, D_MODEL, tn), lambda l, j: (l, 0, j)),
                  pl.BlockSpec((None, 1, tn), lambda l, j: (l, 0, j))],
        out_specs=pl.BlockSpec((None, 8, tn), lambda l, j: (l, 0, j)),
        out_shape=jax.ShapeDtypeStruct((depth, 8, n), F32),
        compiler_params=_params("parallel", "parallel"),
        name="mod_vectors",
    )(cs, mod_w, mod_b.reshape(depth, 1, n))


def _ffn_kernel(x_ref, xc_ref, att_ref, wo_ref, m_ref, g_ref, w1_ref, w2_ref, fg_ref, *rest,
                mod0, final, n_ctx_tiles, attn_residual, qkv_proj):
    if qkv_proj:
        gq_ref, wq_ref, cos_ref, sin_ref, o_ref, qkv_ref = rest
    else:
        (o_ref,) = rest
    x = x_ref[...]
    if n_ctx_tiles:
        x = jnp.where(pl.program_id(0) < n_ctx_tiles, xc_ref[...], x)
    if attn_residual:
        x = x + m_ref[5:6, :] * _dot(att_ref[...].astype(BF16), wo_ref[...])
    h = _modnorm(x, g_ref[...], m_ref[mod0:mod0 + 1, :], m_ref[mod0 + 1:mod0 + 2, :]).astype(BF16)
    acc = None
    f0 = 0
    for fc in FFN_F_CHUNKS:
        gate = _dot(h, w1_ref[:, f0:f0 + fc])
        up = _dot(h, w1_ref[:, D_FF + f0:D_FF + f0 + fc])
        part = _dot((_silu(gate) * up).astype(BF16), w2_ref[f0:f0 + fc, :])
        acc = part if acc is None else acc + part
        f0 += fc
    y = x + (0.5 * m_ref[mod0 + 2:mod0 + 3, :]) * acc
    if final:
        y = y * lax.rsqrt(jnp.mean(y * y, axis=-1, keepdims=True) + RMS_EPS) * fg_ref[...]
    o_ref[...] = y
    if qkv_proj:
        _qkv_rope(y, m_ref, gq_ref, wq_ref, cos_ref, sin_ref, qkv_ref)


def _ffn(x, modrows, g, w1, w2, lj, final_g, *, mod0, row_tile_of_mod, final=False, x_ctx=None, attn=None,
         qkv=None):
    tm = FFN_ROW_TILE
    step = tm // ROW_TILE
    nct = 0 if x_ctx is None else x_ctx.shape[0] // tm
    x_off = 0
    rows = x.shape[0] + nct * tm
    once = pl.Buffered(1)
    small = pl.BlockSpec((8, D_MODEL), lambda i: (0, 0))
    if x_ctx is None:
        x_ctx, ctx_spec = x, small
    else:
        ctx_spec = pl.BlockSpec((tm, D_MODEL), lambda i: (jnp.minimum(i, nct - 1), 0))
    if attn is None:
        att, w_out, att_spec, wo_spec = x, x, small, small
    else:
        att, w_out = attn
        rows = att.shape[0]
        x_off = row_tile_of_mod // step
        att_spec = pl.BlockSpec((tm, D_MODEL), lambda i: (i, 0))
        wo_spec = pl.BlockSpec(w_out.shape, lambda i: (0, 0), pipeline_mode=once)
    layer, which = lj
    operands = [x, x_ctx, att, w_out, modrows, g.reshape(1, D_MODEL), w1, w2, final_g.reshape(1, D_MODEL)]
    in_specs = [pl.BlockSpec((tm, D_MODEL), lambda i: (x_off + jnp.maximum(i - nct, 0), 0)),
                ctx_spec, att_spec, wo_spec,
                pl.BlockSpec((None, N_MOD, D_MODEL), lambda i: (row_tile_of_mod + i * step, 0, 0)),
                pl.BlockSpec((1, D_MODEL), lambda i: (0, 0)),
                pl.BlockSpec((None, None) + w1.shape[2:], lambda i: (layer, which, 0, 0), pipeline_mode=once),
                pl.BlockSpec((None, None) + w2.shape[2:], lambda i: (layer, which, 0, 0), pipeline_mode=once),
                pl.BlockSpec((1, D_MODEL), lambda i: (0, 0))]
    out_specs = [pl.BlockSpec((tm, D_MODEL), lambda i: (i, 0))]
    out_shape = [jax.ShapeDtypeStruct((rows, D_MODEL), F32)]
    if qkv is not None:
        g_mix, w_in, cos_t, sin_t, rope_tile = qkv
        operands += [g_mix.reshape(1, D_MODEL), w_in, cos_t, sin_t]
        in_specs += [pl.BlockSpec((1, D_MODEL), lambda i: (0, 0)),
                     pl.BlockSpec(w_in.shape, lambda i: (0, 0), pipeline_mode=once),
                     pl.BlockSpec((tm, 128), rope_tile), pl.BlockSpec((tm, 128), rope_tile)]
        out_specs.append(pl.BlockSpec((tm, w_in.shape[1]), lambda i: (i, 0)))
        out_shape.append(jax.ShapeDtypeStruct((rows, w_in.shape[1]), BF16))
    out = pl.pallas_call(
        functools.partial(_ffn_kernel, mod0=mod0, final=final, n_ctx_tiles=nct, attn_residual=attn is not None,
                          qkv_proj=qkv is not None),
        grid=(rows // tm,),
        in_specs=in_specs,
        out_specs=out_specs,
        out_shape=out_shape,
        compiler_params=_params("parallel"),
        name="ffn",
    )(*operands)
    return out if qkv is not None else out[0]


def _x_tile(bsz, n_lat_tiles):
    return lambda b, k: (jnp.where(k == 0, b, bsz + b * n_lat_tiles + k - 1), 0)


def _ab_in_kernel(x_ref, xp_ref, xn_ref, m_ref, g_ref, w_ref, mu_ref, wl_ref, vec_ref, ones_ref,
                  pb_ref, r_ref, v_ref, a_ref, k_ref, b_ref, lw_ref, g_out_ref, bonus_ref):
    k_idx = pl.program_id(1)
    nt = pl.num_programs(1)
    xa = jnp.concatenate([xp_ref[...], x_ref[...], xn_ref[...]], axis=0)
    h = _modnorm(xa, g_ref[...], m_ref[3:4, :], m_ref[4:5, :]).astype(BF16)
    p_all = _dot(h, w_ref[...])
    rows = x_ref.shape[0]
    pb_ref[...] = p_all[8:8 + rows, RWKV_PROJ:]
    cur = p_all[8:8 + rows, :RWKV_PROJ]
    row = lax.broadcasted_iota(jnp.int32, cur.shape, 0)
    use_prev = k_idx >= 2
    use_next = jnp.logical_and(k_idx >= 1, k_idx < nt - 1)
    prev_row = jnp.where(use_prev, p_all[7:8, :RWKV_PROJ], 0.0)
    next_row = jnp.where(use_next, p_all[8 + rows:9 + rows, :RWKV_PROJ], 0.0)
    prev = jnp.where(row == 0, prev_row, pltpu.roll(cur, 1, axis=0))
    nxt = jnp.where(row == rows - 1, next_row, pltpu.roll(cur, rows - 1, axis=0))
    p = cur + mu_ref[0:1, :] * (prev - cur) + mu_ref[1:2, :] * (nxt - cur)

    W = RWKV_WIDTH
    k_k, k_a, r_k = vec_ref[0:1, :], vec_ref[1:2, :], vec_ref[2:3, :]
    ones = ones_ref[...]
    halves = [slice(0, rows // 2), slice(rows // 2, rows)]
    stage1 = []
    for hs in halves:
        ph = p[hs]
        kk = ph[:, W:2 * W] * k_k
        stage1.append(dict(
            kk=kk,
            lo_w=_dot3_packed(jnp.tanh(ph[:, 3 * W:3 * W + 64]), wl_ref[0:192, :]),
            lo_a=_dot3_packed(ph[:, 3 * W + 64:3 * W + 128], wl_ref[192:384, :]),
            gate=_dot3_packed(_sigmoid(ph[:, 3 * W + 128:]), wl_ref[384:768, 0:W]),
            norm2=_head_sum(kk * kk, ones)))
    for hs, s in zip(halves, stage1):
        ph = p[hs]
        r, k, v = ph[:, :W], ph[:, W:2 * W], ph[:, 2 * W:3 * W]
        kk = s["kk"] * lax.rsqrt(jnp.maximum(s["norm2"], 1e-12))
        r_ref[hs, :] = r
        v_ref[hs, :] = v
        a_ref[hs, :] = -kk
        k_sum = jnp.zeros_like(k)
        for d in range(2):
            z = -(vec_ref[3 + d:4 + d, :] + s["lo_w"][:, d * W:(d + 1) * W])
            softplus = jnp.maximum(z, 0.0) + jnp.log(1.0 + jnp.exp(-jnp.abs(z)))
            lw_ref[d, hs, :] = -jnp.exp(-softplus - 0.5)
            a = _sigmoid(vec_ref[5 + d:6 + d, :] + s["lo_a"][:, d * W:(d + 1) * W])
            kd = k * (1.0 + (a - 1.0) * k_a)
            k_ref[d, hs, :] = kd
            b_ref[d, hs, :] = kk * a
            k_sum = k_sum + kd
        g_out_ref[hs, :] = s["gate"]
        bonus_ref[hs, :] = _head_sum(r * (0.5 * k_sum) * r_k, ones) * v


def _ab_in_proj(x, modrows, g, w, mu, w_lora, vecs, ones, bsz, n_lat_tiles):
    nt = n_lat_tiles + 1
    seq = nt * ROW_TILE
    sub = ROW_TILE // 8
    last_blk = x.shape[0] // 8 - 1
    xt = _x_tile(bsz, n_lat_tiles)
    W = RWKV_WIDTH
    one = jax.ShapeDtypeStruct((bsz, seq, W), F32)
    two = jax.ShapeDtypeStruct((2, bsz, seq, W), F32)
    spec1 = pl.BlockSpec((None, ROW_TILE, W), lambda b, k: (b, k, 0))
    spec2 = pl.BlockSpec((2, None, ROW_TILE, W), lambda b, k: (0, b, k, 0))
    const = lambda b, k: (0, 0)
    return pl.pallas_call(
        _ab_in_kernel,
        grid=(bsz, nt),
        in_specs=[pl.BlockSpec((ROW_TILE, D_MODEL), xt),
                  pl.BlockSpec((8, D_MODEL), lambda b, k: (jnp.maximum(xt(b, k)[0] * sub - 1, 0), 0)),
                  pl.BlockSpec((8, D_MODEL), lambda b, k: (jnp.minimum((xt(b, k)[0] + 1) * sub, last_blk), 0)),
                  pl.BlockSpec((None, N_MOD, D_MODEL), lambda b, k: (xt(b, k)[0], 0, 0)),
                  pl.BlockSpec((1, D_MODEL), const),
                  pl.BlockSpec(w.shape, const),
                  pl.BlockSpec(mu.shape, const),
                  pl.BlockSpec(w_lora.shape, const),
                  pl.BlockSpec(vecs.shape, const),
                  pl.BlockSpec(ones.shape, const)],
        out_specs=[spec1, spec1, spec1, spec1, spec2, spec2, spec2, spec1, spec1],
        out_shape=[one, one, one, one, two, two, two, one, one],
        compiler_params=_params("parallel", "parallel"),
        name="ab_in_proj",
    )(x, x, x, modrows, g.reshape(1, D_MODEL), w, mu, w_lora, vecs, ones)


RWKV_TILE_HEADS = 2
RWKV_TILE = RWKV_TILE_HEADS * HEAD_DIM
RWKV_BATCHES_PER_STEP = 4


def _rwkv_scaled(fwd, r, v, a, k, b, lw):
    C = RWKV_CHUNK
    rr = lax.broadcasted_iota(jnp.int32, (C, C), 0)
    cc = lax.broadcasted_iota(jnp.int32, (C, C), 1)
    tri = ((rr >= cc) if fwd else (rr <= cc)).astype(BF16)
    hi, mid, lo = _split3(lw)
    cum = _dot(jnp.concatenate([tri, tri, tri], axis=1), jnp.concatenate([hi, mid, lo], axis=0))
    last = cum[C - 1:C, :] if fwd else cum[0:1, :]
    einv = jnp.exp(-cum)
    to_end = jnp.exp(last - cum)
    return dict(at=a * jnp.exp(cum - lw), rt=r * jnp.exp(cum), bt=b * einv, kt=k * einv,
                b_end=b * to_end, k_end=k * to_end, p_end=jnp.exp(last), v=v)


def _rwkv_chunks(dirs, h_ref, y_refs):
    C = RWKV_CHUNK
    TW = RWKV_TILE
    tq = lax.broadcasted_iota(jnp.int32, (C, TW), 0)
    sq = lax.broadcasted_iota(jnp.int32, (C, TW), 1) & (C - 1)
    eye_cat = (tq == sq).astype(F32)
    rb = lax.broadcasted_iota(jnp.int32, (TW, TW), 0)
    cb = lax.broadcasted_iota(jnp.int32, (TW, TW), 1)
    same_head = (rb >> 6) == (cb >> 6)
    diag = rb == cb

    def stack(x):
        return jnp.where(same_head, jnp.concatenate([x] * RWKV_TILE_HEADS, axis=0), jnp.zeros((), x.dtype))

    chains = []
    for di, (fwd, op) in enumerate(dirs):
        strict = (tq > sq) if fwd else (tq < sq)
        incl = (tq >= sq) if fwd else (tq <= sq)
        for q in range(RWKV_WIDTH // TW):
            sl = slice(TW * q, TW * (q + 1))
            ch = {name: val[:, sl] for name, val in op.items()}
            ch.update(strict=strict, incl=incl, di=di, q=q, sl=sl)
            chains.append(ch)

    for ch in chains:
        lhs = jnp.concatenate([ch["at"], ch["rt"]], axis=0).astype(BF16)
        rhs = jnp.concatenate([stack(ch["bt"].astype(BF16)), stack(ch["kt"].astype(BF16))], axis=0)
        s_all = _dot(lhs, rhs, _NT)
        ch["n_ab"] = jnp.where(ch["strict"], s_all[:C, :TW], 0.0)
        ch["n_ak"] = jnp.where(ch["strict"], s_all[:C, TW:], 0.0)
        ch["n_rb"] = jnp.where(ch["incl"], s_all[C:, :TW], 0.0)
        ch["n_rk"] = jnp.where(ch["incl"], s_all[C:, TW:], 0.0)
        ch["tinv"] = eye_cat + ch["n_ab"]
        ch["pw"] = ch["n_ab"]

    levels = int(math.log2(C))
    for i in range(levels):
        for ch in chains:
            pw_b = ch["pw"].astype(BF16)
            if i == 0:
                ch["pw"] = _dot(pw_b, stack(pw_b))
                v_s = stack(ch["v"].astype(BF16))
                ch["kv"] = _dot(jnp.concatenate([ch["n_ak"], ch["n_rk"]], axis=0).astype(BF16), v_s)
            elif i == levels - 1:
                ch["tinv"] = ch["tinv"] + _dot(pw_b, stack(ch["tinv"].astype(BF16)))
            else:
                res = _dot(pw_b, jnp.concatenate([stack(pw_b), stack(ch["tinv"].astype(BF16))], axis=1))
                ch["pw"] = res[:, :TW]
                ch["tinv"] = ch["tinv"] + res[:, TW:]

    for ch in chains:
        nh, nl = _split2(ch["n_ab"])
        xh, xl = _split2(ch["tinv"])
        ch["xh"] = xh
        ch["resid"] = (eye_cat - ch["tinv"]) + _dot(jnp.concatenate([nh, nh, nl], axis=1),
                                                     jnp.concatenate([stack(xh), stack(xl), stack(xh)], axis=0))
    for ch in chains:
        ch["tinv"] = ch["tinv"] + _dot(ch["xh"], stack(ch["resid"].astype(BF16)))

    for ch in chains:
        rhs = jnp.concatenate([stack(ch["at"].astype(BF16)), stack(ch["kv"][:C].astype(BF16))], axis=1)
        ch["aw"] = _dot(ch["tinv"].astype(BF16), rhs).astype(BF16)

    for ch in chains:
        aw = ch["aw"]
        c1 = _dot(ch["n_rb"].astype(BF16), jnp.concatenate([stack(aw[:, :TW]), stack(aw[:, TW:])], axis=1))
        ch["qm"] = ch["rt"] + c1[:, :TW]
        ch["z"] = c1[:, TW:] + ch["kv"][C:]
        bk_t = jnp.concatenate([ch["b_end"], ch["k_end"]], axis=0).T.astype(BF16)
        low = jnp.concatenate([jnp.zeros((C, TW), BF16), ch["v"].astype(BF16)], axis=1)
        c2 = _dot(bk_t, jnp.concatenate([aw, low], axis=0))
        ch["m_upd"] = jnp.where(same_head, c2[:, :TW], 0.0)
        ch["g_upd"] = jnp.where(same_head, c2[:, TW:], 0.0)

    for ch in chains:
        h = h_ref[ch["di"], ch["q"]]
        d = _dot(jnp.concatenate([ch["m_upd"], ch["qm"]], axis=0).astype(BF16), h.astype(BF16))
        y_refs[ch["di"]][:, ch["sl"]] = d[TW:] + ch["z"]
        p_col = jnp.sum(jnp.where(diag, ch["p_end"], 0.0), axis=1, keepdims=True)
        h_ref[ch["di"], ch["q"]] = p_col * h + d[:TW] + ch["g_upd"]


def _rwkv_scan_kernel(rf_ref, vf_ref, af_ref, rb_ref, vb_ref, ab_ref, kf_ref, bf_ref, lwf_ref,
                      kb_ref, bb_ref, lwb_ref, yf_ref, yb_ref, h_ref, *, nb):
    @pl.when(pl.program_id(1) == 0)
    def _():
        h_ref[...] = jnp.zeros_like(h_ref)

    dirs, y_refs = [], []
    for i in range(nb):
        dirs.append((True, _rwkv_scaled(True, rf_ref[i], vf_ref[i], af_ref[i], kf_ref[i], bf_ref[i], lwf_ref[i])))
        dirs.append((False, _rwkv_scaled(False, rb_ref[i], vb_ref[i], ab_ref[i], kb_ref[i], bb_ref[i], lwb_ref[i])))
        y_refs += [yf_ref.at[i], yb_ref.at[i]]
    _rwkv_chunks(dirs, h_ref, y_refs)


def _rwkv_scan(r, v, a, k, b, lw, n_ctx_chunks):
    bsz, seq, W = r.shape
    C = RWKV_CHUNK
    nch = seq // C

    def back(j):
        return jnp.where(j < n_ctx_chunks, n_ctx_chunks - 1 - j, nch - 1 - (j - n_ctx_chunks))

    nb = math.gcd(bsz, RWKV_BATCHES_PER_STEP)
    one_f = pl.BlockSpec((nb, C, W), lambda g, j: (g, j, 0))
    one_b = pl.BlockSpec((nb, C, W), lambda g, j: (g, back(j), 0))
    two_f = pl.BlockSpec((None, nb, C, W), lambda g, j: (0, g, j, 0))
    two_b = pl.BlockSpec((None, nb, C, W), lambda g, j: (1, g, back(j), 0))
    out = jax.ShapeDtypeStruct((bsz, seq, W), F32)
    return pl.pallas_call(
        functools.partial(_rwkv_scan_kernel, nb=nb),
        grid=(bsz // nb, nch),
        in_specs=[one_f, one_f, one_f, one_b, one_b, one_b, two_f, two_f, two_f, two_b, two_b, two_b],
        out_specs=[one_f, one_b],
        out_shape=[out, out],
        scratch_shapes=[pltpu.VMEM((2 * nb, W // RWKV_TILE, RWKV_TILE, RWKV_TILE), F32)],
        compiler_params=_params("parallel", "arbitrary"),
        name="rwkv_scan",
    )(r, v, a, r, v, a, k, b, lw, k, b, lw)


def _s5_scan_kernel(uf_ref, ub_ref, lam_ref, bmat_ref, cmat_ref, yf_ref, yb_ref, bu_ref, st_ref, *, bsz):
    S = S5_STEPS
    half = 8 * S5_STATE
    n_nat = 2 * bsz * S

    @pl.when(pl.program_id(0) == 0)
    def _():
        st_ref[...] = jnp.zeros_like(st_ref)

    def perm_mask(rho, col):
        t, s = rho >> 3, rho & 7
        src = jnp.where(s < 4, s * S + t, bsz * S + (s - 4) * S + (S - 1 - t))
        return jnp.logical_and((s & 3) < bsz, col == src).astype(BF16)

    perm = perm_mask(lax.broadcasted_iota(jnp.int32, (S * 8, n_nat), 0),
                     lax.broadcasted_iota(jnp.int32, (S * 8, n_nat), 1))
    perm_t = perm_mask(lax.broadcasted_iota(jnp.int32, (n_nat, S * 8), 1),
                       lax.broadcasted_iota(jnp.int32, (n_nat, S * 8), 0))
    u_nat = jnp.concatenate([uf_ref[...].reshape(bsz * S, S5_WIDTH), ub_ref[...].reshape(bsz * S, S5_WIDTH)], axis=0)
    u = _dot(perm, u_nat.astype(BF16))
    is_fwd = (lax.broadcasted_iota(jnp.int32, (S * 8, 128), 0) & 7) < 4
    nq = S5_WIDTH // 128
    ys = []
    for q in range(nq):
        uq = u[:, 128 * q:128 * (q + 1)]
        lhs = jnp.concatenate([jnp.where(is_fwd, uq, 0.0), jnp.where(is_fwd, 0.0, uq)], axis=1)
        bu_ref[:, 2 * half * q:2 * half * (q + 1)] = _dot(lhs.astype(BF16), bmat_ref[q])
    for q in range(nq):
        base = 2 * half * q
        re, im = slice(base, base + half), slice(base + half, base + 2 * half)
        ar, ai = lam_ref[:, re], lam_ref[:, im]
        xr, xi = st_ref[:, re], st_ref[:, im]
        for t in range(S):
            rows = slice(8 * t, 8 * t + 8)
            xr, xi = (ar * xr - ai * xi + bu_ref[rows, re], ar * xi + ai * xr + bu_ref[rows, im])
            bu_ref[rows, re] = xr
            bu_ref[rows, im] = xi
        st_ref[:, re] = xr
        st_ref[:, im] = xi
        y2 = _dot(bu_ref[:, base:base + 2 * half].astype(BF16), cmat_ref[q])
        ys.append(jnp.where(is_fwd, y2[:, :128], y2[:, 128:]))
    hi, lo = _split2(jnp.concatenate(ys, axis=1))
    y_nat = _dot(jnp.concatenate([perm_t, perm_t], axis=1), jnp.concatenate([hi, lo], axis=0))
    yf_ref[...] = y_nat[:bsz * S].reshape(bsz, S, S5_WIDTH)
    yb_ref[...] = y_nat[bsz * S:].reshape(bsz, S, S5_WIDTH)


def _s5_scan(u, lam, bmat, cmat, n_ctx_blocks):
    bsz, seq, _ = u.shape
    S = S5_STEPS
    nblk = seq // S
    nstate = 2 * S5_GROUPS * S5_STATE

    def back(i):
        return jnp.where(i < n_ctx_blocks, n_ctx_blocks - 1 - i, nblk - 1 - (i - n_ctx_blocks))

    spec_f = pl.BlockSpec((bsz, S, S5_WIDTH), lambda i: (0, i, 0))
    spec_b = pl.BlockSpec((bsz, S, S5_WIDTH), lambda i: (0, back(i), 0))
    out = jax.ShapeDtypeStruct(u.shape, F32)
    return pl.pallas_call(
        functools.partial(_s5_scan_kernel, bsz=bsz),
        grid=(nblk,),
        in_specs=[spec_f, spec_b,
                  pl.BlockSpec(lam.shape, lambda i: (0, 0)),
                  pl.BlockSpec(bmat.shape, lambda i: (0, 0, 0)),
                  pl.BlockSpec(cmat.shape, lambda i: (0, 0, 0))],
        out_specs=[spec_f, spec_b],
        out_shape=[out, out],
        scratch_shapes=[pltpu.VMEM((S * 8, nstate), F32), pltpu.VMEM((8, nstate), F32)],
        compiler_params=_params("arbitrary"),
        name="s5_scan",
    )(u, u, lam, bmat, cmat)


def _s5_discretise(a_re, a_im, log_step, b_re, b_im, c_re, c_im):
    lam_re = jnp.minimum(a_re, -1e-4)
    lam_im = a_im
    dt = jnp.exp(log_step)[..., None]
    mag = jnp.exp(lam_re * dt)
    ab_re, ab_im = mag * jnp.cos(lam_im * dt), mag * jnp.sin(lam_im * dt)
    den = lam_re * lam_re + lam_im * lam_im
    f_re = ((ab_re - 1.0) * lam_re + ab_im * lam_im) / den
    f_im = (ab_im * lam_re - (ab_re - 1.0) * lam_im) / den
    bb_re = f_re[..., None] * b_re - f_im[..., None] * b_im
    bb_im = f_re[..., None] * b_im + f_im[..., None] * b_re
    nq = S5_WIDTH // 128
    eye8 = jnp.eye(8, dtype=F32)

    def lanes(t):
        return t.reshape(2, nq, 8 * S5_STATE)

    lam = jnp.concatenate([lanes(ab_re), lanes(ab_im)], axis=-1).reshape(2, nq * 16 * S5_STATE)
    lam = jnp.repeat(lam, 4, axis=0)

    def in_block(t):
        t = t.reshape(2, nq, 8, S5_STATE, S5_GROUP_CH)
        return jnp.einsum('dqgpi,gh->dqgihp', t, eye8).reshape(2, nq, 128, 8 * S5_STATE)

    bmat = jnp.concatenate([in_block(bb_re), in_block(bb_im)], axis=-1)
    bmat = jnp.concatenate([bmat[0], bmat[1]], axis=1)

    def out_block(t):
        t = t.reshape(2, nq, 8, S5_GROUP_CH, S5_STATE)
        return jnp.einsum('dqgip,gh->dqgphi', t, eye8).reshape(2, nq, 8 * S5_STATE, 128)

    cmat = jnp.concatenate([out_block(c_re), -out_block(c_im)], axis=2)
    cmat = jnp.concatenate([cmat[0], cmat[1]], axis=-1)
    return lam, bmat.astype(BF16), cmat.astype(BF16)


def _ab_out_kernel(x_ref, m_ref, yf_ref, yb_ref, g_ref, bonus_ref, sf_ref, sb_ref, u_ref, vec_ref, gluw_ref, w_ref,
                   ones_ref, o_ref):
    W = RWKV_WIDTH
    ones = ones_ref[...]
    y = yf_ref[...] + yb_ref[...]
    mu = _head_sum(y, ones) * (1.0 / HEAD_DIM)
    yc = y - mu
    var = _head_sum(yc * yc, ones) * (1.0 / HEAD_DIM)
    ya = (yc * lax.rsqrt(var + GN_EPS) * vec_ref[0:1, :] + vec_ref[1:2, :] + bonus_ref[...]) * g_ref[...]

    s = (sf_ref[...] + sb_ref[...]) + vec_ref[2:3, :] * u_ref[...]
    z = 0.5 * s * (1.0 + jnp.tanh(math.sqrt(2.0 / math.pi) * (s + 0.044715 * (s * s * s))))
    yb = z * _sigmoid(_mm1(z, gluw_ref[...]) + vec_ref[3:4, :])
    out = _dot(ya.astype(BF16), w_ref[:W, :]) + _dot(yb.astype(BF16), w_ref[W:, :])
    o_ref[...] = x_ref[...] + m_ref[5:6, :] * out


def _ab_out(x, modrows, yf, yb, g, bonus, sf, sb, u, vecs, glu_w, out_w, ones, bsz, n_lat_tiles):
    nt = n_lat_tiles + 1
    W = RWKV_WIDTH
    xt = _x_tile(bsz, n_lat_tiles)
    seq_spec = pl.BlockSpec((None, ROW_TILE, W), lambda b, k: (b, k, 0))
    return pl.pallas_call(
        _ab_out_kernel,
        grid=(bsz, nt),
        in_specs=[pl.BlockSpec((ROW_TILE, D_MODEL), xt),
                  pl.BlockSpec((None, N_MOD, D_MODEL), lambda b, k: (xt(b, k)[0], 0, 0)),
                  seq_spec, seq_spec, seq_spec, seq_spec, seq_spec, seq_spec, seq_spec,
                  pl.BlockSpec(vecs.shape, lambda b, k: (0, 0)),
                  pl.BlockSpec(glu_w.shape, lambda b, k: (0, 0)),
                  pl.BlockSpec(out_w.shape, lambda b, k: (0, 0)),
                  pl.BlockSpec(ones.shape, lambda b, k: (0, 0))],
        out_specs=pl.BlockSpec((ROW_TILE, D_MODEL), xt),
        out_shape=jax.ShapeDtypeStruct(x.shape, F32),
        compiler_params=_params("parallel", "parallel"),
        name="ab_out",
    )(x, modrows, yf, yb, g, bonus, sf, sb, u, vecs, glu_w, out_w, ones)


def _qkv_rope(x, m_ref, g_ref, w_ref, cos_ref, sin_ref, o_ref):
    h = _modnorm(x, g_ref[...], m_ref[3:4, :], m_ref[4:5, :]).astype(BF16)
    p = _dot(h, w_ref[...])
    qk_w = ATTN_Q_W + ATTN_KV_W
    qk = p[:, :qk_w]
    half = HEAD_DIM // 2
    lane = lax.broadcasted_iota(jnp.int32, qk.shape, 1)
    first = (lane & (HEAD_DIM - 1)) < half
    partner = jnp.where(first, pltpu.roll(qk, qk_w - half, axis=1), pltpu.roll(qk, half, axis=1))
    reps = qk_w // 128
    cos = jnp.concatenate([cos_ref[...]] * reps, axis=1)
    sin = jnp.concatenate([sin_ref[...]] * reps, axis=1)
    o_ref[:, :qk_w] = (qk * cos + partner * sin).astype(o_ref.dtype)
    o_ref[:, qk_w:] = p[:, qk_w:].astype(o_ref.dtype)


def _rope_tables(n_lat, grid_w):
    half = HEAD_DIM // 2
    t = jnp.arange(n_lat)
    row_id = (t // grid_w).astype(F32)
    col_id = (t % grid_w).astype(F32)
    inv_freq = ROPE_BASE ** (-jnp.arange(0, half, 2, dtype=F32) / half)
    ang = jnp.concatenate([row_id[:, None] * inv_freq, col_id[:, None] * inv_freq], axis=-1)
    cos, sin = jnp.cos(ang), jnp.sin(ang)
    cos_t = jnp.concatenate([cos, cos, cos, cos], axis=-1)
    sin_t = jnp.concatenate([-sin, sin, -sin, sin], axis=-1)
    cos_t = jnp.concatenate([jnp.ones((FFN_ROW_TILE, 128), F32), cos_t], axis=0)
    sin_t = jnp.concatenate([jnp.zeros((FFN_ROW_TILE, 128), F32), sin_t], axis=0)
    return cos_t, sin_t


def _attn_kernel(sink_ref, q_ref, kp_ref, kc_ref, kn_ref, vp_ref, vc_ref, vn_ref, kx_ref, vx_ref, o_ref, *, nblk):
    i = pl.program_id(1)
    Q = ATTN_BLOCK
    G = ATTN_HEADS // ATTN_KV_HEADS
    k_all = jnp.concatenate([kp_ref[...], kc_ref[...], kn_ref[...], kx_ref[...]], axis=0)
    v_all = jnp.concatenate([vp_ref[...], vc_ref[...], vn_ref[...], vx_ref[...]], axis=0)
    nk = k_all.shape[0]
    qi = lax.broadcasted_iota(jnp.int32, (G * Q, nk), 0) & (Q - 1)
    cj = lax.broadcasted_iota(jnp.int32, (G * Q, nk), 1)
    mj = cj - Q
    blk = jnp.where(cj < Q, i - 1, jnp.where(cj < 2 * Q, i, i + 1))
    valid = jnp.logical_and(jnp.abs(mj - qi) <= ATTN_WINDOW, jnp.logical_and(blk >= 0, blk < nblk))
    valid = jnp.logical_or(valid, cj >= 3 * Q)
    q = q_ref[...] * (HEAD_DIM ** -0.5)
    row_head = lax.broadcasted_iota(jnp.int32, (G * Q, 1), 0) >> 7
    def scores(kh):
        q4 = jnp.concatenate([q[:, (kh * G + g) * HEAD_DIM:(kh * G + g + 1) * HEAD_DIM] for g in range(G)], axis=0)
        return _dot(q4, k_all[:, kh * HEAD_DIM:(kh + 1) * HEAD_DIM], _NT)

    nxt = scores(0)
    for kh in range(ATTN_KV_HEADS):
        cur = nxt
        if kh + 1 < ATTN_KV_HEADS:
            nxt = scores(kh + 1)
        s = jnp.where(valid, cur, NEG_INF)
        sink = jnp.zeros((G * Q, 1), F32)
        for g in range(G):
            sink = jnp.where(row_head == g, sink_ref[kh * G + g], sink)
        m = jnp.maximum(jnp.max(s, axis=-1, keepdims=True), sink)
        pr = jnp.exp(s - m)
        den = jnp.sum(pr, axis=-1, keepdims=True) + jnp.exp(sink - m)
        o = _dot(pr.astype(BF16), v_all[:, kh * HEAD_DIM:(kh + 1) * HEAD_DIM]) / den
        for g in range(G):
            hq = kh * G + g
            o_ref[:, hq * HEAD_DIM:(hq + 1) * HEAD_DIM] = o[g * Q:(g + 1) * Q]


def _attention(qkv, sink, bsz, n_lat, n_ctx):
    Q = ATTN_BLOCK
    nblk = n_lat // Q
    lat0 = bsz * n_ctx // Q
    kcol, vcol = ATTN_Q_W // ATTN_KV_W, ATTN_Q_W // ATTN_KV_W + 1

    def kv_spec(col, off):
        return pl.BlockSpec((Q, ATTN_KV_W),
                            lambda b, i: (lat0 + b * nblk + jnp.clip(i + off, 0, nblk - 1), col))

    def ctx_spec(col):
        return pl.BlockSpec((n_ctx, ATTN_KV_W), lambda b, i: (b, col))

    return pl.pallas_call(
        functools.partial(_attn_kernel, nblk=nblk),
        grid=(bsz, nblk),
        in_specs=[pl.BlockSpec(memory_space=pltpu.SMEM),
                  pl.BlockSpec((Q, ATTN_Q_W), lambda b, i: (lat0 + b * nblk + i, 0)),
                  kv_spec(kcol, -1), kv_spec(kcol, 0), kv_spec(kcol, 1),
                  kv_spec(vcol, -1), kv_spec(vcol, 0), kv_spec(vcol, 1),
                  ctx_spec(kcol), ctx_spec(vcol)],
        out_specs=pl.BlockSpec((Q, ATTN_Q_W), lambda b, i: (b * nblk + i, 0)),
        out_shape=jax.ShapeDtypeStruct((bsz * n_lat, ATTN_Q_W), F32),
        compiler_params=_params("parallel", "parallel"),
        name="window_attention",
    )(sink, qkv, qkv, qkv, qkv, qkv, qkv, qkv, qkv, qkv)


def kernel(x, c, ctx, c_ctx, norm_g, mod_w, mod_b, ffn_w1, ffn_w2, ab_in_w, ab_out_w, rwkv_mu, rwkv_w0, rwkv_w2, rwkv_a0, rwkv_a2, rwkv_g2, rwkv_k_k, rwkv_k_a, rwkv_r_k, rwkv_lnx_g, rwkv_lnx_b, s5_a_re, s5_a_im, s5_log_step, s5_b_re, s5_b_im, s5_c_re, s5_c_im, s5_d, s5_glu_w, s5_glu_b, attn_in_w, attn_out_w, attn_sink, final_g):
    bsz, n_lat, _ = x.shape
    n_ctx = ctx.shape[1]
    depth = mod_w.shape[0]
    grid_w = 64
    assert n_ctx == ROW_TILE and n_lat % FFN_ROW_TILE == 0 and (bsz * n_ctx) % FFN_ROW_TILE == 0 and bsz <= 4
    n_ctx_tiles = bsz
    n_lat_tiles = n_lat // ROW_TILE
    W = RWKV_WIDTH

    cs = jnp.zeros((8, D_MODEL), F32).at[:bsz].set(c).at[bsz].set(c_ctx)
    mods = _mod_vectors(jax.nn.silu(cs), mod_w, mod_b).reshape(depth, 8, N_MOD, D_MODEL)
    tile_row = jnp.concatenate([jnp.full((n_ctx_tiles,), bsz, jnp.int32),
                                jnp.repeat(jnp.arange(bsz, dtype=jnp.int32), n_lat_tiles)])
    modrows = mods[:, tile_row]

    xs = x.reshape(bsz * n_lat, D_MODEL)
    w1 = ffn_w1.astype(BF16)
    w2 = ffn_w2.astype(BF16)

    for l in range(depth):
        last = l == depth - 1
        mr = modrows[l]
        qkv_args = None
        if l % 2 == 1:
            cos_t, sin_t = _rope_tables(n_lat, grid_w)
            nct, nlt = bsz * n_ctx // FFN_ROW_TILE, n_lat // FFN_ROW_TILE

            def rope_tile(i):
                return (jnp.where(i < nct, 0, 1 + (i - nct) % nlt), 0)

            qkv_args = (norm_g[l, 1], attn_in_w[l // 2].astype(BF16), cos_t, sin_t, rope_tile)
        xs = _ffn(xs, mr, norm_g[l, 0], w1, w2, (l, 0), final_g, mod0=0, row_tile_of_mod=0,
                  x_ctx=ctx.reshape(bsz * n_ctx, D_MODEL) if l == 0 else None, qkv=qkv_args)
        if l % 2 == 0:
            e = l // 2
            w_lora = jnp.concatenate([
                _pack3(jnp.concatenate([rwkv_w2[e, 0], rwkv_w2[e, 1]], axis=1)),
                _pack3(jnp.concatenate([rwkv_a2[e, 0], rwkv_a2[e, 1]], axis=1)),
                _pack3(jnp.concatenate([rwkv_g2[e], jnp.zeros_like(rwkv_g2[e])], axis=1))], axis=0)
            vecs = jnp.stack([rwkv_k_k[e], rwkv_k_a[e], rwkv_r_k[e].reshape(W), rwkv_w0[e, 0], rwkv_w0[e, 1],
                              rwkv_a0[e, 0], rwkv_a0[e, 1], jnp.zeros((W,), F32)])
            ones3 = _head_ones3(W)
            pb, r, v, a, kd, bv, lw, g, bonus = _ab_in_proj(xs, mr, norm_g[l, 1], ab_in_w[e].astype(BF16), rwkv_mu[e],
                                                            w_lora, vecs, ones3, bsz, n_lat_tiles)
            yf, yb = _rwkv_scan(r, v, a, kd, bv, lw, n_ctx // RWKV_CHUNK)
            lam, bmat, cmat = _s5_discretise(s5_a_re[e], s5_a_im[e], s5_log_step[e], s5_b_re[e], s5_b_im[e],
                                             s5_c_re[e], s5_c_im[e])
            sf, sb = _s5_scan(pb, lam, bmat, cmat, n_ctx // S5_STEPS)
            vecs_out = jnp.stack([rwkv_lnx_g[e], rwkv_lnx_b[e], s5_d[e], s5_glu_b[e]] + [jnp.zeros((W,), F32)] * 4)
            xs = _ab_out(xs, mr, yf, yb, g, bonus, sf, sb, pb, vecs_out, s5_glu_w[e].astype(BF16),
                         ab_out_w[e].astype(BF16), ones3, bsz, n_lat_tiles)
            rows_mod0, attn = 0, None
        else:
            o = l // 2
            xs, qkv = xs
            att = _attention(qkv, attn_sink[o], bsz, n_lat, n_ctx)
            if not last:
                raise NotImplementedError("context update after an attention layer")
            rows_mod0, attn = n_ctx_tiles, (att, attn_out_w[o].astype(BF16))
        xs = _ffn(xs, mr, norm_g[l, 2], w1, w2, (l, 1), final_g, mod0=6, row_tile_of_mod=rows_mod0,
                  final=last, attn=attn)
    return xs.reshape(bsz, n_lat, D_MODEL)
```
